```python
import math
import jax, jax.numpy as jnp
from jax import lax
import numpy as np

D_MODEL = 1024
BATCH = 8
SEQ = 4096
DEPTH = 4

N_BRANCH = 4
D_BRANCH = 256
S5_GROUP = 16
S5_GROUPS = D_BRANCH // S5_GROUP
S5_STATE = 64
HEAD_DIM = 64
FOX_HEADS = D_BRANCH // HEAD_DIM
SB_HEADS = D_BRANCH // HEAD_DIM
Q_BLOCK = 128
CONV_WIDTH = 31
D_FF = 2816
N_EXPERTS = 8
TOP_K = 2
D_FF_EXPERT = 2816
N_DENSE = (DEPTH + 1) // 2
N_MOE = DEPTH // 2
EPS = 1e-6

N_S5_IN = D_BRANCH
N_FOX_QKV = 3 * D_BRANCH
N_FOX_F = FOX_HEADS
N_CONV_IN = 2 * D_BRANCH
N_SB_QKV = 3 * D_BRANCH
N_GATES = N_BRANCH * D_MODEL
D_PROJ = N_S5_IN + N_FOX_QKV + N_FOX_F + N_CONV_IN + N_SB_QKV + N_GATES
SPLIT_POINTS = [N_S5_IN,
                N_S5_IN + N_FOX_QKV,
                N_S5_IN + N_FOX_QKV + N_FOX_F,
                N_S5_IN + N_FOX_QKV + N_FOX_F + N_CONV_IN,
                N_S5_IN + N_FOX_QKV + N_FOX_F + N_CONV_IN + N_SB_QKV]

kernel_name = "hybrid_s5_fox_conformer_stickbreak_moe"


def rms_norm(x, g):
    xf = x.astype(jnp.float32)
    y = xf * lax.rsqrt(jnp.mean(xf * xf, axis=-1, keepdims=True) + EPS)
    return (y * g.astype(jnp.float32)).astype(x.dtype)


def layer_norm(x, g, b):
    xf = x.astype(jnp.float32)
    mu = jnp.mean(xf, axis=-1, keepdims=True)
    var = jnp.mean(jnp.square(xf - mu), axis=-1, keepdims=True)
    y = (xf - mu) * lax.rsqrt(var + EPS)
    return (y * g.astype(jnp.float32) + b.astype(jnp.float32)).astype(x.dtype)


def split_heads(t, n_heads):
    b, l, _ = t.shape
    return t.reshape(b, l, n_heads, HEAD_DIM).transpose(0, 2, 1, 3)


def merge_blocks(o):
    nb, b, h, qb, dh = o.shape
    return o.transpose(1, 0, 3, 2, 4).reshape(b, nb * qb, h * dh)


def forgetting_attention(q, k, v, log_f):
    seq = q.shape[2]
    scale = 1.0 / math.sqrt(HEAD_DIM)
    cum = jnp.cumsum(log_f, axis=-1)
    kpos = jnp.arange(seq)

    def block(i):
        start = i * Q_BLOCK
        qb = lax.dynamic_slice_in_dim(q, start, Q_BLOCK, axis=2)
        cq = lax.dynamic_slice_in_dim(cum, start, Q_BLOCK, axis=2)
        qpos = start + jnp.arange(Q_BLOCK)
        s = jnp.einsum("bhqd,bhkd->bhqk", qb, k).astype(jnp.float32) * scale
        s = s + cq[..., :, None] - cum[..., None, :]
        causal = kpos[None, :] <= qpos[:, None]
        s = jnp.where(causal, s, -jnp.inf)
        p = jax.nn.softmax(s, axis=-1)
        return jnp.einsum("bhqk,bhkd->bhqd", p.astype(v.dtype), v)

    return merge_blocks(lax.map(block, jnp.arange(seq // Q_BLOCK)))


def stick_breaking_attention(q, k, v):
    seq = q.shape[2]
    scale = 1.0 / math.sqrt(HEAD_DIM)
    kpos = jnp.arange(seq)

    def block(i):
        start = i * Q_BLOCK
        qb = lax.dynamic_slice_in_dim(q, start, Q_BLOCK, axis=2)
        qpos = start + jnp.arange(Q_BLOCK)
        z = jnp.einsum("bhqd,bhkd->bhqk", qb, k).astype(jnp.float32) * scale
        past = kpos[None, :] < qpos[:, None]
        log_beta = jax.nn.log_sigmoid(z)
        log_keep = jnp.where(past, jax.nn.log_sigmoid(-z), 0.0)
        after = lax.cumsum(log_keep, axis=3, reverse=True) - log_keep
        a = jnp.where(past, jnp.exp(log_beta + after), 0.0)
        return jnp.einsum("bhqk,bhkd->bhqd", a.astype(v.dtype), v)

    return merge_blocks(lax.map(block, jnp.arange(seq // Q_BLOCK)))


def _s5_combine(e1, e2):
    a1r, a1i, b1r, b1i = e1
    a2r, a2i, b2r, b2i = e2
    return (a2r * a1r - a2i * a1i,
            a2r * a1i + a2i * a1r,
            a2r * b1r - a2i * b1i + b2r,
            a2r * b1i + a2i * b1r + b2i)


def s5_ssm(u, lam_re, lam_im, log_dt, b_re, b_im, c_re, c_im, d_skip):
    bsz, seq, _ = u.shape
    f32 = jnp.float32
    uf = u.astype(f32)
    ug = uf.reshape(bsz, seq, S5_GROUPS, S5_GROUP)
    lam_re, lam_im = lam_re.astype(f32), lam_im.astype(f32)
    dt = jnp.exp(log_dt.astype(f32))[:, None]
    mag = jnp.exp(lam_re * dt)
    ab_re = mag * jnp.cos(lam_im * dt)
    ab_im = mag * jnp.sin(lam_im * dt)
    nr, ni = ab_re - 1.0, ab_im
    den = lam_re * lam_re + lam_im * lam_im
    k_re = (nr * lam_re + ni * lam_im) / den
    k_im = (ni * lam_re - nr * lam_im) / den
    b_re, b_im = b_re.astype(f32), b_im.astype(f32)
    bb_re = k_re[..., None] * b_re - k_im[..., None] * b_im
    bb_im = k_re[..., None] * b_im + k_im[..., None] * b_re
    bu_re = jnp.einsum("blgh,gph->blgp", ug, bb_re)
    bu_im = jnp.einsum("blgh,gph->blgp", ug, bb_im)
    a_re = jnp.broadcast_to(ab_re, bu_re.shape)
    a_im = jnp.broadcast_to(ab_im, bu_im.shape)
    _, _, s_re, s_im = lax.associative_scan(_s5_combine, (a_re, a_im, bu_re, bu_im), axis=1)
    y = (jnp.einsum("blgp,ghp->blgh", s_re, c_re.astype(f32))
         - jnp.einsum("blgp,ghp->blgh", s_im, c_im.astype(f32)))
    y = y.reshape(bsz, seq, D_BRANCH) + d_skip.astype(f32) * uf
    return y.astype(u.dtype)


def conformer_conv(ab, w_dw, b_dw, ln_g, ln_b):
    a, b = jnp.split(ab, 2, axis=-1)
    u = a * jax.nn.sigmoid(b)
    u = lax.conv_general_dilated(u, w_dw[:, None, :], window_strides=(1,),
                                 padding=[(CONV_WIDTH - 1, 0)],
                                 dimension_numbers=("NWC", "WIO", "NWC"),
                                 feature_group_count=D_BRANCH) + b_dw
    u = layer_norm(u, ln_g, ln_b)
    return jax.nn.silu(u)


def swiglu(h, w1, w3, w2):
    return (jax.nn.silu(h @ w1) * (h @ w3)) @ w2


def moe_ffn(h, w_router, b_router, w1, w3, w2):
    bsz, seq, d = h.shape
    ht = h.reshape(bsz * seq, d)
    logits = ht.astype(jnp.float32) @ w_router.astype(jnp.float32) + b_router.astype(jnp.float32)
    top_val, top_idx = lax.top_k(logits, TOP_K)
    top_w = jax.nn.softmax(top_val, axis=-1)
    combine = jnp.sum(jax.nn.one_hot(top_idx, N_EXPERTS, dtype=jnp.float32) * top_w[..., None], axis=1)
    combine = combine.astype(h.dtype)
    y = jnp.zeros_like(ht)
    for e in range(N_EXPERTS):
        y = y + combine[:, e:e + 1] * swiglu(ht, w1[e], w3[e], w2[e])
    return y.reshape(bsz, seq, d)


def setup_inputs(seed: int = 0) -> dict:
    key = jax.random.key(seed)
    ks = jax.random.split(key, 40)
    f32 = jnp.float32

    def nrm(k, shape, scale):
        return jax.random.normal(k, shape, f32) * scale

    def gain(k, shape):
        return 1.0 + nrm(k, shape, 0.02)

    G, P, H = S5_GROUPS, S5_STATE, S5_GROUP
    lam_im_init = jnp.pi * jnp.arange(P, dtype=f32)
    return {
        "x": nrm(ks[0], (BATCH, SEQ, D_MODEL), 1.0),
        "g_mix": gain(ks[1], (DEPTH, D_MODEL)),
        "w_in": nrm(ks[2], (DEPTH, D_MODEL, D_PROJ), D_MODEL ** -0.5),
        "b_forget": jax.random.uniform(ks[3], (DEPTH, FOX_HEADS), f32, 1.0, 4.0),
        "fox_q_gain": gain(ks[4], (DEPTH, HEAD_DIM)),
        "fox_k_gain": gain(ks[5], (DEPTH, HEAD_DIM)),
        "sb_q_gain": gain(ks[6], (DEPTH, HEAD_DIM)),
        "sb_k_gain": gain(ks[7], (DEPTH, HEAD_DIM)),
        "s5_lam_re": -0.5 + nrm(ks[8], (DEPTH, G, P), 0.01),
        "s5_lam_im": lam_im_init + nrm(ks[9], (DEPTH, G, P), 0.01),
        "s5_log_dt": jax.random.uniform(ks[10], (DEPTH, G), f32, math.log(1e-3), math.log(1e-1)),
        "s5_b_re": nrm(ks[11], (DEPTH, G, P, H), (2 * H) ** -0.5),
        "s5_b_im": nrm(ks[12], (DEPTH, G, P, H), (2 * H) ** -0.5),
        "s5_c_re": nrm(ks[13], (DEPTH, G, H, P), P ** -0.5),
        "s5_c_im": nrm(ks[14], (DEPTH, G, H, P), P ** -0.5),
        "s5_d": nrm(ks[15], (DEPTH, D_BRANCH), 1.0),
        "s5_w_glu": nrm(ks[16], (DEPTH, D_BRANCH, D_BRANCH), D_BRANCH ** -0.5),
        "conv_w": nrm(ks[17], (DEPTH, CONV_WIDTH, D_BRANCH), CONV_WIDTH ** -0.5),
        "conv_b": nrm(ks[18], (DEPTH, D_BRANCH), 0.02),
        "conv_ln_g": gain(ks[19], (DEPTH, D_BRANCH)),
        "conv_ln_b": nrm(ks[20], (DEPTH, D_BRANCH), 0.02),
        "w_branch": nrm(ks[21], (DEPTH, N_BRANCH, D_BRANCH, D_MODEL), D_BRANCH ** -0.5),
        "w_out": nrm(ks[22], (DEPTH, D_MODEL, D_MODEL), D_MODEL ** -0.5),
        "g_ffn": gain(ks[23], (DEPTH, D_MODEL)),
        "ffn_w1": nrm(ks[24], (N_DENSE, D_MODEL, D_FF), D_MODEL ** -0.5),
        "ffn_w3": nrm(ks[25], (N_DENSE, D_MODEL, D_FF), D_MODEL ** -0.5),
        "ffn_w2": nrm(ks[26], (N_DENSE, D_FF, D_MODEL), D_FF ** -0.5),
        "router_w": nrm(ks[27], (N_MOE, D_MODEL, N_EXPERTS), D_MODEL ** -0.5),
        "router_b": nrm(ks[28], (N_MOE, N_EXPERTS), 0.01),
        "moe_w1": nrm(ks[29], (N_MOE, N_EXPERTS, D_MODEL, D_FF_EXPERT), D_MODEL ** -0.5),
        "moe_w3": nrm(ks[30], (N_MOE, N_EXPERTS, D_MODEL, D_FF_EXPERT), D_MODEL ** -0.5),
        "moe_w2": nrm(ks[31], (N_MOE, N_EXPERTS, D_FF_EXPERT, D_MODEL), D_FF_EXPERT ** -0.5),
    }


def reference(x, g_mix, w_in, b_forget, fox_q_gain, fox_k_gain, sb_q_gain, sb_k_gain,
              s5_lam_re, s5_lam_im, s5_log_dt, s5_b_re, s5_b_im, s5_c_re, s5_c_im, s5_d, s5_w_glu,
              conv_w, conv_b, conv_ln_g, conv_ln_b, w_branch, w_out, g_ffn,
              ffn_w1, ffn_w3, ffn_w2, router_w, router_b, moe_w1, moe_w3, moe_w2):
    bsz, seq = x.shape[0], x.shape[1]
    for i in range(DEPTH):
        h = rms_norm(x, g_mix[i])
        proj = h @ w_in[i]
        u_s5, qkv_f, f_logit, conv_ab, qkv_sb, gate_logit = jnp.split(proj, SPLIT_POINTS, axis=-1)

        y_s5 = s5_ssm(u_s5, s5_lam_re[i], s5_lam_im[i], s5_log_dt[i], s5_b_re[i], s5_b_im[i],
                      s5_c_re[i], s5_c_im[i], s5_d[i])
        y_s5 = jax.nn.gelu(y_s5)
        y_s5 = y_s5 * jax.nn.sigmoid(y_s5 @ s5_w_glu[i])

        qf, kf, vf = [split_heads(t, FOX_HEADS) for t in jnp.split(qkv_f, 3, axis=-1)]
        qf, kf = rms_norm(qf, fox_q_gain[i]), rms_norm(kf, fox_k_gain[i])
        log_f = jax.nn.log_sigmoid(f_logit.astype(jnp.float32) + b_forget[i].astype(jnp.float32))
        y_fox = forgetting_attention(qf, kf, vf, log_f.transpose(0, 2, 1))

        y_conv = conformer_conv(conv_ab, conv_w[i], conv_b[i], conv_ln_g[i], conv_ln_b[i])

        qs, ks_, vs = [split_heads(t, SB_HEADS) for t in jnp.split(qkv_sb, 3, axis=-1)]
        qs, ks_ = rms_norm(qs, sb_q_gain[i]), rms_norm(ks_, sb_k_gain[i])
        y_sb = stick_breaking_attention(qs, ks_, vs)

        ys = jnp.stack([y_s5.astype(x.dtype), y_fox.astype(x.dtype),
                        y_conv.astype(x.dtype), y_sb.astype(x.dtype)], axis=2)
        gates = jax.nn.sigmoid(gate_logit.reshape(bsz, seq, N_BRANCH, D_MODEL))
        merged = jnp.sum(gates * jnp.einsum("blnc,ncd->blnd", ys, w_branch[i]), axis=2)
        x = x + merged @ w_out[i]

        h = rms_norm(x, g_ffn[i])
        j = i // 2
        if i % 2 == 0:
            ffn = swiglu(h, ffn_w1[j], ffn_w3[j], ffn_w2[j])
        else:
            ffn = moe_ffn(h, router_w[j], router_b[j], moe_w1[j], moe_w3[j], moe_w2[j])
        x = x + ffn
    return x
```

```python
import functools
import math

import jax
import jax.numpy as jnp
from jax import lax
from jax.experimental import pallas as pl
from jax.experimental.pallas import tpu as pltpu

F32 = jnp.float32
BF16 = jnp.bfloat16
EPS = 1e-6

D_BRANCH = 256
HEAD_DIM = 64
N_HEADS = D_BRANCH // HEAD_DIM
S5_GROUP = 16
S5_GROUPS = D_BRANCH // S5_GROUP
S5_STATE = 64
N_STATE = S5_GROUPS * S5_STATE
CONV_WIDTH = 31
CONV_HALO = 32
N_EXPERTS = 8
LANES = 128
SUBLANES = 8
VMEM_LIMIT = 56 * 1024 * 1024

TM_PROJ = 512
TM_MERGE = 512
TM_FFN = 512
TQ_ATT = 512
TK_ATT = 256
LC_S5 = 128
LC_CONV = 512

NEG_INF = float("-inf")


def _cparams(sem):
    return pltpu.CompilerParams(dimension_semantics=sem, vmem_limit_bytes=VMEM_LIMIT)


def _dot(a, b):
    return jnp.dot(a, b, preferred_element_type=F32)


def _dot_nt(a, b):
    return lax.dot_general(a, b, (((1,), (1,)), ((), ())), preferred_element_type=F32)


def _split2(x):
    hi = x.astype(BF16)
    lo = (x - hi.astype(F32)).astype(BF16)
    return hi, lo


def _rmsnorm(x, g):
    ms = jnp.mean(x * x, axis=-1, keepdims=True)
    return x * lax.rsqrt(ms + EPS) * g


def _sigmoid(x):
    return 1.0 / (1.0 + jnp.exp(-x))


def _log_sigmoid(x):
    return jnp.minimum(x, 0.0) - jnp.log(1.0 + jnp.exp(-jnp.abs(x)))


def _inproj_kernel(x_ref, g_ref, w_ref, wf_ref, bf_ref, gain_ref,
                   us5_ref, qkvf_ref, conv_ref, qkvs_ref, cum_ref, carry_ref, *, tm):
    li = pl.program_id(1)
    x = x_ref[0]
    h = _rmsnorm(x, g_ref[...]).astype(BF16)

    def proj(a, b):
        return _dot(h, w_ref[:, a:b])

    r = lax.broadcasted_iota(jnp.int32, (D_BRANCH, D_BRANCH), 0) // HEAD_DIM
    c = lax.broadcasted_iota(jnp.int32, (D_BRANCH, D_BRANCH), 1) // HEAD_DIM
    ones_bd = jnp.where(r == c, 1.0, 0.0).astype(BF16)

    def qknorm(t, gi):
        hi, lo = _split2(t * t)
        ss = _dot(hi, ones_bd) + _dot(lo, ones_bd)
        return (t * lax.rsqrt(ss * (1.0 / HEAD_DIM) + EPS) * gain_ref[gi:gi + 1, :]).astype(BF16)

    us5_ref[...] = proj(0, 256)
    qkvf_ref[0, :, 0:256] = qknorm(proj(256, 512), 0)
    qkvf_ref[0, :, 256:512] = qknorm(proj(512, 768), 1)
    qkvf_ref[0, :, 512:768] = proj(768, 1024).astype(BF16)
    conv_ref[0] = proj(1024, 1536)
    qkvs_ref[0, :, 0:256] = qknorm(proj(1536, 1792), 2)
    qkvs_ref[0, :, 256:512] = qknorm(proj(1792, 2048), 3)
    qkvs_ref[0, :, 512:768] = proj(2048, 2304).astype(BF16)

    @pl.when(li == 0)
    def _():
        carry_ref[...] = jnp.zeros_like(carry_ref)

    lf = _log_sigmoid(_dot(h, wf_ref[...]) + bf_ref[...])
    rr = lax.broadcasted_iota(jnp.int32, (tm, tm), 0)
    cc = lax.broadcasted_iota(jnp.int32, (tm, tm), 1)
    tri = jnp.where(cc <= rr, 1.0, 0.0).astype(BF16)
    hi = lf.astype(BF16)
    r1 = lf - hi.astype(F32)
    mid = r1.astype(BF16)
    lo = (r1 - mid.astype(F32)).astype(BF16)
    cum = _dot(tri, hi) + _dot(tri, mid) + _dot(tri, lo) + carry_ref[...]
    cum_ref[0] = cum
    carry_ref[...] = cum[tm - 1:tm, :]


def _inproj(x, g, w, wf, bfg, gains):
    B, L, D = x.shape
    tm = min(TM_PROJ, L)
    n = w.shape[1]
    const = lambda b, l: (0, 0)
    return pl.pallas_call(
        functools.partial(_inproj_kernel, tm=tm),
        grid=(B, L // tm),
        in_specs=[
            pl.BlockSpec((1, tm, D), lambda b, l: (b, l, 0)),
            pl.BlockSpec((1, D), const),
            pl.BlockSpec((D, n), const),
            pl.BlockSpec((D, LANES), const),
            pl.BlockSpec((1, LANES), const),
            pl.BlockSpec((4, D_BRANCH), const),
        ],
        out_specs=[
            pl.BlockSpec((tm, D_BRANCH), lambda b, l: (l, b)),
            pl.BlockSpec((1, tm, 3 * D_BRANCH), lambda b, l: (b, l, 0)),
            pl.BlockSpec((1, tm, 2 * D_BRANCH), lambda b, l: (b, l, 0)),
            pl.BlockSpec((1, tm, 3 * D_BRANCH), lambda b, l: (b, l, 0)),
            pl.BlockSpec((1, tm, LANES), lambda b, l: (b, l, 0)),
        ],
        out_shape=[
            jax.ShapeDtypeStruct((L, B * D_BRANCH), F32),
            jax.ShapeDtypeStruct((B, L, 3 * D_BRANCH), BF16),
            jax.ShapeDtypeStruct((B, L, 2 * D_BRANCH), F32),
            jax.ShapeDtypeStruct((B, L, 3 * D_BRANCH), BF16),
            jax.ShapeDtypeStruct((B, L, LANES), F32),
        ],
        scratch_shapes=[pltpu.VMEM((1, LANES), F32)],
        compiler_params=_cparams(("parallel", "arbitrary")),
        name="inproj",
    )(x, g, w, wf, bfg, gains)


def _s5_kernel(u_ref, bmat_ref, cmat_ref, a_ref, d_ref, wglu_ref, y_ref, bu_ref, st_ref, *, lc, nb):
    ci = pl.program_id(0)

    @pl.when(ci == 0)
    def _():
        st_ref[...] = jnp.zeros_like(st_ref)

    u = u_ref[...]
    bu_ref[...] = _dot(u.astype(BF16), bmat_ref[...])

    a_re = jnp.broadcast_to(a_ref[0:1, :], (nb, N_STATE))
    a_im = jnp.broadcast_to(a_ref[1:2, :], (nb, N_STATE))

    def step(t, carry):
        s_re, s_im = carry
        r0 = pl.multiple_of(t * nb, nb)
        b_re = bu_ref[pl.ds(r0, nb), 0:N_STATE]
        b_im = bu_ref[pl.ds(r0, nb), N_STATE:2 * N_STATE]
        n_re = a_re * s_re - a_im * s_im + b_re
        n_im = a_re * s_im + a_im * s_re + b_im
        bu_ref[pl.ds(r0, nb), 0:N_STATE] = n_re
        bu_ref[pl.ds(r0, nb), N_STATE:2 * N_STATE] = n_im
        return n_re, n_im

    s_re, s_im = lax.fori_loop(0, lc, step, (st_ref[:, 0:N_STATE], st_ref[:, N_STATE:2 * N_STATE]),
                               unroll=2)
    st_ref[:, 0:N_STATE] = s_re
    st_ref[:, N_STATE:2 * N_STATE] = s_im

    y = _dot(bu_ref[...].astype(BF16), cmat_ref[...]) + d_ref[...] * u
    y = jax.nn.gelu(y, approximate=True)
    y = y * _sigmoid(_dot(y.astype(BF16), wglu_ref[...]))
    y_ref[...] = y.astype(y_ref.dtype)


def _s5(u2, bmat, cmat, a, d, wglu, nb):
    rows = u2.shape[0]
    L = rows // nb
    lc = min(LC_S5, L)
    const = lambda c: (0, 0)
    return pl.pallas_call(
        functools.partial(_s5_kernel, lc=lc, nb=nb),
        grid=(L // lc,),
        in_specs=[
            pl.BlockSpec((lc * nb, D_BRANCH), lambda c: (c, 0)),
            pl.BlockSpec((D_BRANCH, 2 * N_STATE), const),
            pl.BlockSpec((2 * N_STATE, D_BRANCH), const),
            pl.BlockSpec((2, N_STATE), const),
            pl.BlockSpec((1, D_BRANCH), const),
            pl.BlockSpec((D_BRANCH, D_BRANCH), const),
        ],
        out_specs=pl.BlockSpec((lc * nb, D_BRANCH), lambda c: (c, 0)),
        out_shape=jax.ShapeDtypeStruct((rows, D_BRANCH), BF16),
        scratch_shapes=[pltpu.VMEM((lc * nb, 2 * N_STATE), F32), pltpu.VMEM((nb, 2 * N_STATE), F32)],
        compiler_params=_cparams(("arbitrary",)),
        name="s5",
    )(u2, bmat, cmat, a, d, wglu)


def _head_split(q2, tq):
    lane = lax.broadcasted_iota(jnp.int32, (tq, LANES), 1)
    zero = jnp.zeros_like(q2)
    return lane, (jnp.where(lane < HEAD_DIM, q2, zero), jnp.where(lane >= HEAD_DIM, q2, zero))


def _fox_kernel(q_ref, k_ref, v_ref, cq_ref, ck_ref, o_ref, *, tq, tk):
    qi = pl.program_id(2)
    ratio = tq // tk
    lane, qh = _head_split(q_ref[0], tq)
    cq = (cq_ref[0, 0, :, 0:1], cq_ref[0, 0, :, 1:2])
    row = lax.broadcasted_iota(jnp.int32, (tq, tk), 0)
    col = lax.broadcasted_iota(jnp.int32, (tq, tk), 1)

    def tile(ki, carry, j):
        k0 = pl.multiple_of(ki * tk, tk)
        kt = k_ref[0, pl.ds(k0, tk), :]
        vt = v_ref[0, pl.ds(k0, tk), :]
        new = []
        for h in range(2):
            m, l, acc = carry[h]
            s = _dot_nt(qh[h], kt) + cq[h] - ck_ref[0, 0, h:h + 1, pl.ds(k0, tk)]
            if j is not None:
                s = jnp.where(col + j * tk <= row, s, NEG_INF)
            m_new = jnp.maximum(m, jnp.max(s, axis=1, keepdims=True))
            alpha = jnp.exp(m - m_new)
            p = jnp.exp(s - m_new)
            l = alpha * l + jnp.sum(p, axis=1, keepdims=True)
            acc = alpha * acc + _dot(p.astype(BF16), vt)
            new.append((m_new, l, acc))
        return tuple(new)

    init = tuple((jnp.full((tq, 1), NEG_INF, F32), jnp.zeros((tq, 1), F32), jnp.zeros((tq, LANES), F32))
                 for _ in range(2))
    n_full = qi * ratio
    carry = lax.fori_loop(0, n_full, lambda ki, c: tile(ki, c, None), init)
    for j in range(ratio):
        carry = tile(n_full + j, carry, j)
    (_, l0, a0), (_, l1, a1) = carry
    o_ref[0] = jnp.where(lane < HEAD_DIM, a0 / l0, a1 / l1).astype(o_ref.dtype)


def _fox(qkv, cq, ck):
    B, L, _ = qkv.shape
    tq = min(TQ_ATT, L)
    tk = min(TK_ATT, tq)
    return pl.pallas_call(
        functools.partial(_fox_kernel, tq=tq, tk=tk),
        grid=(B, 2, L // tq),
        in_specs=[
            pl.BlockSpec((1, tq, LANES), lambda b, p, i: (b, i, p)),
            pl.BlockSpec((1, L, LANES), lambda b, p, i: (b, 0, 2 + p)),
            pl.BlockSpec((1, L, LANES), lambda b, p, i: (b, 0, 4 + p)),
            pl.BlockSpec((1, 1, tq, 2), lambda b, p, i: (b, p, i, 0)),
            pl.BlockSpec((1, 1, SUBLANES, L), lambda b, p, i: (b, p, 0, 0)),
        ],
        out_specs=pl.BlockSpec((1, tq, LANES), lambda b, p, i: (b, i, p)),
        out_shape=jax.ShapeDtypeStruct((B, L, D_BRANCH), BF16),
        compiler_params=_cparams(("parallel", "parallel", "arbitrary")),
        name="fox",
    )(qkv, qkv, qkv, cq, ck)


def _sb_kernel(q_ref, k_ref, v_ref, o_ref, *, tq, tk):
    qi = pl.program_id(2)
    ratio = tq // tk
    lane, qh = _head_split(q_ref[0], tq)
    row = lax.broadcasted_iota(jnp.int32, (tq, tk), 0)
    col = lax.broadcasted_iota(jnp.int32, (tq, tk), 1)
    ur = lax.broadcasted_iota(jnp.int32, (tk, tk), 0)
    uc = lax.broadcasted_iota(jnp.int32, (tk, tk), 1)
    upper = jnp.where(ur > uc, 1.0, 0.0).astype(BF16)

    def tile(ki, carry, j):
        k0 = pl.multiple_of(ki * tk, tk)
        kt = k_ref[0, pl.ds(k0, tk), :]
        vt = v_ref[0, pl.ds(k0, tk), :]
        new = []
        for h in range(2):
            csum, acc = carry[h]
            z = _dot_nt(qh[h], kt)
            l1p = jnp.log(1.0 + jnp.exp(-jnp.abs(z)))
            log_beta = jnp.minimum(z, 0.0) - l1p
            log_keep = -jnp.maximum(z, 0.0) - l1p
            if j is not None:
                past = col + j * tk < row
                log_keep = jnp.where(past, log_keep, 0.0)
            hi, lo = _split2(log_keep)
            after = _dot(hi, upper) + _dot(lo, upper)
            a = jnp.exp(log_beta + after + csum)
            if j is not None:
                a = jnp.where(past, a, 0.0)
            acc = acc + _dot(a.astype(BF16), vt)
            csum = csum + jnp.sum(log_keep, axis=1, keepdims=True)
            new.append((csum, acc))
        return tuple(new)

    carry = tuple((jnp.zeros((tq, 1), F32), jnp.zeros((tq, LANES), F32)) for _ in range(2))
    n_full = qi * ratio
    for j in reversed(range(ratio)):
        carry = tile(n_full + j, carry, j)
    carry = lax.fori_loop(0, n_full, lambda i, c: tile(n_full - 1 - i, c, None), carry)
    (_, a0), (_, a1) = carry
    o_ref[0] = jnp.where(lane < HEAD_DIM, a0, a1).astype(o_ref.dtype)


def _sb(qkv):
    B, L, _ = qkv.shape
    tq = min(TQ_ATT, L)
    tk = min(TK_ATT, tq)
    return pl.pallas_call(
        functools.partial(_sb_kernel, tq=tq, tk=tk),
        grid=(B, 2, L // tq),
        in_specs=[
            pl.BlockSpec((1, tq, LANES), lambda b, p, i: (b, i, p)),
            pl.BlockSpec((1, L, LANES), lambda b, p, i: (b, 0, 2 + p)),
            pl.BlockSpec((1, L, LANES), lambda b, p, i: (b, 0, 4 + p)),
        ],
        out_specs=pl.BlockSpec((1, tq, LANES), lambda b, p, i: (b, i, p)),
        out_shape=jax.ShapeDtypeStruct((B, L, D_BRANCH), BF16),
        compiler_params=_cparams(("parallel", "parallel", "arbitrary")),
        name="stickbreak",
    )(qkv, qkv, qkv)


def _conv_kernel(ab_ref, w_ref, b_ref, g_ref, beta_ref, o_ref, pad_ref, *, lc):
    li = pl.program_id(1)

    @pl.when(li == 0)
    def _():
        pad_ref[0:CONV_HALO, :] = jnp.zeros((CONV_HALO, D_BRANCH), F32)

    ab = ab_ref[0]
    pad_ref[CONV_HALO:CONV_HALO + lc, :] = ab[:, 0:D_BRANCH] * _sigmoid(ab[:, D_BRANCH:2 * D_BRANCH])
    off = CONV_HALO - (CONV_WIDTH - 1)
    acc = jnp.zeros((lc, D_BRANCH), F32) + b_ref[...]
    for j in range(CONV_WIDTH):
        acc = acc + w_ref[j:j + 1, :] * pad_ref[off + j:off + j + lc, :]
    pad_ref[0:CONV_HALO, :] = pad_ref[lc:lc + CONV_HALO, :]
    mu = jnp.mean(acc, axis=-1, keepdims=True)
    xc = acc - mu
    var = jnp.mean(xc * xc, axis=-1, keepdims=True)
    y = xc * lax.rsqrt(var + EPS) * g_ref[...] + beta_ref[...]
    o_ref[0] = (y * _sigmoid(y)).astype(o_ref.dtype)


def _conv(ab, w, b, g, beta):
    B, L, _ = ab.shape
    lc = min(LC_CONV, L)
    const = lambda b_, l: (0, 0)
    return pl.pallas_call(
        functools.partial(_conv_kernel, lc=lc),
        grid=(B, L // lc),
        in_specs=[
            pl.BlockSpec((1, lc, 2 * D_BRANCH), lambda b_, l: (b_, l, 0)),
            pl.BlockSpec((CONV_WIDTH, D_BRANCH), const),
            pl.BlockSpec((1, D_BRANCH), const),
            pl.BlockSpec((1, D_BRANCH), const),
            pl.BlockSpec((1, D_BRANCH), const),
        ],
        out_specs=pl.BlockSpec((1, lc, D_BRANCH), lambda b_, l: (b_, l, 0)),
        out_shape=jax.ShapeDtypeStruct((B, L, D_BRANCH), BF16),
        scratch_shapes=[pltpu.VMEM((CONV_HALO + lc, D_BRANCH), F32)],
        compiler_params=_cparams(("parallel", "arbitrary")),
        name="conv",
    )(ab, w, b, g, beta)


def _merge_kernel(x_ref, g_ref, ys5_ref, yfox_ref, yconv_ref, ysb_ref, wg_ref, wb_ref, wo_ref, o_ref):
    x = x_ref[0]
    D = x.shape[-1]
    h = _rmsnorm(x, g_ref[...]).astype(BF16)
    ys = (ys5_ref[...], yfox_ref[0], yconv_ref[0], ysb_ref[0])
    merged = None
    for n in range(4):
        gate = _sigmoid(_dot(h, wg_ref[:, n * D:(n + 1) * D]))
        term = gate * _dot(ys[n], wb_ref[n])
        merged = term if merged is None else merged + term
    o_ref[0] = x + _dot(merged.astype(BF16), wo_ref[...])


def _merge(x, g, ys5, yfox, yconv, ysb, wg, wb, wo):
    B, L, D = x.shape
    tm = min(TM_MERGE, L)
    c2 = lambda b, l: (0, 0)
    yspec = pl.BlockSpec((1, tm, D_BRANCH), lambda b, l: (b, l, 0))
    return pl.pallas_call(
        _merge_kernel,
        grid=(B, L // tm),
        in_specs=[
            pl.BlockSpec((1, tm, D), lambda b, l: (b, l, 0)),
            pl.BlockSpec((1, D), c2),
            pl.BlockSpec((tm, D_BRANCH), lambda b, l: (l, b)),
            yspec, yspec, yspec,
            pl.BlockSpec((D, 4 * D), c2),
            pl.BlockSpec((4, D_BRANCH, D), lambda b, l: (0, 0, 0)),
            pl.BlockSpec((D, D), c2),
        ],
        out_specs=pl.BlockSpec((1, tm, D), lambda b, l: (b, l, 0)),
        out_shape=jax.ShapeDtypeStruct((B, L, D), F32),
        compiler_params=_cparams(("parallel", "parallel")),
        name="merge",
    )(x, g, ys5, yfox, yconv, ysb, wg, wb, wo)


def _ffn_kernel(x_ref, g_ref, wr_ref, br_ref, w1_ref, w3_ref, w2_ref, o_ref, h_ref, acc_ref, comb_ref,
                *, moe, n_e, n_f):
    e = pl.program_id(1)
    f = pl.program_id(2)

    @pl.when((e == 0) & (f == 0))
    def _():
        h = _rmsnorm(x_ref[...], g_ref[...])
        h_ref[...] = h.astype(BF16)
        acc_ref[...] = jnp.zeros_like(acc_ref)
        if moe:
            logits = jnp.dot(h, wr_ref[...], preferred_element_type=F32,
                             precision=lax.Precision.HIGHEST) + br_ref[...]
            lane = lax.broadcasted_iota(jnp.int32, logits.shape, 1).astype(F32)
            m1 = jnp.max(logits, axis=1, keepdims=True)
            i1 = jnp.min(jnp.where(logits == m1, lane, float(LANES)), axis=1, keepdims=True)
            rest = jnp.where(lane == i1, NEG_INF, logits)
            m2 = jnp.max(rest, axis=1, keepdims=True)
            i2 = jnp.min(jnp.where(rest == m2, lane, float(LANES)), axis=1, keepdims=True)
            e2 = jnp.exp(m2 - m1)
            p1 = 1.0 / (1.0 + e2)
            comb_ref[...] = jnp.where(lane == i1, p1, 0.0) + jnp.where(lane == i2, e2 * p1, 0.0)

    h = h_ref[...]
    a = _dot(h, w1_ref[0])
    b = _dot(h, w3_ref[0])
    gated = a * _sigmoid(a) * b
    if moe:
        lane = lax.broadcasted_iota(jnp.int32, comb_ref.shape, 1)
        ce = jnp.sum(jnp.where(lane == e, comb_ref[...], 0.0), axis=1, keepdims=True)
        gated = gated * ce
    acc_ref[...] += _dot(gated.astype(BF16), w2_ref[0])

    @pl.when((e == n_e - 1) & (f == n_f - 1))
    def _():
        o_ref[...] = x_ref[...] + acc_ref[...]


def _ffn(x2, g, wr, br, w1, w3, w2, moe):
    T, D = x2.shape
    n_e, _, F = w1.shape
    tm = min(TM_FFN, T)
    n_f = 2
    fc = F // n_f
    c2 = lambda i, e, f: (0, 0)
    return pl.pallas_call(
        functools.partial(_ffn_kernel, moe=moe, n_e=n_e, n_f=n_f),
        grid=(T // tm, n_e, n_f),
        in_specs=[
            pl.BlockSpec((tm, D), lambda i, e, f: (i, 0)),
            pl.BlockSpec((1, D), c2),
            pl.BlockSpec((D, LANES), c2),
            pl.BlockSpec((1, LANES), c2),
            pl.BlockSpec((1, D, fc), lambda i, e, f: (e, 0, f)),
            pl.BlockSpec((1, D, fc), lambda i, e, f: (e, 0, f)),
            pl.BlockSpec((1, fc, D), lambda i, e, f: (e, f, 0)),
        ],
        out_specs=pl.BlockSpec((tm, D), lambda i, e, f: (i, 0)),
        out_shape=jax.ShapeDtypeStruct((T, D), F32),
        scratch_shapes=[pltpu.VMEM((tm, D), BF16), pltpu.VMEM((tm, D), F32), pltpu.VMEM((tm, LANES), F32)],
        compiler_params=_cparams(("parallel", "arbitrary", "arbitrary")),
        name="moe_ffn" if moe else "dense_ffn",
    )(x2, g, wr, br, w1, w3, w2)


def _s5_params(lam_re, lam_im, log_dt, b_re, b_im, c_re, c_im):
    G, P, H = S5_GROUPS, S5_STATE, S5_GROUP
    dt = jnp.exp(log_dt)[:, None]
    mag = jnp.exp(lam_re * dt)
    ab_re = mag * jnp.cos(lam_im * dt)
    ab_im = mag * jnp.sin(lam_im * dt)
    nr, ni = ab_re - 1.0, ab_im
    den = lam_re * lam_re + lam_im * lam_im
    k_re = (nr * lam_re + ni * lam_im) / den
    k_im = (ni * lam_re - nr * lam_im) / den
    bb_re = k_re[..., None] * b_re - k_im[..., None] * b_im
    bb_im = k_re[..., None] * b_im + k_im[..., None] * b_re
    eye = jnp.eye(G, dtype=F32)
    bm_re = jnp.einsum("gph,gk->ghkp", bb_re, eye).reshape(G * H, G * P)
    bm_im = jnp.einsum("gph,gk->ghkp", bb_im, eye).reshape(G * H, G * P)
    bmat = jnp.concatenate([bm_re, bm_im], axis=1).astype(BF16)
    cm_re = jnp.einsum("ghp,gk->kpgh", c_re, eye).reshape(G * P, G * H)
    cm_im = jnp.einsum("ghp,gk->kpgh", c_im, eye).reshape(G * P, G * H)
    cmat = jnp.concatenate([cm_re, -cm_im], axis=0).astype(BF16)
    a = jnp.stack([ab_re.reshape(-1), ab_im.reshape(-1)], axis=0)
    return bmat, cmat, a


def kernel(x, g_mix, w_in, b_forget, fox_q_gain, fox_k_gain, sb_q_gain, sb_k_gain, s5_lam_re, s5_lam_im, s5_log_dt, s5_b_re, s5_b_im, s5_c_re, s5_c_im, s5_d, s5_w_glu, conv_w, conv_b, conv_ln_g, conv_ln_b, w_branch, w_out, g_ffn, ffn_w1, ffn_w3, ffn_w2, router_w, router_b, moe_w1, moe_w3, moe_w2):
    B, L, D = x.shape
    depth = g_mix.shape[0]
    assert B == SUBLANES, "the S5 scan lays the batch along the sublanes"
    n_qkv = 3 * D_BRANCH
    o_f = D_BRANCH + n_qkv
    o_c = o_f + N_HEADS
    o_g = o_c + 2 * D_BRANCH + n_qkv
    scale = 1.0 / math.sqrt(HEAD_DIM)

    for i in range(depth):
        wi = w_in[i]
        w_main = jnp.concatenate([wi[:, :o_f], wi[:, o_c:o_g]], axis=1).astype(BF16)
        wf = jnp.pad(wi[:, o_f:o_c], ((0, 0), (0, LANES - N_HEADS))).astype(BF16)
        bfg = jnp.pad(b_forget[i], (0, LANES - N_HEADS)).reshape(1, LANES)
        gains = jnp.stack([jnp.tile(fox_q_gain[i] * scale, N_HEADS), jnp.tile(fox_k_gain[i], N_HEADS),
                           jnp.tile(sb_q_gain[i] * scale, N_HEADS), jnp.tile(sb_k_gain[i], N_HEADS)], axis=0)
        us5, qkvf, conv_ab, qkvs, cum = _inproj(x, g_mix[i].reshape(1, D), w_main, wf, bfg, gains)

        bmat, cmat, a = _s5_params(s5_lam_re[i], s5_lam_im[i], s5_log_dt[i], s5_b_re[i], s5_b_im[i],
                                   s5_c_re[i], s5_c_im[i])
        ys5 = _s5(us5.reshape(L * B, D_BRANCH), bmat, cmat, a, s5_d[i].reshape(1, D_BRANCH),
                  s5_w_glu[i].astype(BF16), B).reshape(L, B * D_BRANCH)

        cum4 = cum[:, :, :N_HEADS]
        cq = cum4.reshape(B, L, 2, 2).transpose(0, 2, 1, 3)
        ck = jnp.pad(cum4.reshape(B, L, 2, 2).transpose(0, 2, 3, 1), ((0, 0), (0, 0), (0, SUBLANES - 2), (0, 0)))
        yfox = _fox(qkvf, cq, ck)
        yconv = _conv(conv_ab, conv_w[i], conv_b[i].reshape(1, -1), conv_ln_g[i].reshape(1, -1),
                      conv_ln_b[i].reshape(1, -1))
        ysb = _sb(qkvs)

        x = _merge(x, g_mix[i].reshape(1, D), ys5, yfox, yconv, ysb, wi[:, o_g:].astype(BF16),
                   w_branch[i].astype(BF16), w_out[i].astype(BF16))

        j = i // 2
        gf = g_ffn[i].reshape(1, D)
        if i % 2 == 0:
            wr = jnp.zeros((D, LANES), F32)
            br = jnp.zeros((1, LANES), F32)
            x2 = _ffn(x.reshape(B * L, D), gf, wr, br, ffn_w1[j][None].astype(BF16),
                      ffn_w3[j][None].astype(BF16), ffn_w2[j][None].astype(BF16), moe=False)
        else:
            wr = jnp.pad(router_w[j], ((0, 0), (0, LANES - N_EXPERTS)))
            br = jnp.pad(router_b[j], (0, LANES - N_EXPERTS), constant_values=-1e30).reshape(1, LANES)
            x2 = _ffn(x.reshape(B * L, D), gf, wr, br, moe_w1[j].astype(BF16), moe_w3[j].astype(BF16),
                      moe_w2[j].astype(BF16), moe=True)
        x = x2.reshape(B, L, D)
    return x
```

```python
import functools
import math

import numpy as np

import jax
import jax.numpy as jnp
from jax import lax
from jax.experimental import pallas as pl
from jax.experimental.pallas import tpu as pltpu

F32 = jnp.float32
BF16 = jnp.bfloat16
EPS = 1e-6
LOG2E = math.log2(math.e)

D_BRANCH = 256
HEAD_DIM = 64
N_HEADS = D_BRANCH // HEAD_DIM
S5_GROUP = 16
S5_GROUPS = D_BRANCH // S5_GROUP
S5_STATE = 64
N_STATE = S5_GROUPS * S5_STATE
CONV_WIDTH = 31
CONV_HALO = 32
N_EXPERTS = 8
LANES = 128
SUBLANES = 8
VMEM_LIMIT = 56 * 1024 * 1024
D_SPREAD = N_HEADS * LANES
N_SPLIT = 3

TM_PROJ = 512
TM_MERGE = 512
TM_FFN = 512
T_ATT = 512
SB_BLOCK = 256
LC_S5 = 128
LC_CONV = 512

NEG_INF = float("-inf")


def _cparams(sem):
    return pltpu.CompilerParams(dimension_semantics=sem, vmem_limit_bytes=VMEM_LIMIT)


def _dot(a, b):
    return jnp.dot(a, b, preferred_element_type=F32)


def _dot_nt(a, b):
    return lax.dot_general(a, b, (((1,), (1,)), ((), ())), preferred_element_type=F32)


def _split2(x):
    hi = x.astype(BF16)
    lo = (x - hi.astype(F32)).astype(BF16)
    return hi, lo


def _rmsnorm(x, g):
    ms = jnp.mean(x * x, axis=-1, keepdims=True)
    return x * lax.rsqrt(ms + EPS) * g


def _sigmoid(x):
    return 1.0 / (1.0 + jnp.exp(-x))


def _log_sigmoid(x):
    return jnp.minimum(x, 0.0) - jnp.log(1.0 + jnp.exp(-jnp.abs(x)))


def _lane_tile(x, n):
    return x if n == 1 else jnp.concatenate([x] * n, axis=1)


def _forget_feature_tables():
    sel = np.zeros((LANES, 2 * D_SPREAD), np.float32)
    const = np.zeros((1, 2 * D_SPREAD), np.float32)
    for h in range(N_HEADS):
        for j in range(N_SPLIT):
            sel[j * N_HEADS + h, h * LANES + HEAD_DIM + j] = 1.0
            sel[j * N_HEADS + h, D_SPREAD + h * LANES + HEAD_DIM + N_SPLIT + j] = -1.0
            const[0, h * LANES + HEAD_DIM + N_SPLIT + j] = 1.0
            const[0, D_SPREAD + h * LANES + HEAD_DIM + j] = 1.0
    return sel, const


def _inproj_kernel(x_ref, g_ref, wa_ref, wb_ref, wf_ref, bf_ref, gain_ref, sel_ref, const_ref,
                   us5_ref, qf_ref, kf_ref, vf_ref, conv_ref, qs_ref, ks_ref, vs_ref, carry_ref, *, tm):
    li = pl.program_id(1)
    x = x_ref[0]
    h = _rmsnorm(x, g_ref[...]).astype(BF16)
    lane = lax.broadcasted_iota(jnp.int32, (tm, LANES), 1)

    r = lax.broadcasted_iota(jnp.int32, (D_BRANCH, D_BRANCH), 0) // HEAD_DIM
    c = lax.broadcasted_iota(jnp.int32, (D_BRANCH, D_BRANCH), 1) // HEAD_DIM
    ones_bd = jnp.where(r == c, 1.0, 0.0).astype(BF16)

    def qknorm(t, gi):
        hi, lo = _split2(t * t)
        ss = _dot(hi, ones_bd) + _dot(lo, ones_bd)
        return t * lax.rsqrt(ss * (1.0 / HEAD_DIM) + EPS) * gain_ref[gi:gi + 1, :]

    def spread(t, fill):
        blocks = []
        for hp in range(2):
            pair = t[:, hp * LANES:(hp + 1) * LANES]
            blocks += [pair, pltpu.roll(pair, HEAD_DIM, axis=1)]
        out = [jnp.where(lane < HEAD_DIM, blocks[n], fill(n)) for n in range(N_HEADS)]
        return jnp.concatenate(out, axis=1).astype(BF16)

    @pl.when(li == 0)
    def _():
        carry_ref[...] = jnp.zeros_like(carry_ref)

    lf = _log_sigmoid(_dot(h, wf_ref[...]) + bf_ref[...]) * LOG2E
    rr = lax.broadcasted_iota(jnp.int32, (tm, tm), 0)
    cc = lax.broadcasted_iota(jnp.int32, (tm, tm), 1)
    tri = jnp.where(cc <= rr, 1.0, 0.0).astype(BF16)

    def split3(v):
        hi = v.astype(BF16).astype(F32)
        r1 = v - hi
        mid = r1.astype(BF16).astype(F32)
        lo = (r1 - mid).astype(BF16).astype(F32)
        return hi, mid, lo

    hi, mid, lo = split3(lf)
    cum = _dot(tri, hi.astype(BF16)) + _dot(tri, mid.astype(BF16)) + _dot(tri, lo.astype(BF16)) + carry_ref[...]
    carry_ref[...] = cum[tm - 1:tm, :]
    hi, mid, lo = split3(jnp.where(lane < N_HEADS, cum, 0.0))
    packed = hi + pltpu.roll(mid, N_HEADS, axis=1) + pltpu.roll(lo, 2 * N_HEADS, axis=1)
    feat = _dot(packed.astype(BF16), sel_ref[...]) + const_ref[...]

    def feat_q(n):
        return feat[:, n * LANES:(n + 1) * LANES]

    def feat_k(n):
        return feat[:, D_SPREAD + n * LANES:D_SPREAD + (n + 1) * LANES]

    zero = lambda n: 0.0
    one = lambda n: 1.0

    us5_ref[...] = _dot(h, wa_ref[:, 0:256])
    qf_ref[0] = spread(qknorm(_dot(h, wa_ref[:, 256:512]), 0), feat_q)
    kf_ref[0] = spread(qknorm(_dot(h, wa_ref[:, 512:768]), 1), feat_k)
    vf_ref[0] = spread(_dot(h, wa_ref[:, 768:1024]), one)
    conv_ref[0] = _dot(h, wb_ref[:, 0:512])
    qs_ref[0] = spread(qknorm(_dot(h, wb_ref[:, 512:768]), 2), zero)
    ks_ref[0] = spread(qknorm(_dot(h, wb_ref[:, 768:1024]), 3), zero)
    vs_ref[0] = _dot(h, wb_ref[:, 1024:1280]).astype(BF16)


def _inproj(x, g, wa, wb, wf, bfg, gains, sel, const):
    B, L, D = x.shape
    tm = min(TM_PROJ, L)
    c2 = lambda b, l: (0, 0)
    row_spec = lambda n: pl.BlockSpec((1, tm, n), lambda b, l: (b, l, 0))
    return pl.pallas_call(
        functools.partial(_inproj_kernel, tm=tm),
        grid=(B, L // tm),
        in_specs=[
            row_spec(D),
            pl.BlockSpec((1, D), c2),
            pl.BlockSpec(wa.shape, c2),
            pl.BlockSpec(wb.shape, c2),
            pl.BlockSpec((D, LANES), c2),
            pl.BlockSpec((1, LANES), c2),
            pl.BlockSpec((4, D_BRANCH), c2),
            pl.BlockSpec(sel.shape, c2),
            pl.BlockSpec(const.shape, c2),
        ],
        out_specs=[
            pl.BlockSpec((tm, D_BRANCH), lambda b, l: (l, b)),
            row_spec(D_SPREAD), row_spec(D_SPREAD), row_spec(D_SPREAD),
            row_spec(2 * D_BRANCH),
            row_spec(D_SPREAD), row_spec(D_SPREAD), row_spec(D_BRANCH),
        ],
        out_shape=[
            jax.ShapeDtypeStruct((L, B * D_BRANCH), F32),
            jax.ShapeDtypeStruct((B, L, D_SPREAD), BF16),
            jax.ShapeDtypeStruct((B, L, D_SPREAD), BF16),
            jax.ShapeDtypeStruct((B, L, D_SPREAD), BF16),
            jax.ShapeDtypeStruct((B, L, 2 * D_BRANCH), F32),
            jax.ShapeDtypeStruct((B, L, D_SPREAD), BF16),
            jax.ShapeDtypeStruct((B, L, D_SPREAD), BF16),
            jax.ShapeDtypeStruct((B, L, D_BRANCH), BF16),
        ],
        scratch_shapes=[pltpu.VMEM((1, LANES), F32)],
        compiler_params=_cparams(("parallel", "arbitrary")),
        name="inproj",
    )(x, g, wa, wb, wf, bfg, gains, sel, const)


def _s5_kernel(u_ref, bmat_ref, cmat_ref, a_ref, d_ref, wglu_ref, y_ref, bu_ref, st_ref, *, lc, nb):
    ci = pl.program_id(0)

    @pl.when(ci == 0)
    def _():
        st_ref[...] = jnp.zeros_like(st_ref)

    u = u_ref[...]
    bu_ref[...] = _dot(u.astype(BF16), bmat_ref[...])

    a_re = jnp.broadcast_to(a_ref[0:1, :], (nb, N_STATE))
    a_im = jnp.broadcast_to(a_ref[1:2, :], (nb, N_STATE))

    def step(t, carry):
        s_re, s_im = carry
        r0 = pl.multiple_of(t * nb, nb)
        b_re = bu_ref[pl.ds(r0, nb), 0:N_STATE]
        b_im = bu_ref[pl.ds(r0, nb), N_STATE:2 * N_STATE]
        n_re = a_re * s_re - a_im * s_im + b_re
        n_im = a_re * s_im + a_im * s_re + b_im
        bu_ref[pl.ds(r0, nb), 0:N_STATE] = n_re
        bu_ref[pl.ds(r0, nb), N_STATE:2 * N_STATE] = n_im
        return n_re, n_im

    s_re, s_im = lax.fori_loop(0, lc, step, (st_ref[:, 0:N_STATE], st_ref[:, N_STATE:2 * N_STATE]),
                               unroll=2)
    st_ref[:, 0:N_STATE] = s_re
    st_ref[:, N_STATE:2 * N_STATE] = s_im

    y = _dot(bu_ref[...].astype(BF16), cmat_ref[...]) + d_ref[...] * u
    y = jax.nn.gelu(y, approximate=True)
    y = y * _sigmoid(_dot(y.astype(BF16), wglu_ref[...]))
    y_ref[...] = y.astype(y_ref.dtype)


def _s5(u2, bmat, cmat, a, d, wglu, nb):
    rows = u2.shape[0]
    L = rows // nb
    lc = min(LC_S5, L)
    const = lambda c: (0, 0)
    return pl.pallas_call(
        functools.partial(_s5_kernel, lc=lc, nb=nb),
        grid=(L // lc,),
        in_specs=[
            pl.BlockSpec((lc * nb, D_BRANCH), lambda c: (c, 0)),
            pl.BlockSpec((D_BRANCH, 2 * N_STATE), const),
            pl.BlockSpec((2 * N_STATE, D_BRANCH), const),
            pl.BlockSpec((2, N_STATE), const),
            pl.BlockSpec((1, D_BRANCH), const),
            pl.BlockSpec((D_BRANCH, D_BRANCH), const),
        ],
        out_specs=pl.BlockSpec((lc * nb, D_BRANCH), lambda c: (c, 0)),
        out_shape=jax.ShapeDtypeStruct((rows, D_BRANCH), BF16),
        scratch_shapes=[pltpu.VMEM((lc * nb, 2 * N_STATE), F32), pltpu.VMEM((nb, 2 * N_STATE), F32)],
        compiler_params=_cparams(("arbitrary",)),
        name="s5",
    )(u2, bmat, cmat, a, d, wglu)


def _fox_kernel(q_ref, k_ref, v_ref, o_ref, *, tq):
    qi = pl.program_id(2)
    nrep = tq // LANES
    lane = lax.broadcasted_iota(jnp.int32, (tq, LANES), 1)
    row = lax.broadcasted_iota(jnp.int32, (tq, tq), 0)
    col = lax.broadcasted_iota(jnp.int32, (tq, tq), 1)

    def tile(ki, carry, diag):
        k0 = pl.multiple_of(ki * tq, tq)
        new = []
        for h in range(2):
            m, acc = carry[h]
            hs = slice(h * LANES, (h + 1) * LANES)
            s = _dot_nt(q_ref[0, :, hs], k_ref[0, pl.ds(k0, tq), hs])
            if diag:
                s = jnp.where(col <= row, s, NEG_INF)
            m_new = jnp.maximum(m, jnp.max(s, axis=1, keepdims=True))
            p = jnp.exp2(s - _lane_tile(m_new, nrep))
            acc = jnp.exp2(m - m_new) * acc + _dot(p.astype(BF16), v_ref[0, pl.ds(k0, tq), hs])
            new.append((m_new, acc))
        return tuple(new)

    init = tuple((jnp.full((tq, LANES), NEG_INF, F32), jnp.zeros((tq, LANES), F32)) for _ in range(2))
    carry = lax.fori_loop(0, qi, lambda ki, c: tile(ki, c, False), init)
    carry = tile(qi, carry, True)
    res = [acc / pltpu.roll(acc, HEAD_DIM, axis=1) for _, acc in carry]
    o_ref[0] = jnp.where(lane < HEAD_DIM, res[0], pltpu.roll(res[1], HEAD_DIM, axis=1)).astype(o_ref.dtype)


def _fox(q, k, v):
    B, L, _ = q.shape
    tq = min(T_ATT, L)
    return pl.pallas_call(
        functools.partial(_fox_kernel, tq=tq),
        grid=(B, 2, L // tq),
        in_specs=[
            pl.BlockSpec((1, tq, 2 * LANES), lambda b, p, i: (b, i, p)),
            pl.BlockSpec((1, L, 2 * LANES), lambda b, p, i: (b, 0, p)),
            pl.BlockSpec((1, L, 2 * LANES), lambda b, p, i: (b, 0, p)),
        ],
        out_specs=pl.BlockSpec((1, tq, LANES), lambda b, p, i: (b, i, p)),
        out_shape=jax.ShapeDtypeStruct((B, L, D_BRANCH), BF16),
        compiler_params=_cparams(("parallel", "parallel", "arbitrary")),
        name="fox",
    )(q, k, v)


def _sb_kernel(q_ref, k_ref, v_ref, o_ref, *, tq, blk):
    qi = pl.program_id(2)
    n_blk = tq // blk
    lane = lax.broadcasted_iota(jnp.int32, (tq, LANES), 1)
    row = lax.broadcasted_iota(jnp.int32, (tq, tq), 0)
    col = lax.broadcasted_iota(jnp.int32, (tq, tq), 1)
    ur = lax.broadcasted_iota(jnp.int32, (2 * blk, blk), 0)
    uc = lax.broadcasted_iota(jnp.int32, (2 * blk, blk), 1)
    upper2 = jnp.where(jnp.where(ur >= blk, ur - blk, ur) > uc, 1.0, 0.0).astype(BF16)

    def tile(ki, carry, diag):
        k0 = pl.multiple_of(ki * tq, tq)
        vt = v_ref[0, pl.ds(k0, tq), :]
        new = []
        for h in range(2):
            run, acc = carry[h]
            hs = slice(h * LANES, (h + 1) * LANES)
            z = _dot_nt(q_ref[0, :, hs], k_ref[0, pl.ds(k0, tq), hs])
            l1p = jnp.log(1.0 + jnp.exp2(jnp.abs(z) * (-LOG2E)))
            log_beta = jnp.minimum(z, 0.0) - l1p
            log_keep = log_beta - z
            if diag:
                log_keep = jnp.where(col < row, log_keep, 0.0)
            hi, lo = _split2(log_keep)
            after = [None] * n_blk
            for c in reversed(range(n_blk)):
                cs = slice(c * blk, (c + 1) * blk)
                after[c] = (_dot(jnp.concatenate([hi[:, cs], lo[:, cs]], axis=1), upper2)
                            + _lane_tile(run, blk // LANES))
                run = run + jnp.sum(log_keep[:, cs], axis=1, keepdims=True)
            a = jnp.exp(log_beta + jnp.concatenate(after, axis=1))
            if diag:
                a = jnp.where(col < row, a, 0.0)
            acc = acc + _dot(a.astype(BF16), vt)
            new.append((run, acc))
        return tuple(new)

    carry = tuple((jnp.zeros((tq, LANES), F32), jnp.zeros((tq, LANES), F32)) for _ in range(2))
    carry = tile(qi, carry, True)
    carry = lax.fori_loop(0, qi, lambda i, c: tile(qi - 1 - i, c, False), carry)
    (_, a0), (_, a1) = carry
    o_ref[0] = jnp.where(lane < HEAD_DIM, a0, a1).astype(o_ref.dtype)


def _sb(q, k, v):
    B, L, _ = q.shape
    tq = min(T_ATT, L)
    return pl.pallas_call(
        functools.partial(_sb_kernel, tq=tq, blk=min(SB_BLOCK, tq)),
        grid=(B, 2, L // tq),
        in_specs=[
            pl.BlockSpec((1, tq, 2 * LANES), lambda b, p, i: (b, i, p)),
            pl.BlockSpec((1, L, 2 * LANES), lambda b, p, i: (b, 0, p)),
            pl.BlockSpec((1, L, LANES), lambda b, p, i: (b, 0, p)),
        ],
        out_specs=pl.BlockSpec((1, tq, LANES), lambda b, p, i: (b, i, p)),
        out_shape=jax.ShapeDtypeStruct((B, L, D_BRANCH), BF16),
        compiler_params=_cparams(("parallel", "parallel", "arbitrary")),
        name="stickbreak",
    )(q, k, v)


def _conv_kernel(ab_ref, w_ref, b_ref, g_ref, beta_ref, o_ref, pad_ref, *, lc):
    li = pl.program_id(1)

    @pl.when(li == 0)
    def _():
        pad_ref[0:CONV_HALO, :] = jnp.zeros((CONV_HALO, D_BRANCH), F32)

    ab = ab_ref[0]
    pad_ref[CONV_HALO:CONV_HALO + lc, :] = ab[:, 0:D_BRANCH] * _sigmoid(ab[:, D_BRANCH:2 * D_BRANCH])
    off = CONV_HALO - (CONV_WIDTH - 1)
    acc = jnp.zeros((lc, D_BRANCH), F32) + b_ref[...]
    for j in range(CONV_WIDTH):
        acc = acc + w_ref[j:j + 1, :] * pad_ref[off + j:off + j + lc, :]
    pad_ref[0:CONV_HALO, :] = pad_ref[lc:lc + CONV_HALO, :]
    mu = jnp.mean(acc, axis=-1, keepdims=True)
    xc = acc - mu
    var = jnp.mean(xc * xc, axis=-1, keepdims=True)
    y = xc * lax.rsqrt(var + EPS) * g_ref[...] + beta_ref[...]
    o_ref[0] = (y * _sigmoid(y)).astype(o_ref.dtype)


def _conv(ab, w, b, g, beta):
    B, L, _ = ab.shape
    lc = min(LC_CONV, L)
    const = lambda b_, l: (0, 0)
    return pl.pallas_call(
        functools.partial(_conv_kernel, lc=lc),
        grid=(B, L // lc),
        in_specs=[
            pl.BlockSpec((1, lc, 2 * D_BRANCH), lambda b_, l: (b_, l, 0)),
            pl.BlockSpec((CONV_WIDTH, D_BRANCH), const),
            pl.BlockSpec((1, D_BRANCH), const),
            pl.BlockSpec((1, D_BRANCH), const),
            pl.BlockSpec((1, D_BRANCH), const),
        ],
        out_specs=pl.BlockSpec((1, lc, D_BRANCH), lambda b_, l: (b_, l, 0)),
        out_shape=jax.ShapeDtypeStruct((B, L, D_BRANCH), BF16),
        scratch_shapes=[pltpu.VMEM((CONV_HALO + lc, D_BRANCH), F32)],
        compiler_params=_cparams(("parallel", "arbitrary")),
        name="conv",
    )(ab, w, b, g, beta)


def _merge_kernel(x_ref, g_ref, ys5_ref, yfox_ref, yconv_ref, ysb_ref, wg_ref, wb_ref, wo_ref, o_ref):
    x = x_ref[0]
    D = x.shape[-1]
    h = _rmsnorm(x, g_ref[...]).astype(BF16)
    ys = (ys5_ref[...], yfox_ref[0], yconv_ref[0], ysb_ref[0])
    merged = None
    for n in range(4):
        gate = _sigmoid(_dot(h, wg_ref[:, n * D:(n + 1) * D]))
        term = gate * _dot(ys[n], wb_ref[n])
        merged = term if merged is None else merged + term
    o_ref[0] = x + _dot(merged.astype(BF16), wo_ref[...])


def _merge(x, g, ys5, yfox, yconv, ysb, wg, wb, wo):
    B, L, D = x.shape
    tm = min(TM_MERGE, L)
    c2 = lambda b, l: (0, 0)
    yspec = pl.BlockSpec((1, tm, D_BRANCH), lambda b, l: (b, l, 0))
    return pl.pallas_call(
        _merge_kernel,
        grid=(B, L // tm),
        in_specs=[
            pl.BlockSpec((1, tm, D), lambda b, l: (b, l, 0)),
            pl.BlockSpec((1, D), c2),
            pl.BlockSpec((tm, D_BRANCH), lambda b, l: (l, b)),
            yspec, yspec, yspec,
            pl.BlockSpec((D, 4 * D), c2),
            pl.BlockSpec((4, D_BRANCH, D), lambda b, l: (0, 0, 0)),
            pl.BlockSpec((D, D), c2),
        ],
        out_specs=pl.BlockSpec((1, tm, D), lambda b, l: (b, l, 0)),
        out_shape=jax.ShapeDtypeStruct((B, L, D), F32),
        compiler_params=_cparams(("parallel", "parallel")),
        name="merge",
    )(x, g, ys5, yfox, yconv, ysb, wg, wb, wo)


def _ffn_kernel(x_ref, g_ref, wr_ref, br_ref, w1_ref, w3_ref, w2_ref, o_ref, h_ref, acc_ref, comb_ref,
                *, moe, n_e, n_f):
    e = pl.program_id(1)
    f = pl.program_id(2)

    @pl.when((e == 0) & (f == 0))
    def _():
        h = _rmsnorm(x_ref[...], g_ref[...])
        h_ref[...] = h.astype(BF16)
        acc_ref[...] = jnp.zeros_like(acc_ref)
        if moe:
            logits = jnp.dot(h, wr_ref[...], preferred_element_type=F32,
                             precision=lax.Precision.HIGHEST) + br_ref[...]
            lane = lax.broadcasted_iota(jnp.int32, logits.shape, 1).astype(F32)
            m1 = jnp.max(logits, axis=1, keepdims=True)
            i1 = jnp.min(jnp.where(logits == m1, lane, float(LANES)), axis=1, keepdims=True)
            rest = jnp.where(lane == i1, NEG_INF, logits)
            m2 = jnp.max(rest, axis=1, keepdims=True)
            i2 = jnp.min(jnp.where(rest == m2, lane, float(LANES)), axis=1, keepdims=True)
            e2 = jnp.exp(m2 - m1)
            p1 = 1.0 / (1.0 + e2)
            comb_ref[...] = jnp.where(lane == i1, p1, 0.0) + jnp.where(lane == i2, e2 * p1, 0.0)

    h = h_ref[...]
    a = _dot(h, w1_ref[0])
    b = _dot(h, w3_ref[0])
    gated = a * _sigmoid(a) * b
    if moe:
        lane = lax.broadcasted_iota(jnp.int32, comb_ref.shape, 1)
        ce = jnp.sum(jnp.where(lane == e, comb_ref[...], 0.0), axis=1, keepdims=True)
        gated = gated * ce
    acc_ref[...] += _dot(gated.astype(BF16), w2_ref[0])

    @pl.when((e == n_e - 1) & (f == n_f - 1))
    def _():
        o_ref[...] = x_ref[...] + acc_ref[...]


def _ffn(x2, g, wr, br, w1, w3, w2, moe):
    T, D = x2.shape
    n_e, _, F = w1.shape
    tm = min(TM_FFN, T)
    n_f = 2
    fc = F // n_f
    c2 = lambda i, e, f: (0, 0)
    return pl.pallas_call(
        functools.partial(_ffn_kernel, moe=moe, n_e=n_e, n_f=n_f),
        grid=(T // tm, n_e, n_f),
        in_specs=[
            pl.BlockSpec((tm, D), lambda i, e, f: (i, 0)),
            pl.BlockSpec((1, D), c2),
            pl.BlockSpec((D, LANES), c2),
            pl.BlockSpec((1, LANES), c2),
            pl.BlockSpec((1, D, fc), lambda i, e, f: (e, 0, f)),
            pl.BlockSpec((1, D, fc), lambda i, e, f: (e, 0, f)),
            pl.BlockSpec((1, fc, D), lambda i, e, f: (e, f, 0)),
        ],
        out_specs=pl.BlockSpec((tm, D), lambda i, e, f: (i, 0)),
        out_shape=jax.ShapeDtypeStruct((T, D), F32),
        scratch_shapes=[pltpu.VMEM((tm, D), BF16), pltpu.VMEM((tm, D), F32), pltpu.VMEM((tm, LANES), F32)],
        compiler_params=_cparams(("parallel", "arbitrary", "arbitrary")),
        name="moe_ffn" if moe else "dense_ffn",
    )(x2, g, wr, br, w1, w3, w2)


def _s5_params(lam_re, lam_im, log_dt, b_re, b_im, c_re, c_im):
    G, P, H = S5_GROUPS, S5_STATE, S5_GROUP
    dt = jnp.exp(log_dt)[:, None]
    mag = jnp.exp(lam_re * dt)
    ab_re = mag * jnp.cos(lam_im * dt)
    ab_im = mag * jnp.sin(lam_im * dt)
    nr, ni = ab_re - 1.0, ab_im
    den = lam_re * lam_re + lam_im * lam_im
    k_re = (nr * lam_re + ni * lam_im) / den
    k_im = (ni * lam_re - nr * lam_im) / den
    bb_re = k_re[..., None] * b_re - k_im[..., None] * b_im
    bb_im = k_re[..., None] * b_im + k_im[..., None] * b_re
    eye = jnp.eye(G, dtype=F32)
    bm_re = jnp.einsum("gph,gk->ghkp", bb_re, eye).reshape(G * H, G * P)
    bm_im = jnp.einsum("gph,gk->ghkp", bb_im, eye).reshape(G * H, G * P)
    bmat = jnp.concatenate([bm_re, bm_im], axis=1).astype(BF16)
    cm_re = jnp.einsum("ghp,gk->kpgh", c_re, eye).reshape(G * P, G * H)
    cm_im = jnp.einsum("ghp,gk->kpgh", c_im, eye).reshape(G * P, G * H)
    cmat = jnp.concatenate([cm_re, -cm_im], axis=0).astype(BF16)
    a = jnp.stack([ab_re.reshape(-1), ab_im.reshape(-1)], axis=0)
    return bmat, cmat, a


def kernel(x, g_mix, w_in, b_forget, fox_q_gain, fox_k_gain, sb_q_gain, sb_k_gain, s5_lam_re, s5_lam_im, s5_log_dt, s5_b_re, s5_b_im, s5_c_re, s5_c_im, s5_d, s5_w_glu, conv_w, conv_b, conv_ln_g, conv_ln_b, w_branch, w_out, g_ffn, ffn_w1, ffn_w3, ffn_w2, router_w, router_b, moe_w1, moe_w3, moe_w2):
    B, L, D = x.shape
    depth = g_mix.shape[0]
    assert B == SUBLANES, "the S5 scan lays the batch along the sublanes"
    n_qkv = 3 * D_BRANCH
    o_f = D_BRANCH + n_qkv
    o_c = o_f + N_HEADS
    o_g = o_c + 2 * D_BRANCH + n_qkv
    scale = 1.0 / math.sqrt(HEAD_DIM)
    sel_np, const_np = _forget_feature_tables()
    sel = jnp.asarray(sel_np, BF16)
    const = jnp.asarray(const_np, F32)

    w_a = w_in[:, :, :o_f].astype(BF16)
    w_b = w_in[:, :, o_c:o_g].astype(BF16)
    w_f = jnp.pad(w_in[:, :, o_f:o_c], ((0, 0), (0, 0), (0, LANES - N_HEADS))).astype(BF16)
    w_g = w_in[:, :, o_g:].astype(BF16)

    for i in range(depth):
        bfg = jnp.pad(b_forget[i], (0, LANES - N_HEADS)).reshape(1, LANES)
        gains = jnp.stack([jnp.tile(fox_q_gain[i] * (scale * LOG2E), N_HEADS), jnp.tile(fox_k_gain[i], N_HEADS),
                           jnp.tile(sb_q_gain[i] * scale, N_HEADS), jnp.tile(sb_k_gain[i], N_HEADS)], axis=0)
        us5, qf, kf, vf, conv_ab, qs, ks, vs = _inproj(x, g_mix[i].reshape(1, D), w_a[i], w_b[i], w_f[i], bfg,
                                                       gains, sel, const)

        bmat, cmat, a = _s5_params(s5_lam_re[i], s5_lam_im[i], s5_log_dt[i], s5_b_re[i], s5_b_im[i],
                                   s5_c_re[i], s5_c_im[i])
        ys5 = _s5(us5.reshape(L * B, D_BRANCH), bmat, cmat, a, s5_d[i].reshape(1, D_BRANCH),
                  s5_w_glu[i].astype(BF16), B).reshape(L, B * D_BRANCH)
        yfox = _fox(qf, kf, vf)
        yconv = _conv(conv_ab, conv_w[i], conv_b[i].reshape(1, -1), conv_ln_g[i].reshape(1, -1),
                      conv_ln_b[i].reshape(1, -1))
        ysb = _sb(qs, ks, vs)

        x = _merge(x, g_mix[i].reshape(1, D), ys5, yfox, yconv, ysb, w_g[i],
                   w_branch[i].astype(BF16), w_out[i].astype(BF16))

        j = i // 2
        gf = g_ffn[i].reshape(1, D)
        if i % 2 == 0:
            wr = jnp.zeros((D, LANES), F32)
            br = jnp.zeros((1, LANES), F32)
            x2 = _ffn(x.reshape(B * L, D), gf, wr, br, ffn_w1[j][None].astype(BF16),
                      ffn_w3[j][None].astype(BF16), ffn_w2[j][None].astype(BF16), moe=False)
        else:
            wr = jnp.pad(router_w[j], ((0, 0), (0, LANES - N_EXPERTS)))
            br = jnp.pad(router_b[j], (0, LANES - N_EXPERTS), constant_values=-1e30).reshape(1, LANES)
            x2 = _ffn(x.reshape(B * L, D), gf, wr, br, moe_w1[j].astype(BF16), moe_w3[j].astype(BF16),
                      moe_w2[j].astype(BF16), moe=True)
        x = x2.reshape(B, L, D)
    return x
```

```python
import functools
import math

import numpy as np

import jax
import jax.numpy as jnp
from jax import lax
from jax.experimental import pallas as pl
from jax.experimental.pallas import tpu as pltpu

F32 = jnp.float32
BF16 = jnp.bfloat16
EPS = 1e-6
LOG2E = math.log2(math.e)

D_BRANCH = 256
HEAD_DIM = 64
N_HEADS = D_BRANCH // HEAD_DIM
S5_GROUP = 16
S5_GROUPS = D_BRANCH // S5_GROUP
S5_STATE = 64
N_STATE = S5_GROUPS * S5_STATE
CONV_WIDTH = 31
CONV_HALO = 32
N_EXPERTS = 8
LANES = 128
SUBLANES = 8
VMEM_LIMIT = 56 * 1024 * 1024
D_SPREAD = N_HEADS * LANES
N_SPLIT = 3

TM_PROJ = 512
TM_MERGE = 512
TM_FFN = 512
T_ATT = 512
SB_BLOCK = 256
LC_S5 = 128
LC_CONV = 512
TM_MOE = 512
FC_MOE = 256
TC_MOE = 256
VMEM_LIMIT_MOE = 60 * 1024 * 1024

NEG_INF = float("-inf")


def _cparams(sem):
    return pltpu.CompilerParams(dimension_semantics=sem, vmem_limit_bytes=VMEM_LIMIT)


def _dot(a, b):
    return jnp.dot(a, b, preferred_element_type=F32)


def _dot_nt(a, b):
    return lax.dot_general(a, b, (((1,), (1,)), ((), ())), preferred_element_type=F32)


def _split2(x):
    hi = x.astype(BF16)
    lo = (x - hi.astype(F32)).astype(BF16)
    return hi, lo


def _rmsnorm(x, g):
    ms = jnp.mean(x * x, axis=-1, keepdims=True)
    return x * lax.rsqrt(ms + EPS) * g


def _sigmoid(x):
    return 1.0 / (1.0 + jnp.exp(-x))


def _log_sigmoid(x):
    return jnp.minimum(x, 0.0) - jnp.log(1.0 + jnp.exp(-jnp.abs(x)))


def _lane_tile(x, n):
    return x if n == 1 else jnp.concatenate([x] * n, axis=1)


def _forget_feature_tables():
    sel = np.zeros((LANES, 2 * D_SPREAD), np.float32)
    const = np.zeros((1, 2 * D_SPREAD), np.float32)
    for h in range(N_HEADS):
        for j in range(N_SPLIT):
            sel[j * N_HEADS + h, h * LANES + HEAD_DIM + j] = 1.0
            sel[j * N_HEADS + h, D_SPREAD + h * LANES + HEAD_DIM + N_SPLIT + j] = -1.0
            const[0, h * LANES + HEAD_DIM + N_SPLIT + j] = 1.0
            const[0, D_SPREAD + h * LANES + HEAD_DIM + j] = 1.0
    return sel, const


def _inproj_kernel(x_ref, g_ref, wa_ref, wb_ref, wf_ref, bf_ref, gain_ref, sel_ref, const_ref,
                   us5_ref, qf_ref, kf_ref, vf_ref, conv_ref, qs_ref, ks_ref, vs_ref, carry_ref, *, tm):
    li = pl.program_id(1)
    x = x_ref[0]
    h = _rmsnorm(x, g_ref[...]).astype(BF16)
    lane = lax.broadcasted_iota(jnp.int32, (tm, LANES), 1)

    r = lax.broadcasted_iota(jnp.int32, (D_BRANCH, D_BRANCH), 0) // HEAD_DIM
    c = lax.broadcasted_iota(jnp.int32, (D_BRANCH, D_BRANCH), 1) // HEAD_DIM
    ones_bd = jnp.where(r == c, 1.0, 0.0).astype(BF16)

    def qknorm(t, gi):
        hi, lo = _split2(t * t)
        ss = _dot(hi, ones_bd) + _dot(lo, ones_bd)
        return t * lax.rsqrt(ss * (1.0 / HEAD_DIM) + EPS) * gain_ref[gi:gi + 1, :]

    def spread(t, fill):
        blocks = []
        for hp in range(2):
            pair = t[:, hp * LANES:(hp + 1) * LANES]
            blocks += [pair, pltpu.roll(pair, HEAD_DIM, axis=1)]
        out = [jnp.where(lane < HEAD_DIM, blocks[n], fill(n)) for n in range(N_HEADS)]
        return jnp.concatenate(out, axis=1).astype(BF16)

    @pl.when(li == 0)
    def _():
        carry_ref[...] = jnp.zeros_like(carry_ref)

    lf = _log_sigmoid(_dot(h, wf_ref[...]) + bf_ref[...]) * LOG2E
    rr = lax.broadcasted_iota(jnp.int32, (tm, tm), 0)
    cc = lax.broadcasted_iota(jnp.int32, (tm, tm), 1)
    tri = jnp.where(cc <= rr, 1.0, 0.0).astype(BF16)

    def split3(v):
        hi = v.astype(BF16).astype(F32)
        r1 = v - hi
        mid = r1.astype(BF16).astype(F32)
        lo = (r1 - mid).astype(BF16).astype(F32)
        return hi, mid, lo

    hi, mid, lo = split3(lf)
    cum = _dot(tri, hi.astype(BF16)) + _dot(tri, mid.astype(BF16)) + _dot(tri, lo.astype(BF16)) + carry_ref[...]
    carry_ref[...] = cum[tm - 1:tm, :]
    hi, mid, lo = split3(jnp.where(lane < N_HEADS, cum, 0.0))
    packed = hi + pltpu.roll(mid, N_HEADS, axis=1) + pltpu.roll(lo, 2 * N_HEADS, axis=1)
    feat = _dot(packed.astype(BF16), sel_ref[...]) + const_ref[...]

    def feat_q(n):
        return feat[:, n * LANES:(n + 1) * LANES]

    def feat_k(n):
        return feat[:, D_SPREAD + n * LANES:D_SPREAD + (n + 1) * LANES]

    zero = lambda n: 0.0
    one = lambda n: 1.0

    us5_ref[...] = _dot(h, wa_ref[:, 0:256])
    qf_ref[0] = spread(qknorm(_dot(h, wa_ref[:, 256:512]), 0), feat_q)
    kf_ref[0] = spread(qknorm(_dot(h, wa_ref[:, 512:768]), 1), feat_k)
    vf_ref[0] = spread(_dot(h, wa_ref[:, 768:1024]), one)
    conv_ref[0] = _dot(h, wb_ref[:, 0:512])
    qs_ref[0] = spread(qknorm(_dot(h, wb_ref[:, 512:768]), 2), zero)
    ks_ref[0] = spread(qknorm(_dot(h, wb_ref[:, 768:1024]), 3), zero)
    vs_ref[0] = _dot(h, wb_ref[:, 1024:1280]).astype(BF16)


def _inproj(x, g, wa, wb, wf, bfg, gains, sel, const):
    B, L, D = x.shape
    tm = min(TM_PROJ, L)
    c2 = lambda b, l: (0, 0)
    row_spec = lambda n: pl.BlockSpec((1, tm, n), lambda b, l: (b, l, 0))
    return pl.pallas_call(
        functools.partial(_inproj_kernel, tm=tm),
        grid=(B, L // tm),
        in_specs=[
            row_spec(D),
            pl.BlockSpec((1, D), c2),
            pl.BlockSpec(wa.shape, c2),
            pl.BlockSpec(wb.shape, c2),
            pl.BlockSpec((D, LANES), c2),
            pl.BlockSpec((1, LANES), c2),
            pl.BlockSpec((4, D_BRANCH), c2),
            pl.BlockSpec(sel.shape, c2),
            pl.BlockSpec(const.shape, c2),
        ],
        out_specs=[
            pl.BlockSpec((tm, D_BRANCH), lambda b, l: (l, b)),
            row_spec(D_SPREAD), row_spec(D_SPREAD), row_spec(D_SPREAD),
            row_spec(2 * D_BRANCH),
            row_spec(D_SPREAD), row_spec(D_SPREAD), row_spec(D_BRANCH),
        ],
        out_shape=[
            jax.ShapeDtypeStruct((L, B * D_BRANCH), F32),
            jax.ShapeDtypeStruct((B, L, D_SPREAD), BF16),
            jax.ShapeDtypeStruct((B, L, D_SPREAD), BF16),
            jax.ShapeDtypeStruct((B, L, D_SPREAD), BF16),
            jax.ShapeDtypeStruct((B, L, 2 * D_BRANCH), F32),
            jax.ShapeDtypeStruct((B, L, D_SPREAD), BF16),
            jax.ShapeDtypeStruct((B, L, D_SPREAD), BF16),
            jax.ShapeDtypeStruct((B, L, D_BRANCH), BF16),
        ],
        scratch_shapes=[pltpu.VMEM((1, LANES), F32)],
        compiler_params=_cparams(("parallel", "arbitrary")),
        name="inproj",
    )(x, g, wa, wb, wf, bfg, gains, sel, const)


def _s5_kernel(u_ref, bmat_ref, cmat_ref, a_ref, d_ref, wglu_ref, y_ref, bu_ref, st_ref, *, lc, nb):
    ci = pl.program_id(0)

    @pl.when(ci == 0)
    def _():
        st_ref[...] = jnp.zeros_like(st_ref)

    u = u_ref[...]
    bu_ref[...] = _dot(u.astype(BF16), bmat_ref[...])

    a_re = jnp.broadcast_to(a_ref[0:1, :], (nb, N_STATE))
    a_im = jnp.broadcast_to(a_ref[1:2, :], (nb, N_STATE))

    def step(t, carry):
        s_re, s_im = carry
        r0 = pl.multiple_of(t * nb, nb)
        b_re = bu_ref[pl.ds(r0, nb), 0:N_STATE]
        b_im = bu_ref[pl.ds(r0, nb), N_STATE:2 * N_STATE]
        n_re = a_re * s_re - a_im * s_im + b_re
        n_im = a_re * s_im + a_im * s_re + b_im
        bu_ref[pl.ds(r0, nb), 0:N_STATE] = n_re
        bu_ref[pl.ds(r0, nb), N_STATE:2 * N_STATE] = n_im
        return n_re, n_im

    s_re, s_im = lax.fori_loop(0, lc, step, (st_ref[:, 0:N_STATE], st_ref[:, N_STATE:2 * N_STATE]),
                               unroll=2)
    st_ref[:, 0:N_STATE] = s_re
    st_ref[:, N_STATE:2 * N_STATE] = s_im

    y = _dot(bu_ref[...].astype(BF16), cmat_ref[...]) + d_ref[...] * u
    y = jax.nn.gelu(y, approximate=True)
    y = y * _sigmoid(_dot(y.astype(BF16), wglu_ref[...]))
    y_ref[...] = y.astype(y_ref.dtype)


def _s5(u2, bmat, cmat, a, d, wglu, nb):
    rows = u2.shape[0]
    L = rows // nb
    lc = min(LC_S5, L)
    const = lambda c: (0, 0)
    return pl.pallas_call(
        functools.partial(_s5_kernel, lc=lc, nb=nb),
        grid=(L // lc,),
        in_specs=[
            pl.BlockSpec((lc * nb, D_BRANCH), lambda c: (c, 0)),
            pl.BlockSpec((D_BRANCH, 2 * N_STATE), const),
            pl.BlockSpec((2 * N_STATE, D_BRANCH), const),
            pl.BlockSpec((2, N_STATE), const),
            pl.BlockSpec((1, D_BRANCH), const),
            pl.BlockSpec((D_BRANCH, D_BRANCH), const),
        ],
        out_specs=pl.BlockSpec((lc * nb, D_BRANCH), lambda c: (c, 0)),
        out_shape=jax.ShapeDtypeStruct((rows, D_BRANCH), BF16),
        scratch_shapes=[pltpu.VMEM((lc * nb, 2 * N_STATE), F32), pltpu.VMEM((nb, 2 * N_STATE), F32)],
        compiler_params=_cparams(("arbitrary",)),
        name="s5",
    )(u2, bmat, cmat, a, d, wglu)


def _fox_kernel(q_ref, k_ref, v_ref, o_ref, *, tq):
    qi = pl.program_id(2)
    nrep = tq // LANES
    lane = lax.broadcasted_iota(jnp.int32, (tq, LANES), 1)
    row = lax.broadcasted_iota(jnp.int32, (tq, tq), 0)
    col = lax.broadcasted_iota(jnp.int32, (tq, tq), 1)

    def tile(ki, carry, diag):
        k0 = pl.multiple_of(ki * tq, tq)
        new = []
        for h in range(2):
            m, acc = carry[h]
            hs = slice(h * LANES, (h + 1) * LANES)
            s = _dot_nt(q_ref[0, :, hs], k_ref[0, pl.ds(k0, tq), hs])
            if diag:
                s = jnp.where(col <= row, s, NEG_INF)
            m_new = jnp.maximum(m, jnp.max(s, axis=1, keepdims=True))
            p = jnp.exp2(s - _lane_tile(m_new, nrep))
            acc = jnp.exp2(m - m_new) * acc + _dot(p.astype(BF16), v_ref[0, pl.ds(k0, tq), hs])
            new.append((m_new, acc))
        return tuple(new)

    init = tuple((jnp.full((tq, LANES), NEG_INF, F32), jnp.zeros((tq, LANES), F32)) for _ in range(2))
    carry = lax.fori_loop(0, qi, lambda ki, c: tile(ki, c, False), init)
    carry = tile(qi, carry, True)
    res = [acc / pltpu.roll(acc, HEAD_DIM, axis=1) for _, acc in carry]
    o_ref[0] = jnp.where(lane < HEAD_DIM, res[0], pltpu.roll(res[1], HEAD_DIM, axis=1)).astype(o_ref.dtype)


def _fox(q, k, v):
    B, L, _ = q.shape
    tq = min(T_ATT, L)
    return pl.pallas_call(
        functools.partial(_fox_kernel, tq=tq),
        grid=(B, 2, L // tq),
        in_specs=[
            pl.BlockSpec((1, tq, 2 * LANES), lambda b, p, i: (b, i, p)),
            pl.BlockSpec((1, L, 2 * LANES), lambda b, p, i: (b, 0, p)),
            pl.BlockSpec((1, L, 2 * LANES), lambda b, p, i: (b, 0, p)),
        ],
        out_specs=pl.BlockSpec((1, tq, LANES), lambda b, p, i: (b, i, p)),
        out_shape=jax.ShapeDtypeStruct((B, L, D_BRANCH), BF16),
        compiler_params=_cparams(("parallel", "parallel", "arbitrary")),
        name="fox",
    )(q, k, v)


def _sb_kernel(q_ref, k_ref, v_ref, o_ref, *, tq, blk):
    qi = pl.program_id(2)
    n_blk = tq // blk
    lane = lax.broadcasted_iota(jnp.int32, (tq, LANES), 1)
    row = lax.broadcasted_iota(jnp.int32, (tq, tq), 0)
    col = lax.broadcasted_iota(jnp.int32, (tq, tq), 1)
    ur = lax.broadcasted_iota(jnp.int32, (2 * blk, blk), 0)
    uc = lax.broadcasted_iota(jnp.int32, (2 * blk, blk), 1)
    upper2 = jnp.where(jnp.where(ur >= blk, ur - blk, ur) > uc, 1.0, 0.0).astype(BF16)

    def tile(ki, carry, diag):
        k0 = pl.multiple_of(ki * tq, tq)
        vt = v_ref[0, pl.ds(k0, tq), :]
        new = []
        for h in range(2):
            run, acc = carry[h]
            hs = slice(h * LANES, (h + 1) * LANES)
            z = _dot_nt(q_ref[0, :, hs], k_ref[0, pl.ds(k0, tq), hs])
            l1p = jnp.log(1.0 + jnp.exp2(jnp.abs(z) * (-LOG2E)))
            log_beta = jnp.minimum(z, 0.0) - l1p
            log_keep = log_beta - z
            if diag:
                log_keep = jnp.where(col < row, log_keep, 0.0)
            hi, lo = _split2(log_keep)
            after = [None] * n_blk
            for c in reversed(range(n_blk)):
                cs = slice(c * blk, (c + 1) * blk)
                after[c] = (_dot(jnp.concatenate([hi[:, cs], lo[:, cs]], axis=1), upper2)
                            + _lane_tile(run, blk // LANES))
                run = run + jnp.sum(log_keep[:, cs], axis=1, keepdims=True)
            a = jnp.exp(log_beta + jnp.concatenate(after, axis=1))
            if diag:
                a = jnp.where(col < row, a, 0.0)
            acc = acc + _dot(a.astype(BF16), vt)
            new.append((run, acc))
        return tuple(new)

    carry = tuple((jnp.zeros((tq, LANES), F32), jnp.zeros((tq, LANES), F32)) for _ in range(2))
    carry = tile(qi, carry, True)
    carry = lax.fori_loop(0, qi, lambda i, c: tile(qi - 1 - i, c, False), carry)
    (_, a0), (_, a1) = carry
    o_ref[0] = jnp.where(lane < HEAD_DIM, a0, a1).astype(o_ref.dtype)


def _sb(q, k, v):
    B, L, _ = q.shape
    tq = min(T_ATT, L)
    return pl.pallas_call(
        functools.partial(_sb_kernel, tq=tq, blk=min(SB_BLOCK, tq)),
        grid=(B, 2, L // tq),
        in_specs=[
            pl.BlockSpec((1, tq, 2 * LANES), lambda b, p, i: (b, i, p)),
            pl.BlockSpec((1, L, 2 * LANES), lambda b, p, i: (b, 0, p)),
            pl.BlockSpec((1, L, LANES), lambda b, p, i: (b, 0, p)),
        ],
        out_specs=pl.BlockSpec((1, tq, LANES), lambda b, p, i: (b, i, p)),
        out_shape=jax.ShapeDtypeStruct((B, L, D_BRANCH), BF16),
        compiler_params=_cparams(("parallel", "parallel", "arbitrary")),
        name="stickbreak",
    )(q, k, v)


def _conv_kernel(ab_ref, w_ref, b_ref, g_ref, beta_ref, o_ref, pad_ref, *, lc):
    li = pl.program_id(1)

    @pl.when(li == 0)
    def _():
        pad_ref[0:CONV_HALO, :] = jnp.zeros((CONV_HALO, D_BRANCH), F32)

    ab = ab_ref[0]
    pad_ref[CONV_HALO:CONV_HALO + lc, :] = ab[:, 0:D_BRANCH] * _sigmoid(ab[:, D_BRANCH:2 * D_BRANCH])
    off = CONV_HALO - (CONV_WIDTH - 1)
    acc = jnp.zeros((lc, D_BRANCH), F32) + b_ref[...]
    for j in range(CONV_WIDTH):
        acc = acc + w_ref[j:j + 1, :] * pad_ref[off + j:off + j + lc, :]
    pad_ref[0:CONV_HALO, :] = pad_ref[lc:lc + CONV_HALO, :]
    mu = jnp.mean(acc, axis=-1, keepdims=True)
    xc = acc - mu
    var = jnp.mean(xc * xc, axis=-1, keepdims=True)
    y = xc * lax.rsqrt(var + EPS) * g_ref[...] + beta_ref[...]
    o_ref[0] = (y * _sigmoid(y)).astype(o_ref.dtype)


def _conv(ab, w, b, g, beta):
    B, L, _ = ab.shape
    lc = min(LC_CONV, L)
    const = lambda b_, l: (0, 0)
    return pl.pallas_call(
        functools.partial(_conv_kernel, lc=lc),
        grid=(B, L // lc),
        in_specs=[
            pl.BlockSpec((1, lc, 2 * D_BRANCH), lambda b_, l: (b_, l, 0)),
            pl.BlockSpec((CONV_WIDTH, D_BRANCH), const),
            pl.BlockSpec((1, D_BRANCH), const),
            pl.BlockSpec((1, D_BRANCH), const),
            pl.BlockSpec((1, D_BRANCH), const),
        ],
        out_specs=pl.BlockSpec((1, lc, D_BRANCH), lambda b_, l: (b_, l, 0)),
        out_shape=jax.ShapeDtypeStruct((B, L, D_BRANCH), BF16),
        scratch_shapes=[pltpu.VMEM((CONV_HALO + lc, D_BRANCH), F32)],
        compiler_params=_cparams(("parallel", "arbitrary")),
        name="conv",
    )(ab, w, b, g, beta)


def _merge_kernel(x_ref, g_ref, ys5_ref, yfox_ref, yconv_ref, ysb_ref, wg_ref, wb_ref, wo_ref, o_ref):
    x = x_ref[0]
    D = x.shape[-1]
    h = _rmsnorm(x, g_ref[...]).astype(BF16)
    ys = (ys5_ref[...], yfox_ref[0], yconv_ref[0], ysb_ref[0])
    merged = None
    for n in range(4):
        gate = _sigmoid(_dot(h, wg_ref[:, n * D:(n + 1) * D]))
        term = gate * _dot(ys[n], wb_ref[n])
        merged = term if merged is None else merged + term
    o_ref[0] = x + _dot(merged.astype(BF16), wo_ref[...])


def _merge(x, g, ys5, yfox, yconv, ysb, wg, wb, wo):
    B, L, D = x.shape
    tm = min(TM_MERGE, L)
    c2 = lambda b, l: (0, 0)
    yspec = pl.BlockSpec((1, tm, D_BRANCH), lambda b, l: (b, l, 0))
    return pl.pallas_call(
        _merge_kernel,
        grid=(B, L // tm),
        in_specs=[
            pl.BlockSpec((1, tm, D), lambda b, l: (b, l, 0)),
            pl.BlockSpec((1, D), c2),
            pl.BlockSpec((tm, D_BRANCH), lambda b, l: (l, b)),
            yspec, yspec, yspec,
            pl.BlockSpec((D, 4 * D), c2),
            pl.BlockSpec((4, D_BRANCH, D), lambda b, l: (0, 0, 0)),
            pl.BlockSpec((D, D), c2),
        ],
        out_specs=pl.BlockSpec((1, tm, D), lambda b, l: (b, l, 0)),
        out_shape=jax.ShapeDtypeStruct((B, L, D), F32),
        compiler_params=_cparams(("parallel", "parallel")),
        name="merge",
    )(x, g, ys5, yfox, yconv, ysb, wg, wb, wo)


def _ffn_kernel(x_ref, g_ref, w1_ref, w3_ref, w2_ref, o_ref, h_ref, acc_ref, *, n_f):
    f = pl.program_id(1)

    @pl.when(f == 0)
    def _():
        h_ref[...] = _rmsnorm(x_ref[...], g_ref[...]).astype(BF16)
        acc_ref[...] = jnp.zeros_like(acc_ref)

    h = h_ref[...]
    a = _dot(h, w1_ref[...])
    b = _dot(h, w3_ref[...])
    acc_ref[...] += _dot((a * _sigmoid(a) * b).astype(BF16), w2_ref[...])

    @pl.when(f == n_f - 1)
    def _():
        o_ref[...] = x_ref[...] + acc_ref[...]


def _ffn(x2, g, w1, w3, w2):
    T, D = x2.shape
    F = w1.shape[1]
    tm = min(TM_FFN, T)
    n_f = 2
    fc = F // n_f
    return pl.pallas_call(
        functools.partial(_ffn_kernel, n_f=n_f),
        grid=(T // tm, n_f),
        in_specs=[
            pl.BlockSpec((tm, D), lambda i, f: (i, 0)),
            pl.BlockSpec((1, D), lambda i, f: (0, 0)),
            pl.BlockSpec((D, fc), lambda i, f: (0, f)),
            pl.BlockSpec((D, fc), lambda i, f: (0, f)),
            pl.BlockSpec((fc, D), lambda i, f: (f, 0)),
        ],
        out_specs=pl.BlockSpec((tm, D), lambda i, f: (i, 0)),
        out_shape=jax.ShapeDtypeStruct((T, D), F32),
        scratch_shapes=[pltpu.VMEM((tm, D), BF16), pltpu.VMEM((tm, D), F32)],
        compiler_params=_cparams(("parallel", "arbitrary")),
        name="dense_ffn",
    )(x2, g, w1, w3, w2)


def _router_kernel(x_ref, g_ref, wr_ref, br_ref, meta_ref, cnt_ref, carry_ref, *, tm):
    i = pl.program_id(0)

    @pl.when(i == 0)
    def _():
        carry_ref[...] = jnp.zeros_like(carry_ref)

    h = _rmsnorm(x_ref[...], g_ref[...])
    logits = jnp.dot(h, wr_ref[...], preferred_element_type=F32, precision=lax.Precision.HIGHEST) + br_ref[...]
    lane = lax.broadcasted_iota(jnp.int32, logits.shape, 1).astype(F32)
    m1 = jnp.max(logits, axis=1, keepdims=True)
    i1 = jnp.min(jnp.where(logits == m1, lane, float(LANES)), axis=1, keepdims=True)
    rest = jnp.where(lane == i1, NEG_INF, logits)
    m2 = jnp.max(rest, axis=1, keepdims=True)
    i2 = jnp.min(jnp.where(rest == m2, lane, float(LANES)), axis=1, keepdims=True)
    e2 = jnp.exp(m2 - m1)
    p1 = 1.0 / (1.0 + e2)
    p2 = e2 * p1
    onehot = jnp.where(lane == i1, 1.0, 0.0) + jnp.where(lane == i2, 1.0, 0.0)
    rr = lax.broadcasted_iota(jnp.int32, (tm, tm), 0)
    cc = lax.broadcasted_iota(jnp.int32, (tm, tm), 1)
    before = jnp.where(cc < rr, 1.0, 0.0).astype(BF16)
    prefix = _dot(before, onehot.astype(BF16)) + carry_ref[...]
    r1 = jnp.sum(jnp.where(lane == i1, prefix, 0.0), axis=1, keepdims=True)
    r2 = jnp.sum(jnp.where(lane == i2, prefix, 0.0), axis=1, keepdims=True)
    meta = jnp.zeros_like(logits)
    for n, col in enumerate((i1, i2, p1, p2, r1, r2)):
        meta = jnp.where(lane == float(n), col, meta)
    meta_ref[...] = meta
    total = carry_ref[...] + jnp.sum(onehot, axis=0, keepdims=True)
    carry_ref[...] = total
    cnt_ref[...] = total


def _router(x2, g, wr, br):
    T, D = x2.shape
    tm = min(TM_FFN, T)
    c2 = lambda i: (0, 0)
    return pl.pallas_call(
        functools.partial(_router_kernel, tm=tm),
        grid=(T // tm,),
        in_specs=[
            pl.BlockSpec((tm, D), lambda i: (i, 0)),
            pl.BlockSpec((1, D), c2),
            pl.BlockSpec((D, LANES), c2),
            pl.BlockSpec((1, LANES), c2),
        ],
        out_specs=[pl.BlockSpec((tm, LANES), lambda i: (i, 0)), pl.BlockSpec((1, LANES), c2)],
        out_shape=[jax.ShapeDtypeStruct((T, LANES), F32), jax.ShapeDtypeStruct((1, LANES), F32)],
        scratch_shapes=[pltpu.VMEM((1, LANES), F32)],
        compiler_params=_cparams(("arbitrary",)),
        name="router",
    )(x2, g, wr, br)


def _row_gather_start(table_hbm, row, dst, sem):
    pltpu.make_async_copy(table_hbm.at[pl.ds(row, 1), :], dst, sem).start()


def _experts_kernel(texp_ref, src_ref, nused_ref, x_hbm, g_ref, gw_ref, w1_ref, w3_ref, w2_ref, y_ref,
                    buf_ref, sem_ref, *, tm, fc):
    i = pl.program_id(0)
    n_used = nused_ref[0]
    slot = lax.rem(i, 2)
    F = w1_ref.shape[2]

    def issue(tile, s):
        base = tile * tm

        def body(r, _):
            _row_gather_start(x_hbm, src_ref[base + r], buf_ref.at[s, pl.ds(r, 1), :], sem_ref.at[s])
            return 0

        lax.fori_loop(0, tm, body, 0, unroll=8)

    @pl.when((i == 0) & (n_used > 0))
    def _():
        issue(0, 0)

    @pl.when(i + 1 < n_used)
    def _():
        issue(i + 1, 1 - slot)

    @pl.when(i < n_used)
    def _():
        pltpu.make_async_copy(x_hbm.at[pl.ds(0, tm), :], buf_ref.at[slot], sem_ref.at[slot]).wait()
        h = _rmsnorm(buf_ref[slot], g_ref[...]).astype(BF16)
        acc = None
        for c in range(F // fc):
            cs = slice(c * fc, (c + 1) * fc)
            a = _dot(h, w1_ref[0, :, cs])
            b = _dot(h, w3_ref[0, :, cs])
            part = _dot((a * _sigmoid(a) * b).astype(BF16), w2_ref[0, cs, :])
            acc = part if acc is None else acc + part
        y_ref[...] = acc * gw_ref[...]

    @pl.when(i >= n_used)
    def _():
        y_ref[...] = jnp.zeros_like(y_ref)


def _experts(x2, g, gw, tile_expert, src, n_used, w1, w3, w2):
    T, D = x2.shape
    R = src.shape[0]
    F = w1.shape[2]
    tm = TM_MOE
    wspec = lambda shape: pl.BlockSpec(shape, lambda i, te, s, n: (te[i], 0, 0))
    grid_spec = pltpu.PrefetchScalarGridSpec(
        num_scalar_prefetch=3,
        grid=(R // tm,),
        in_specs=[
            pl.BlockSpec(memory_space=pl.ANY),
            pl.BlockSpec((1, D), lambda i, te, s, n: (0, 0)),
            pl.BlockSpec((tm, 1), lambda i, te, s, n: (i, 0)),
            wspec((1, D, F)), wspec((1, D, F)), wspec((1, F, D)),
        ],
        out_specs=pl.BlockSpec((tm, D), lambda i, te, s, n: (i, 0)),
        scratch_shapes=[pltpu.VMEM((2, tm, D), F32), pltpu.SemaphoreType.DMA((2,))],
    )
    return pl.pallas_call(
        functools.partial(_experts_kernel, tm=tm, fc=FC_MOE),
        grid_spec=grid_spec,
        out_shape=jax.ShapeDtypeStruct((R, D), F32),
        compiler_params=pltpu.CompilerParams(dimension_semantics=("arbitrary",),
                                             vmem_limit_bytes=VMEM_LIMIT_MOE),
        name="experts",
    )(tile_expert, src, n_used, x2, g, gw, w1, w3, w2)


def _combine_kernel(dst_ref, x_ref, y_hbm, o_ref, buf_ref, sem_ref, *, tc, n_tiles):
    i = pl.program_id(0)
    slot = lax.rem(i, 2)

    def issue(tile, s):
        base = tile * (2 * tc)

        def body(r, _):
            for k in range(2):
                _row_gather_start(y_hbm, dst_ref[base + 2 * r + k], buf_ref.at[s, k, pl.ds(r, 1), :],
                                  sem_ref.at[s])
            return 0

        lax.fori_loop(0, tc, body, 0, unroll=4)

    @pl.when(i == 0)
    def _():
        issue(0, 0)

    @pl.when(i + 1 < n_tiles)
    def _():
        issue(i + 1, 1 - slot)

    for k in range(2):
        pltpu.make_async_copy(y_hbm.at[pl.ds(0, tc), :], buf_ref.at[slot, k], sem_ref.at[slot]).wait()
    o_ref[...] = x_ref[...] + buf_ref[slot, 0] + buf_ref[slot, 1]


def _combine(x2, y, dst):
    T, D = x2.shape
    tc = min(TC_MOE, T)
    n_tiles = T // tc
    grid_spec = pltpu.PrefetchScalarGridSpec(
        num_scalar_prefetch=1,
        grid=(n_tiles,),
        in_specs=[pl.BlockSpec((tc, D), lambda i, d: (i, 0)), pl.BlockSpec(memory_space=pl.ANY)],
        out_specs=pl.BlockSpec((tc, D), lambda i, d: (i, 0)),
        scratch_shapes=[pltpu.VMEM((2, 2, tc, D), F32), pltpu.SemaphoreType.DMA((2,))],
    )
    return pl.pallas_call(
        functools.partial(_combine_kernel, tc=tc, n_tiles=n_tiles),
        grid_spec=grid_spec,
        out_shape=jax.ShapeDtypeStruct((T, D), F32),
        compiler_params=_cparams(("arbitrary",)),
        name="combine",
    )(dst, x2, y)


def _moe(x2, g, wr, br, w1, w3, w2):
    T, D = x2.shape
    tm = TM_MOE
    meta, cnt = _router(x2, g, wr, br)
    expert = meta[:, 0:2].astype(jnp.int32)
    prob = meta[:, 2:4]
    rank = meta[:, 4:6].astype(jnp.int32)
    counts = cnt[0, :N_EXPERTS].astype(jnp.int32)
    padded = (counts + (tm - 1)) // tm * tm
    ends = jnp.cumsum(padded)
    dst = ((ends - padded)[expert] + rank).reshape(-1)
    R = 2 * T + N_EXPERTS * tm
    src = jnp.zeros((R,), jnp.int32).at[dst].set(jnp.repeat(jnp.arange(T, dtype=jnp.int32), 2))
    gw = jnp.zeros((R,), F32).at[dst].set(prob.reshape(-1)).reshape(R, 1)
    tile_start = jnp.arange(R // tm, dtype=jnp.int32) * tm
    tile_expert = jnp.minimum(jnp.searchsorted(ends, tile_start, side="right"), N_EXPERTS - 1).astype(jnp.int32)
    n_used = (ends[-1] // tm).astype(jnp.int32).reshape(1)
    y = _experts(x2, g, gw, tile_expert, src, n_used, w1, w3, w2)
    return _combine(x2, y, dst)


def _s5_params(lam_re, lam_im, log_dt, b_re, b_im, c_re, c_im):
    G, P, H = S5_GROUPS, S5_STATE, S5_GROUP
    dt = jnp.exp(log_dt)[:, None]
    mag = jnp.exp(lam_re * dt)
    ab_re = mag * jnp.cos(lam_im * dt)
    ab_im = mag * jnp.sin(lam_im * dt)
    nr, ni = ab_re - 1.0, ab_im
    den = lam_re * lam_re + lam_im * lam_im
    k_re = (nr * lam_re + ni * lam_im) / den
    k_im = (ni * lam_re - nr * lam_im) / den
    bb_re = k_re[..., None] * b_re - k_im[..., None] * b_im
    bb_im = k_re[..., None] * b_im + k_im[..., None] * b_re
    eye = jnp.eye(G, dtype=F32)
    bm_re = jnp.einsum("gph,gk->ghkp", bb_re, eye).reshape(G * H, G * P)
    bm_im = jnp.einsum("gph,gk->ghkp", bb_im, eye).reshape(G * H, G * P)
    bmat = jnp.concatenate([bm_re, bm_im], axis=1).astype(BF16)
    cm_re = jnp.einsum("ghp,gk->kpgh", c_re, eye).reshape(G * P, G * H)
    cm_im = jnp.einsum("ghp,gk->kpgh", c_im, eye).reshape(G * P, G * H)
    cmat = jnp.concatenate([cm_re, -cm_im], axis=0).astype(BF16)
    a = jnp.stack([ab_re.reshape(-1), ab_im.reshape(-1)], axis=0)
    return bmat, cmat, a


def kernel(x, g_mix, w_in, b_forget, fox_q_gain, fox_k_gain, sb_q_gain, sb_k_gain, s5_lam_re, s5_lam_im, s5_log_dt, s5_b_re, s5_b_im, s5_c_re, s5_c_im, s5_d, s5_w_glu, conv_w, conv_b, conv_ln_g, conv_ln_b, w_branch, w_out, g_ffn, ffn_w1, ffn_w3, ffn_w2, router_w, router_b, moe_w1, moe_w3, moe_w2):
    B, L, D = x.shape
    depth = g_mix.shape[0]
    assert B == SUBLANES, "the S5 scan lays the batch along the sublanes"
    n_qkv = 3 * D_BRANCH
    o_f = D_BRANCH + n_qkv
    o_c = o_f + N_HEADS
    o_g = o_c + 2 * D_BRANCH + n_qkv
    scale = 1.0 / math.sqrt(HEAD_DIM)
    sel_np, const_np = _forget_feature_tables()
    sel = jnp.asarray(sel_np, BF16)
    const = jnp.asarray(const_np, F32)

    w_a = w_in[:, :, :o_f].astype(BF16)
    w_b = w_in[:, :, o_c:o_g].astype(BF16)
    w_f = jnp.pad(w_in[:, :, o_f:o_c], ((0, 0), (0, 0), (0, LANES - N_HEADS))).astype(BF16)
    w_g = w_in[:, :, o_g:].astype(BF16)

    for i in range(depth):
        bfg = jnp.pad(b_forget[i], (0, LANES - N_HEADS)).reshape(1, LANES)
        gains = jnp.stack([jnp.tile(fox_q_gain[i] * (scale * LOG2E), N_HEADS), jnp.tile(fox_k_gain[i], N_HEADS),
                           jnp.tile(sb_q_gain[i] * scale, N_HEADS), jnp.tile(sb_k_gain[i], N_HEADS)], axis=0)
        us5, qf, kf, vf, conv_ab, qs, ks, vs = _inproj(x, g_mix[i].reshape(1, D), w_a[i], w_b[i], w_f[i], bfg,
                                                       gains, sel, const)

        bmat, cmat, a = _s5_params(s5_lam_re[i], s5_lam_im[i], s5_log_dt[i], s5_b_re[i], s5_b_im[i],
                                   s5_c_re[i], s5_c_im[i])
        ys5 = _s5(us5.reshape(L * B, D_BRANCH), bmat, cmat, a, s5_d[i].reshape(1, D_BRANCH),
                  s5_w_glu[i].astype(BF16), B).reshape(L, B * D_BRANCH)
        yfox = _fox(qf, kf, vf)
        yconv = _conv(conv_ab, conv_w[i], conv_b[i].reshape(1, -1), conv_ln_g[i].reshape(1, -1),
                      conv_ln_b[i].reshape(1, -1))
        ysb = _sb(qs, ks, vs)

        x = _merge(x, g_mix[i].reshape(1, D), ys5, yfox, yconv, ysb, w_g[i],
                   w_branch[i].astype(BF16), w_out[i].astype(BF16))

        j = i // 2
        gf = g_ffn[i].reshape(1, D)
        if i % 2 == 0:
            x2 = _ffn(x.reshape(B * L, D), gf, ffn_w1[j].astype(BF16), ffn_w3[j].astype(BF16),
                      ffn_w2[j].astype(BF16))
        else:
            wr = jnp.pad(router_w[j], ((0, 0), (0, LANES - N_EXPERTS)))
            br = jnp.pad(router_b[j], (0, LANES - N_EXPERTS), constant_values=-1e30).reshape(1, LANES)
            x2 = _moe(x.reshape(B * L, D), gf, wr, br, moe_w1[j].astype(BF16), moe_w3[j].astype(BF16),
                      moe_w2[j].astype(BF16))
        x = x2.reshape(B, L, D)
    return x
```

```python
import functools
import math

import numpy as np

import jax
import jax.numpy as jnp
from jax import lax
from jax.experimental import pallas as pl
from jax.experimental.pallas import tpu as pltpu

F32 = jnp.float32
BF16 = jnp.bfloat16
EPS = 1e-6
LOG2E = math.log2(math.e)

D_BRANCH = 256
HEAD_DIM = 64
N_HEADS = D_BRANCH // HEAD_DIM
S5_GROUP = 16
S5_GROUPS = D_BRANCH // S5_GROUP
S5_STATE = 64
N_STATE = S5_GROUPS * S5_STATE
CONV_WIDTH = 31
CONV_HALO = 32
N_EXPERTS = 8
LANES = 128
SUBLANES = 8
VMEM_LIMIT = 56 * 1024 * 1024
D_SPREAD = N_HEADS * LANES
N_SPLIT = 3

TM_PROJ = 512
TM_MERGE = 512
TM_FFN = 512
T_ATT = 512
SB_BLOCK = 256
LC_S5 = 128
LC_CONV = 512
TM_MOE = 512
FC_MOE = 256
TC_MOE = 256
VMEM_LIMIT_MOE = 60 * 1024 * 1024

NEG_INF = float("-inf")


def _cparams(sem):
    return pltpu.CompilerParams(dimension_semantics=sem, vmem_limit_bytes=VMEM_LIMIT)


def _dot(a, b):
    return jnp.dot(a, b, preferred_element_type=F32)


def _dot_nt(a, b):
    return lax.dot_general(a, b, (((1,), (1,)), ((), ())), preferred_element_type=F32)


def _split2(x):
    hi = x.astype(BF16)
    lo = (x - hi.astype(F32)).astype(BF16)
    return hi, lo


def _rmsnorm(x, g):
    ms = jnp.mean(x * x, axis=-1, keepdims=True)
    return x * lax.rsqrt(ms + EPS) * g


def _sigmoid(x):
    return 1.0 / (1.0 + jnp.exp(-x))


def _log_sigmoid(x):
    return jnp.minimum(x, 0.0) - jnp.log(1.0 + jnp.exp(-jnp.abs(x)))


def _lane_tile(x, n):
    return x if n == 1 else jnp.concatenate([x] * n, axis=1)


def _forget_feature_tables():
    sel = np.zeros((LANES, 2 * D_SPREAD), np.float32)
    const = np.zeros((1, 2 * D_SPREAD), np.float32)
    for h in range(N_HEADS):
        for j in range(N_SPLIT):
            sel[j * N_HEADS + h, h * LANES + HEAD_DIM + j] = 1.0
            sel[j * N_HEADS + h, D_SPREAD + h * LANES + HEAD_DIM + N_SPLIT + j] = -1.0
            const[0, h * LANES + HEAD_DIM + N_SPLIT + j] = 1.0
            const[0, D_SPREAD + h * LANES + HEAD_DIM + j] = 1.0
    return sel, const


def _inproj_kernel(x_ref, g_ref, wa_ref, wb_ref, wf_ref, bf_ref, gain_ref, sel_ref, const_ref,
                   us5_ref, qf_ref, kf_ref, vf_ref, conv_ref, qs_ref, ks_ref, vs_ref, carry_ref, *, tm):
    li = pl.program_id(1)
    x = x_ref[0]
    h = _rmsnorm(x, g_ref[...]).astype(BF16)
    lane = lax.broadcasted_iota(jnp.int32, (tm, LANES), 1)

    r = lax.broadcasted_iota(jnp.int32, (D_BRANCH, D_BRANCH), 0) // HEAD_DIM
    c = lax.broadcasted_iota(jnp.int32, (D_BRANCH, D_BRANCH), 1) // HEAD_DIM
    ones_bd = jnp.where(r == c, 1.0, 0.0).astype(BF16)

    def qknorm(t, gi):
        ss = _dot((t * t).astype(BF16), ones_bd)
        return t * lax.rsqrt(ss * (1.0 / HEAD_DIM) + EPS) * gain_ref[gi:gi + 1, :]

    def spread(t, fill):
        blocks = []
        for hp in range(2):
            pair = t[:, hp * LANES:(hp + 1) * LANES]
            blocks += [pair, pltpu.roll(pair, HEAD_DIM, axis=1)]
        out = [jnp.where(lane < HEAD_DIM, blocks[n], fill(n)) for n in range(N_HEADS)]
        return jnp.concatenate(out, axis=1).astype(BF16)

    @pl.when(li == 0)
    def _():
        carry_ref[...] = jnp.zeros_like(carry_ref)

    lf = _log_sigmoid(_dot(h, wf_ref[...]) + bf_ref[...]) * LOG2E
    rr = lax.broadcasted_iota(jnp.int32, (tm, tm), 0)
    cc = lax.broadcasted_iota(jnp.int32, (tm, tm), 1)
    tri = jnp.where(cc <= rr, 1.0, 0.0).astype(BF16)

    def split3(v):
        hi = v.astype(BF16).astype(F32)
        r1 = v - hi
        mid = r1.astype(BF16).astype(F32)
        lo = (r1 - mid).astype(BF16).astype(F32)
        return hi, mid, lo

    hi, mid, lo = split3(lf)
    cum = _dot(tri, hi.astype(BF16)) + _dot(tri, mid.astype(BF16)) + _dot(tri, lo.astype(BF16)) + carry_ref[...]
    carry_ref[...] = cum[tm - 1:tm, :]
    hi, mid, lo = split3(jnp.where(lane < N_HEADS, cum, 0.0))
    packed = hi + pltpu.roll(mid, N_HEADS, axis=1) + pltpu.roll(lo, 2 * N_HEADS, axis=1)
    feat = _dot(packed.astype(BF16), sel_ref[...]) + const_ref[...]

    def feat_q(n):
        return feat[:, n * LANES:(n + 1) * LANES]

    def feat_k(n):
        return feat[:, D_SPREAD + n * LANES:D_SPREAD + (n + 1) * LANES]

    zero = lambda n: 0.0
    one = lambda n: 1.0

    us5_ref[0] = _dot(h, wa_ref[:, 0:256])
    qf_ref[0] = spread(qknorm(_dot(h, wa_ref[:, 256:512]), 0), feat_q)
    kf_ref[0] = spread(qknorm(_dot(h, wa_ref[:, 512:768]), 1), feat_k)
    vf_ref[0] = spread(_dot(h, wa_ref[:, 768:1024]), one)
    conv_ref[0] = _dot(h, wb_ref[:, 0:512])
    qs_ref[0] = spread(qknorm(_dot(h, wb_ref[:, 512:768]), 2), zero)
    ks_ref[0] = spread(qknorm(_dot(h, wb_ref[:, 768:1024]), 3), zero)
    vs_ref[0] = _dot(h, wb_ref[:, 1024:1280]).astype(BF16)


def _inproj(x, g, wa, wb, wf, bfg, gains, sel, const):
    B, L, D = x.shape
    tm = min(TM_PROJ, L)
    c2 = lambda b, l: (0, 0)
    row_spec = lambda n: pl.BlockSpec((1, tm, n), lambda b, l: (b, l, 0))
    return pl.pallas_call(
        functools.partial(_inproj_kernel, tm=tm),
        grid=(B, L // tm),
        in_specs=[
            row_spec(D),
            pl.BlockSpec((1, D), c2),
            pl.BlockSpec(wa.shape, c2),
            pl.BlockSpec(wb.shape, c2),
            pl.BlockSpec((D, LANES), c2),
            pl.BlockSpec((1, LANES), c2),
            pl.BlockSpec((4, D_BRANCH), c2),
            pl.BlockSpec(sel.shape, c2),
            pl.BlockSpec(const.shape, c2),
        ],
        out_specs=[
            row_spec(D_BRANCH),
            row_spec(D_SPREAD), row_spec(D_SPREAD), row_spec(D_SPREAD),
            row_spec(2 * D_BRANCH),
            row_spec(D_SPREAD), row_spec(D_SPREAD), row_spec(D_BRANCH),
        ],
        out_shape=[
            jax.ShapeDtypeStruct((B, L, D_BRANCH), F32),
            jax.ShapeDtypeStruct((B, L, D_SPREAD), BF16),
            jax.ShapeDtypeStruct((B, L, D_SPREAD), BF16),
            jax.ShapeDtypeStruct((B, L, D_SPREAD), BF16),
            jax.ShapeDtypeStruct((B, L, 2 * D_BRANCH), F32),
            jax.ShapeDtypeStruct((B, L, D_SPREAD), BF16),
            jax.ShapeDtypeStruct((B, L, D_SPREAD), BF16),
            jax.ShapeDtypeStruct((B, L, D_BRANCH), BF16),
        ],
        scratch_shapes=[pltpu.VMEM((1, LANES), F32)],
        compiler_params=_cparams(("parallel", "arbitrary")),
        name="inproj",
    )(x, g, wa, wb, wf, bfg, gains, sel, const)


def _s5_kernel(u_ref, bmat_ref, cmat_ref, a_ref, d_ref, wglu_ref, y_ref, utb_ref, bu_ref, st_ref, *, lc, nb):
    ci = pl.program_id(0)

    @pl.when(ci == 0)
    def _():
        st_ref[...] = jnp.zeros_like(st_ref)

    n_half = D_BRANCH // LANES
    for b in range(nb):
        for j in range(n_half):
            utb_ref.at[j][pl.ds(b, lc, stride=nb), :] = u_ref[b, :, j * LANES:(j + 1) * LANES]
    u = jnp.concatenate([utb_ref[j] for j in range(n_half)], axis=1)
    bu_ref[...] = _dot(u.astype(BF16), bmat_ref[...])

    a_re = jnp.broadcast_to(a_ref[0:1, :], (nb, N_STATE))
    a_im = jnp.broadcast_to(a_ref[1:2, :], (nb, N_STATE))

    def step(t, carry):
        s_re, s_im = carry
        r0 = pl.multiple_of(t * nb, nb)
        b_re = bu_ref[pl.ds(r0, nb), 0:N_STATE]
        b_im = bu_ref[pl.ds(r0, nb), N_STATE:2 * N_STATE]
        n_re = a_re * s_re - a_im * s_im + b_re
        n_im = a_re * s_im + a_im * s_re + b_im
        bu_ref[pl.ds(r0, nb), 0:N_STATE] = n_re
        bu_ref[pl.ds(r0, nb), N_STATE:2 * N_STATE] = n_im
        return n_re, n_im

    s_re, s_im = lax.fori_loop(0, lc, step, (st_ref[:, 0:N_STATE], st_ref[:, N_STATE:2 * N_STATE]),
                               unroll=2)
    st_ref[:, 0:N_STATE] = s_re
    st_ref[:, N_STATE:2 * N_STATE] = s_im

    y = _dot(bu_ref[...].astype(BF16), cmat_ref[...]) + d_ref[...] * u
    y = jax.nn.gelu(y, approximate=True)
    y = y * _sigmoid(_dot(y.astype(BF16), wglu_ref[...]))
    for j in range(n_half):
        utb_ref[j] = y[:, j * LANES:(j + 1) * LANES]
    for b in range(nb):
        for j in range(n_half):
            y_ref[b, :, j * LANES:(j + 1) * LANES] = utb_ref.at[j][pl.ds(b, lc, stride=nb), :].astype(y_ref.dtype)


def _s5(u, bmat, cmat, a, d, wglu):
    nb, L, _ = u.shape
    lc = min(LC_S5, L)
    const = lambda c: (0, 0)
    return pl.pallas_call(
        functools.partial(_s5_kernel, lc=lc, nb=nb),
        grid=(L // lc,),
        in_specs=[
            pl.BlockSpec((nb, lc, D_BRANCH), lambda c: (0, c, 0)),
            pl.BlockSpec((D_BRANCH, 2 * N_STATE), const),
            pl.BlockSpec((2 * N_STATE, D_BRANCH), const),
            pl.BlockSpec((2, N_STATE), const),
            pl.BlockSpec((1, D_BRANCH), const),
            pl.BlockSpec((D_BRANCH, D_BRANCH), const),
        ],
        out_specs=pl.BlockSpec((nb, lc, D_BRANCH), lambda c: (0, c, 0)),
        out_shape=jax.ShapeDtypeStruct((nb, L, D_BRANCH), BF16),
        scratch_shapes=[pltpu.VMEM((D_BRANCH // LANES, lc * nb, LANES), F32), pltpu.VMEM((lc * nb, 2 * N_STATE), F32),
                        pltpu.VMEM((nb, 2 * N_STATE), F32)],
        compiler_params=_cparams(("arbitrary",)),
        name="s5",
    )(u, bmat, cmat, a, d, wglu)


def _fox_kernel(q_ref, k_ref, v_ref, o_ref, sa_ref, sb_ref, m_ref, acc_ref, *, tq):
    qi = pl.program_id(2)
    nrep = tq // LANES
    lane = lax.broadcasted_iota(jnp.int32, (tq, LANES), 1)
    row = lax.broadcasted_iota(jnp.int32, (tq, tq), 0)
    col = lax.broadcasted_iota(jnp.int32, (tq, tq), 1)

    def logits(ki, s_ref):
        k0 = pl.multiple_of(ki * tq, tq)
        for h in range(2):
            hs = slice(h * LANES, (h + 1) * LANES)
            s_ref[h] = _dot_nt(q_ref[0, :, hs], k_ref[0, pl.ds(k0, tq), hs])

    def update(ki, s_ref, diag):
        k0 = pl.multiple_of(ki * tq, tq)
        for h in range(2):
            hs = slice(h * LANES, (h + 1) * LANES)
            s = s_ref[h]
            if diag:
                s = jnp.where(col <= row, s, NEG_INF)
            m = m_ref[h]
            m_new = jnp.maximum(m, jnp.max(s, axis=1, keepdims=True))
            p = jnp.exp2(s - _lane_tile(m_new, nrep))
            acc_ref[h] = jnp.exp2(m - m_new) * acc_ref[h] + _dot(p.astype(BF16), v_ref[0, pl.ds(k0, tq), hs])
            m_ref[h] = m_new

    m_ref[...] = jnp.full(m_ref.shape, NEG_INF, F32)
    acc_ref[...] = jnp.zeros_like(acc_ref)
    logits(0, sa_ref)

    def pair(j, _):
        logits(2 * j + 1, sb_ref)
        update(2 * j, sa_ref, False)
        logits(2 * j + 2, sa_ref)
        update(2 * j + 1, sb_ref, False)
        return 0

    lax.fori_loop(0, qi // 2, pair, 0)

    @pl.when(qi % 2 == 0)
    def _():
        update(qi, sa_ref, True)

    @pl.when(qi % 2 == 1)
    def _():
        logits(qi, sb_ref)
        update(qi - 1, sa_ref, False)
        update(qi, sb_ref, True)

    res = [acc_ref[h] / pltpu.roll(acc_ref[h], HEAD_DIM, axis=1) for h in range(2)]
    o_ref[0] = jnp.where(lane < HEAD_DIM, res[0], pltpu.roll(res[1], HEAD_DIM, axis=1)).astype(o_ref.dtype)


def _fox(q, k, v):
    B, L, _ = q.shape
    tq = min(T_ATT, L)
    return pl.pallas_call(
        functools.partial(_fox_kernel, tq=tq),
        grid=(B, 2, L // tq),
        in_specs=[
            pl.BlockSpec((1, tq, 2 * LANES), lambda b, p, i: (b, i, p)),
            pl.BlockSpec((1, L, 2 * LANES), lambda b, p, i: (b, 0, p)),
            pl.BlockSpec((1, L, 2 * LANES), lambda b, p, i: (b, 0, p)),
        ],
        out_specs=pl.BlockSpec((1, tq, LANES), lambda b, p, i: (b, i, p)),
        out_shape=jax.ShapeDtypeStruct((B, L, D_BRANCH), BF16),
        scratch_shapes=[pltpu.VMEM((2, tq, tq), F32), pltpu.VMEM((2, tq, tq), F32),
                        pltpu.VMEM((2, tq, LANES), F32), pltpu.VMEM((2, tq, LANES), F32)],
        compiler_params=_cparams(("parallel", "parallel", "arbitrary")),
        name="fox",
    )(q, k, v)


def _sb_kernel(q_ref, k_ref, v_ref, o_ref, za_ref, zb_ref, run_ref, acc_ref, *, tq, blk):
    qi = pl.program_id(2)
    n_blk = tq // blk
    lane = lax.broadcasted_iota(jnp.int32, (tq, LANES), 1)
    row = lax.broadcasted_iota(jnp.int32, (tq, tq), 0)
    col = lax.broadcasted_iota(jnp.int32, (tq, tq), 1)
    ur = lax.broadcasted_iota(jnp.int32, (blk, blk), 0)
    uc = lax.broadcasted_iota(jnp.int32, (blk, blk), 1)
    upper = jnp.where(ur > uc, 1.0, 0.0).astype(BF16)

    def logits(ki, z_ref):
        k0 = pl.multiple_of(jnp.maximum(ki, 0) * tq, tq)
        for h in range(2):
            hs = slice(h * LANES, (h + 1) * LANES)
            z_ref[h] = _dot_nt(q_ref[0, :, hs], k_ref[0, pl.ds(k0, tq), hs])

    def update(ki, z_ref, diag):
        k0 = pl.multiple_of(ki * tq, tq)
        vt = v_ref[0, pl.ds(k0, tq), :]
        for h in range(2):
            run = run_ref[h]
            z = z_ref[h]
            neg_abs = pltpu.bitcast(pltpu.bitcast(z, jnp.uint32) | jnp.uint32(0x80000000), F32)
            l1p = jnp.log(1.0 + jnp.exp2(neg_abs)) * LOG2E
            log_beta = jnp.minimum(z, 0.0) - l1p
            log_keep = log_beta - z
            if diag:
                log_keep = jnp.where(col < row, log_keep, 0.0)
            keep16 = log_keep.astype(BF16)
            after = [None] * n_blk
            for c in reversed(range(n_blk)):
                cs = slice(c * blk, (c + 1) * blk)
                raw = _dot(keep16[:, cs], upper)
                after[c] = raw + _lane_tile(run, blk // LANES)
                run = run + (raw[:, 0:1] + log_keep[:, c * blk:c * blk + 1])
            a = jnp.exp2(log_beta + jnp.concatenate(after, axis=1))
            if diag:
                a = jnp.where(col < row, a, 0.0)
            acc_ref[h] += _dot(a.astype(BF16), vt)
            run_ref[h] = run

    run_ref[...] = jnp.zeros_like(run_ref)
    acc_ref[...] = jnp.zeros_like(acc_ref)
    logits(qi, za_ref)
    logits(qi - 1, zb_ref)
    update(qi, za_ref, True)

    def pair(j, _):
        t = qi - 1 - 2 * j
        logits(t - 1, za_ref)
        update(t, zb_ref, False)
        logits(t - 2, zb_ref)
        update(t - 1, za_ref, False)
        return 0

    lax.fori_loop(0, qi // 2, pair, 0)

    @pl.when(qi % 2 == 1)
    def _():
        update(0, zb_ref, False)

    o_ref[0] = jnp.where(lane < HEAD_DIM, acc_ref[0], acc_ref[1]).astype(o_ref.dtype)


def _sb(q, k, v):
    B, L, _ = q.shape
    tq = min(T_ATT, L)
    return pl.pallas_call(
        functools.partial(_sb_kernel, tq=tq, blk=min(SB_BLOCK, tq)),
        grid=(B, 2, L // tq),
        in_specs=[
            pl.BlockSpec((1, tq, 2 * LANES), lambda b, p, i: (b, i, p)),
            pl.BlockSpec((1, L, 2 * LANES), lambda b, p, i: (b, 0, p)),
            pl.BlockSpec((1, L, LANES), lambda b, p, i: (b, 0, p)),
        ],
        out_specs=pl.BlockSpec((1, tq, LANES), lambda b, p, i: (b, i, p)),
        out_shape=jax.ShapeDtypeStruct((B, L, D_BRANCH), BF16),
        scratch_shapes=[pltpu.VMEM((2, tq, tq), F32), pltpu.VMEM((2, tq, tq), F32),
                        pltpu.VMEM((2, tq, LANES), F32), pltpu.VMEM((2, tq, LANES), F32)],
        compiler_params=_cparams(("parallel", "parallel", "arbitrary")),
        name="stickbreak",
    )(q, k, v)


def _conv_kernel(ab_ref, w_ref, b_ref, g_ref, beta_ref, o_ref, pad_ref, *, lc):
    li = pl.program_id(1)

    @pl.when(li == 0)
    def _():
        pad_ref[0:CONV_HALO, :] = jnp.zeros((CONV_HALO, D_BRANCH), F32)

    ab = ab_ref[0]
    pad_ref[CONV_HALO:CONV_HALO + lc, :] = ab[:, 0:D_BRANCH] * _sigmoid(ab[:, D_BRANCH:2 * D_BRANCH])
    off = CONV_HALO - (CONV_WIDTH - 1)
    acc = jnp.zeros((lc, D_BRANCH), F32) + b_ref[...]
    for j in range(CONV_WIDTH):
        acc = acc + w_ref[j:j + 1, :] * pad_ref[off + j:off + j + lc, :]
    pad_ref[0:CONV_HALO, :] = pad_ref[lc:lc + CONV_HALO, :]
    mu = jnp.mean(acc, axis=-1, keepdims=True)
    xc = acc - mu
    var = jnp.mean(xc * xc, axis=-1, keepdims=True)
    y = xc * lax.rsqrt(var + EPS) * g_ref[...] + beta_ref[...]
    o_ref[0] = (y * _sigmoid(y)).astype(o_ref.dtype)


def _conv(ab, w, b, g, beta):
    B, L, _ = ab.shape
    lc = min(LC_CONV, L)
    const = lambda b_, l: (0, 0)
    return pl.pallas_call(
        functools.partial(_conv_kernel, lc=lc),
        grid=(B, L // lc),
        in_specs=[
            pl.BlockSpec((1, lc, 2 * D_BRANCH), lambda b_, l: (b_, l, 0)),
            pl.BlockSpec((CONV_WIDTH, D_BRANCH), const),
            pl.BlockSpec((1, D_BRANCH), const),
            pl.BlockSpec((1, D_BRANCH), const),
            pl.BlockSpec((1, D_BRANCH), const),
        ],
        out_specs=pl.BlockSpec((1, lc, D_BRANCH), lambda b_, l: (b_, l, 0)),
        out_shape=jax.ShapeDtypeStruct((B, L, D_BRANCH), BF16),
        scratch_shapes=[pltpu.VMEM((CONV_HALO + lc, D_BRANCH), F32)],
        compiler_params=_cparams(("parallel", "arbitrary")),
        name="conv",
    )(ab, w, b, g, beta)


def _merge_kernel(x_ref, g_ref, ys5_ref, yfox_ref, yconv_ref, ysb_ref, wg_ref, wb_ref, wo_ref, o_ref):
    x = x_ref[0]
    D = x.shape[-1]
    h = _rmsnorm(x, g_ref[...]).astype(BF16)
    ys = (ys5_ref[0], yfox_ref[0], yconv_ref[0], ysb_ref[0])
    merged = None
    for n in range(4):
        gate = _sigmoid(_dot(h, wg_ref[:, n * D:(n + 1) * D]))
        term = gate * _dot(ys[n], wb_ref[n])
        merged = term if merged is None else merged + term
    o_ref[0] = x + _dot(merged.astype(BF16), wo_ref[...])


def _merge(x, g, ys5, yfox, yconv, ysb, wg, wb, wo):
    B, L, D = x.shape
    tm = min(TM_MERGE, L)
    c2 = lambda b, l: (0, 0)
    yspec = pl.BlockSpec((1, tm, D_BRANCH), lambda b, l: (b, l, 0))
    return pl.pallas_call(
        _merge_kernel,
        grid=(B, L // tm),
        in_specs=[
            pl.BlockSpec((1, tm, D), lambda b, l: (b, l, 0)),
            pl.BlockSpec((1, D), c2),
            yspec, yspec, yspec, yspec,
            pl.BlockSpec((D, 4 * D), c2),
            pl.BlockSpec((4, D_BRANCH, D), lambda b, l: (0, 0, 0)),
            pl.BlockSpec((D, D), c2),
        ],
        out_specs=pl.BlockSpec((1, tm, D), lambda b, l: (b, l, 0)),
        out_shape=jax.ShapeDtypeStruct((B, L, D), F32),
        compiler_params=_cparams(("parallel", "parallel")),
        name="merge",
    )(x, g, ys5, yfox, yconv, ysb, wg, wb, wo)


def _ffn_kernel(x_ref, g_ref, w1_ref, w3_ref, w2_ref, o_ref, h_ref, acc_ref, *, n_f):
    f = pl.program_id(1)

    @pl.when(f == 0)
    def _():
        h_ref[...] = _rmsnorm(x_ref[...], g_ref[...]).astype(BF16)
        acc_ref[...] = jnp.zeros_like(acc_ref)

    h = h_ref[...]
    a = _dot(h, w1_ref[...])
    b = _dot(h, w3_ref[...])
    acc_ref[...] += _dot((a * _sigmoid(a) * b).astype(BF16), w2_ref[...])

    @pl.when(f == n_f - 1)
    def _():
        o_ref[...] = x_ref[...] + acc_ref[...]


def _ffn(x2, g, w1, w3, w2):
    T, D = x2.shape
    F = w1.shape[1]
    tm = min(TM_FFN, T)
    n_f = 2
    fc = F // n_f
    return pl.pallas_call(
        functools.partial(_ffn_kernel, n_f=n_f),
        grid=(T // tm, n_f),
        in_specs=[
            pl.BlockSpec((tm, D), lambda i, f: (i, 0)),
            pl.BlockSpec((1, D), lambda i, f: (0, 0)),
            pl.BlockSpec((D, fc), lambda i, f: (0, f)),
            pl.BlockSpec((D, fc), lambda i, f: (0, f)),
            pl.BlockSpec((fc, D), lambda i, f: (f, 0)),
        ],
        out_specs=pl.BlockSpec((tm, D), lambda i, f: (i, 0)),
        out_shape=jax.ShapeDtypeStruct((T, D), F32),
        scratch_shapes=[pltpu.VMEM((tm, D), BF16), pltpu.VMEM((tm, D), F32)],
        compiler_params=_cparams(("parallel", "arbitrary")),
        name="dense_ffn",
    )(x2, g, w1, w3, w2)


def _router_kernel(x_ref, g_ref, wr_ref, br_ref, meta_ref, cnt_ref, carry_ref, *, tm):
    i = pl.program_id(0)

    @pl.when(i == 0)
    def _():
        carry_ref[...] = jnp.zeros_like(carry_ref)

    h = _rmsnorm(x_ref[...], g_ref[...])
    logits = jnp.dot(h, wr_ref[...], preferred_element_type=F32, precision=lax.Precision.HIGHEST) + br_ref[...]
    lane = lax.broadcasted_iota(jnp.int32, logits.shape, 1).astype(F32)
    m1 = jnp.max(logits, axis=1, keepdims=True)
    i1 = jnp.min(jnp.where(logits == m1, lane, float(LANES)), axis=1, keepdims=True)
    rest = jnp.where(lane == i1, NEG_INF, logits)
    m2 = jnp.max(rest, axis=1, keepdims=True)
    i2 = jnp.min(jnp.where(rest == m2, lane, float(LANES)), axis=1, keepdims=True)
    e2 = jnp.exp(m2 - m1)
    p1 = 1.0 / (1.0 + e2)
    p2 = e2 * p1
    onehot = jnp.where(lane == i1, 1.0, 0.0) + jnp.where(lane == i2, 1.0, 0.0)
    rr = lax.broadcasted_iota(jnp.int32, (tm, tm), 0)
    cc = lax.broadcasted_iota(jnp.int32, (tm, tm), 1)
    before = jnp.where(cc < rr, 1.0, 0.0).astype(BF16)
    prefix = _dot(before, onehot.astype(BF16)) + carry_ref[...]
    r1 = jnp.sum(jnp.where(lane == i1, prefix, 0.0), axis=1, keepdims=True)
    r2 = jnp.sum(jnp.where(lane == i2, prefix, 0.0), axis=1, keepdims=True)
    meta = jnp.zeros_like(logits)
    for n, col in enumerate((i1, i2, p1, p2, r1, r2)):
        meta = jnp.where(lane == float(n), col, meta)
    meta_ref[...] = meta
    total = carry_ref[...] + jnp.sum(onehot, axis=0, keepdims=True)
    carry_ref[...] = total
    cnt_ref[...] = total


def _router(x2, g, wr, br):
    T, D = x2.shape
    tm = min(TM_FFN, T)
    c2 = lambda i: (0, 0)
    return pl.pallas_call(
        functools.partial(_router_kernel, tm=tm),
        grid=(T // tm,),
        in_specs=[
            pl.BlockSpec((tm, D), lambda i: (i, 0)),
            pl.BlockSpec((1, D), c2),
            pl.BlockSpec((D, LANES), c2),
            pl.BlockSpec((1, LANES), c2),
        ],
        out_specs=[pl.BlockSpec((tm, LANES), lambda i: (i, 0)), pl.BlockSpec((1, LANES), c2)],
        out_shape=[jax.ShapeDtypeStruct((T, LANES), F32), jax.ShapeDtypeStruct((1, LANES), F32)],
        scratch_shapes=[pltpu.VMEM((1, LANES), F32)],
        compiler_params=_cparams(("arbitrary",)),
        name="router",
    )(x2, g, wr, br)


def _row_gather_start(table_hbm, row, dst, sem):
    pltpu.make_async_copy(table_hbm.at[pl.ds(row, 1), :], dst, sem).start()


def _experts_kernel(texp_ref, src_ref, nused_ref, x_hbm, g_ref, w1_ref, w3_ref, w2_ref, y_ref,
                    buf_ref, sem_ref, *, tm, fc, n_tiles):
    i = pl.program_id(0)
    n_used = nused_ref[0]
    slot = lax.rem(i, 2)
    nxt_base = jnp.minimum(i + 1, n_tiles - 1) * tm
    n_c = w1_ref.shape[2] // fc
    per = -(-tm // n_c)

    def gather(base, s, r):
        _row_gather_start(x_hbm, src_ref[base + r], buf_ref.at[s, pl.ds(r, 1), :], sem_ref.at[s])

    def gather_all(base, s):
        def body(r, _):
            gather(base, s, r)
            return 0
        lax.fori_loop(0, tm, body, 0, unroll=8)

    def wait(s):
        pltpu.make_async_copy(x_hbm.at[pl.ds(0, tm), :], buf_ref.at[s], sem_ref.at[s]).wait()

    @pl.when(i == 0)
    def _():
        gather_all(0, 0)

    wait(slot)

    @pl.when(i < n_used)
    def _():
        h = _rmsnorm(buf_ref[slot], g_ref[...]).astype(BF16)
        acc = None
        for c in range(n_c):
            cs = slice(c * fc, (c + 1) * fc)
            a = _dot(h, w1_ref[0, :, cs])
            b = _dot(h, w3_ref[0, :, cs])
            part = _dot((a * _sigmoid(a) * b).astype(BF16), w2_ref[0, cs, :])
            acc = part if acc is None else acc + part
            for r in range(c * per, min((c + 1) * per, tm)):
                gather(nxt_base, 1 - slot, r)
        y_ref[...] = acc

    @pl.when(i >= n_used)
    def _():
        gather_all(nxt_base, 1 - slot)
        y_ref[...] = jnp.zeros_like(y_ref)

    @pl.when(i == n_tiles - 1)
    def _():
        wait(1 - slot)


def _experts(x2, g, tile_expert, src, n_used, w1, w3, w2):
    T, D = x2.shape
    R = src.shape[0]
    F = w1.shape[2]
    tm = TM_MOE
    n_tiles = R // tm
    wspec = lambda shape: pl.BlockSpec(shape, lambda i, te, s, n: (te[i], 0, 0))
    grid_spec = pltpu.PrefetchScalarGridSpec(
        num_scalar_prefetch=3,
        grid=(n_tiles,),
        in_specs=[
            pl.BlockSpec(memory_space=pl.ANY),
            pl.BlockSpec((1, D), lambda i, te, s, n: (0, 0)),
            wspec((1, D, F)), wspec((1, D, F)), wspec((1, F, D)),
        ],
        out_specs=pl.BlockSpec((tm, D), lambda i, te, s, n: (i, 0)),
        scratch_shapes=[pltpu.VMEM((2, tm, D), F32), pltpu.SemaphoreType.DMA((2,))],
    )
    return pl.pallas_call(
        functools.partial(_experts_kernel, tm=tm, fc=FC_MOE, n_tiles=n_tiles),
        grid_spec=grid_spec,
        out_shape=jax.ShapeDtypeStruct((R, D), F32),
        compiler_params=pltpu.CompilerParams(dimension_semantics=("arbitrary",),
                                             vmem_limit_bytes=VMEM_LIMIT_MOE),
        name="experts",
    )(tile_expert, src, n_used, x2, g, w1, w3, w2)


def _combine_kernel(dst_ref, x_ref, p_ref, y_hbm, o_ref, buf_ref, sem_ref, *, tc, n_tiles):
    i = pl.program_id(0)
    slot = lax.rem(i, 2)

    def issue(tile, s):
        base = tile * (2 * tc)

        def body(r, _):
            for k in range(2):
                _row_gather_start(y_hbm, dst_ref[base + 2 * r + k], buf_ref.at[s, k, pl.ds(r, 1), :],
                                  sem_ref.at[s])
            return 0

        lax.fori_loop(0, tc, body, 0, unroll=4)

    @pl.when(i == 0)
    def _():
        issue(0, 0)

    @pl.when(i + 1 < n_tiles)
    def _():
        issue(i + 1, 1 - slot)

    for k in range(2):
        pltpu.make_async_copy(y_hbm.at[pl.ds(0, tc), :], buf_ref.at[slot, k], sem_ref.at[slot]).wait()
    o_ref[...] = x_ref[...] + p_ref[:, 0:1] * buf_ref[slot, 0] + p_ref[:, 1:2] * buf_ref[slot, 1]


def _combine(x2, prob, y, dst):
    T, D = x2.shape
    tc = min(TC_MOE, T)
    n_tiles = T // tc
    grid_spec = pltpu.PrefetchScalarGridSpec(
        num_scalar_prefetch=1,
        grid=(n_tiles,),
        in_specs=[pl.BlockSpec((tc, D), lambda i, d: (i, 0)), pl.BlockSpec((tc, 2), lambda i, d: (i, 0)),
                  pl.BlockSpec(memory_space=pl.ANY)],
        out_specs=pl.BlockSpec((tc, D), lambda i, d: (i, 0)),
        scratch_shapes=[pltpu.VMEM((2, 2, tc, D), F32), pltpu.SemaphoreType.DMA((2,))],
    )
    return pl.pallas_call(
        functools.partial(_combine_kernel, tc=tc, n_tiles=n_tiles),
        grid_spec=grid_spec,
        out_shape=jax.ShapeDtypeStruct((T, D), F32),
        compiler_params=_cparams(("arbitrary",)),
        name="combine",
    )(dst, x2, prob, y)


def _moe(x2, g, wr, br, w1, w3, w2):
    T, D = x2.shape
    tm = TM_MOE
    meta, cnt = _router(x2, g, wr, br)
    expert = meta[:, 0:2].astype(jnp.int32)
    prob = meta[:, 2:4]
    rank = meta[:, 4:6].astype(jnp.int32)
    counts = cnt[0, :N_EXPERTS].astype(jnp.int32)
    padded = (counts + (tm - 1)) // tm * tm
    ends = jnp.cumsum(padded)
    dst = ((ends - padded)[expert] + rank).reshape(-1)
    R = 2 * T + N_EXPERTS * tm
    src = jnp.zeros((R,), jnp.int32).at[dst].set(jnp.repeat(jnp.arange(T, dtype=jnp.int32), 2))
    tile_start = jnp.arange(R // tm, dtype=jnp.int32) * tm
    tile_expert = jnp.minimum(jnp.searchsorted(ends, tile_start, side="right"), N_EXPERTS - 1).astype(jnp.int32)
    n_used = (ends[-1] // tm).astype(jnp.int32).reshape(1)
    y = _experts(x2, g, tile_expert, src, n_used, w1, w3, w2)
    return _combine(x2, prob, y, dst)


def _s5_params(lam_re, lam_im, log_dt, b_re, b_im, c_re, c_im):
    G, P, H = S5_GROUPS, S5_STATE, S5_GROUP
    dt = jnp.exp(log_dt)[:, None]
    mag = jnp.exp(lam_re * dt)
    ab_re = mag * jnp.cos(lam_im * dt)
    ab_im = mag * jnp.sin(lam_im * dt)
    nr, ni = ab_re - 1.0, ab_im
    den = lam_re * lam_re + lam_im * lam_im
    k_re = (nr * lam_re + ni * lam_im) / den
    k_im = (ni * lam_re - nr * lam_im) / den
    bb_re = k_re[..., None] * b_re - k_im[..., None] * b_im
    bb_im = k_re[..., None] * b_im + k_im[..., None] * b_re
    eye = jnp.eye(G, dtype=F32)
    bm_re = jnp.einsum("gph,gk->ghkp", bb_re, eye).reshape(G * H, G * P)
    bm_im = jnp.einsum("gph,gk->ghkp", bb_im, eye).reshape(G * H, G * P)
    bmat = jnp.concatenate([bm_re, bm_im], axis=1).astype(BF16)
    cm_re = jnp.einsum("ghp,gk->kpgh", c_re, eye).reshape(G * P, G * H)
    cm_im = jnp.einsum("ghp,gk->kpgh", c_im, eye).reshape(G * P, G * H)
    cmat = jnp.concatenate([cm_re, -cm_im], axis=0).astype(BF16)
    a = jnp.stack([ab_re.reshape(-1), ab_im.reshape(-1)], axis=0)
    return bmat, cmat, a


def kernel(x, g_mix, w_in, b_forget, fox_q_gain, fox_k_gain, sb_q_gain, sb_k_gain, s5_lam_re, s5_lam_im, s5_log_dt, s5_b_re, s5_b_im, s5_c_re, s5_c_im, s5_d, s5_w_glu, conv_w, conv_b, conv_ln_g, conv_ln_b, w_branch, w_out, g_ffn, ffn_w1, ffn_w3, ffn_w2, router_w, router_b, moe_w1, moe_w3, moe_w2):
    B, L, D = x.shape
    depth = g_mix.shape[0]
    assert B == SUBLANES, "the S5 scan lays the batch along the sublanes"
    n_qkv = 3 * D_BRANCH
    o_f = D_BRANCH + n_qkv
    o_c = o_f + N_HEADS
    o_g = o_c + 2 * D_BRANCH + n_qkv
    scale = 1.0 / math.sqrt(HEAD_DIM)
    sel_np, const_np = _forget_feature_tables()
    sel = jnp.asarray(sel_np, BF16)
    const = jnp.asarray(const_np, F32)

    w_a = w_in[:, :, :o_f].astype(BF16)
    w_b = w_in[:, :, o_c:o_g].astype(BF16)
    w_f = jnp.pad(w_in[:, :, o_f:o_c], ((0, 0), (0, 0), (0, LANES - N_HEADS))).astype(BF16)
    w_g = w_in[:, :, o_g:].astype(BF16)

    for i in range(depth):
        bfg = jnp.pad(b_forget[i], (0, LANES - N_HEADS)).reshape(1, LANES)
        gains = jnp.stack([jnp.tile(fox_q_gain[i] * (scale * LOG2E), N_HEADS), jnp.tile(fox_k_gain[i], N_HEADS),
                           jnp.tile(sb_q_gain[i] * (scale * LOG2E), N_HEADS), jnp.tile(sb_k_gain[i], N_HEADS)],
                          axis=0)
        us5, qf, kf, vf, conv_ab, qs, ks, vs = _inproj(x, g_mix[i].reshape(1, D), w_a[i], w_b[i], w_f[i], bfg,
                                                       gains, sel, const)

        bmat, cmat, a = _s5_params(s5_lam_re[i], s5_lam_im[i], s5_log_dt[i], s5_b_re[i], s5_b_im[i],
                                   s5_c_re[i], s5_c_im[i])
        ys5 = _s5(us5, bmat, cmat, a, s5_d[i].reshape(1, D_BRANCH), s5_w_glu[i].astype(BF16))
        yfox = _fox(qf, kf, vf)
        yconv = _conv(conv_ab, conv_w[i], conv_b[i].reshape(1, -1), conv_ln_g[i].reshape(1, -1),
                      conv_ln_b[i].reshape(1, -1))
        ysb = _sb(qs, ks, vs)

        x = _merge(x, g_mix[i].reshape(1, D), ys5, yfox, yconv, ysb, w_g[i],
                   w_branch[i].astype(BF16), w_out[i].astype(BF16))

        j = i // 2
        gf = g_ffn[i].reshape(1, D)
        if i % 2 == 0:
            x2 = _ffn(x.reshape(B * L, D), gf, ffn_w1[j].astype(BF16), ffn_w3[j].astype(BF16),
                      ffn_w2[j].astype(BF16))
        else:
            wr = jnp.pad(router_w[j], ((0, 0), (0, LANES - N_EXPERTS)))
            br = jnp.pad(router_b[j], (0, LANES - N_EXPERTS), constant_values=-1e30).reshape(1, LANES)
            x2 = _moe(x.reshape(B * L, D), gf, wr, br, moe_w1[j].astype(BF16), moe_w3[j].astype(BF16),
                      moe_w2[j].astype(BF16))
        x = x2.reshape(B, L, D)
    return x
```

```python
import functools
import math

import numpy as np

import jax
import jax.numpy as jnp
from jax import lax
from jax.experimental import pallas as pl
from jax.experimental.pallas import tpu as pltpu

F32 = jnp.float32
BF16 = jnp.bfloat16
EPS = 1e-6
LOG2E = math.log2(math.e)

D_BRANCH = 256
HEAD_DIM = 64
N_HEADS = D_BRANCH // HEAD_DIM
S5_GROUP = 16
S5_GROUPS = D_BRANCH // S5_GROUP
S5_STATE = 64
N_STATE = S5_GROUPS * S5_STATE
CONV_WIDTH = 31
CONV_HALO = 32
N_EXPERTS = 8
LANES = 128
SUBLANES = 8
VMEM_LIMIT = 56 * 1024 * 1024
D_SPREAD = N_HEADS * LANES
N_SPLIT = 3

TM_PROJ = 512
TM_MERGE = 512
TM_FFN = 512
T_ATT = 512
SB_BLOCK = 256
LC_S5 = 128
LC_CONV = 512
TM_MOE = 512
FC_FFN = 256
TT_MOE = 1024
TC_MOE = 256
VMEM_LIMIT_FFN = 60 * 1024 * 1024

NEG_INF = float("-inf")


def _cparams(sem):
    return pltpu.CompilerParams(dimension_semantics=sem, vmem_limit_bytes=VMEM_LIMIT)


def _dot(a, b):
    return jnp.dot(a, b, preferred_element_type=F32)


def _dot_nt(a, b):
    return lax.dot_general(a, b, (((1,), (1,)), ((), ())), preferred_element_type=F32)


def _split2(x):
    hi = x.astype(BF16)
    lo = (x - hi.astype(F32)).astype(BF16)
    return hi, lo


def _rmsnorm(x, g):
    ms = jnp.mean(x * x, axis=-1, keepdims=True)
    return x * lax.rsqrt(ms + EPS) * g


def _sigmoid(x):
    return 1.0 / (1.0 + jnp.exp(-x))


def _log_sigmoid(x):
    return jnp.minimum(x, 0.0) - jnp.log(1.0 + jnp.exp(-jnp.abs(x)))


def _lane_tile(x, n):
    return x if n == 1 else jnp.concatenate([x] * n, axis=1)


def _forget_feature_tables():
    sel = np.zeros((LANES, 2 * D_SPREAD), np.float32)
    const = np.zeros((1, 2 * D_SPREAD), np.float32)
    for h in range(N_HEADS):
        for j in range(N_SPLIT):
            sel[j * N_HEADS + h, h * LANES + HEAD_DIM + j] = 1.0
            sel[j * N_HEADS + h, D_SPREAD + h * LANES + HEAD_DIM + N_SPLIT + j] = -1.0
            const[0, h * LANES + HEAD_DIM + N_SPLIT + j] = 1.0
            const[0, D_SPREAD + h * LANES + HEAD_DIM + j] = 1.0
    return sel, const


def _inproj_kernel(x_ref, g_ref, wa_ref, wb_ref, wf_ref, bf_ref, gain_ref, sel_ref, const_ref,
                   us5_ref, qf_ref, kf_ref, vf_ref, conv_ref, qs_ref, ks_ref, vs_ref, carry_ref, *, tm):
    li = pl.program_id(1)
    x = x_ref[0]
    h = _rmsnorm(x, g_ref[...]).astype(BF16)
    lane = lax.broadcasted_iota(jnp.int32, (tm, LANES), 1)

    r = lax.broadcasted_iota(jnp.int32, (D_BRANCH, D_BRANCH), 0) // HEAD_DIM
    c = lax.broadcasted_iota(jnp.int32, (D_BRANCH, D_BRANCH), 1) // HEAD_DIM
    ones_bd = jnp.where(r == c, 1.0, 0.0).astype(BF16)

    def qknorm(t, gi):
        ss = _dot((t * t).astype(BF16), ones_bd)
        return t * lax.rsqrt(ss * (1.0 / HEAD_DIM) + EPS) * gain_ref[gi:gi + 1, :]

    def spread(t, fill):
        blocks = []
        for hp in range(2):
            pair = t[:, hp * LANES:(hp + 1) * LANES]
            blocks += [pair, pltpu.roll(pair, HEAD_DIM, axis=1)]
        out = [jnp.where(lane < HEAD_DIM, blocks[n], fill(n)) for n in range(N_HEADS)]
        return jnp.concatenate(out, axis=1).astype(BF16)

    @pl.when(li == 0)
    def _():
        carry_ref[...] = jnp.zeros_like(carry_ref)

    lf = _log_sigmoid(_dot(h, wf_ref[...]) + bf_ref[...]) * LOG2E
    rr = lax.broadcasted_iota(jnp.int32, (tm, tm), 0)
    cc = lax.broadcasted_iota(jnp.int32, (tm, tm), 1)
    tri = jnp.where(cc <= rr, 1.0, 0.0).astype(BF16)

    def split3(v):
        hi = v.astype(BF16).astype(F32)
        r1 = v - hi
        mid = r1.astype(BF16).astype(F32)
        lo = (r1 - mid).astype(BF16).astype(F32)
        return hi, mid, lo

    hi, mid, lo = split3(lf)
    cum = _dot(tri, hi.astype(BF16)) + _dot(tri, mid.astype(BF16)) + _dot(tri, lo.astype(BF16)) + carry_ref[...]
    carry_ref[...] = cum[tm - 1:tm, :]
    hi, mid, lo = split3(jnp.where(lane < N_HEADS, cum, 0.0))
    packed = hi + pltpu.roll(mid, N_HEADS, axis=1) + pltpu.roll(lo, 2 * N_HEADS, axis=1)
    feat = _dot(packed.astype(BF16), sel_ref[...]) + const_ref[...]

    def feat_q(n):
        return feat[:, n * LANES:(n + 1) * LANES]

    def feat_k(n):
        return feat[:, D_SPREAD + n * LANES:D_SPREAD + (n + 1) * LANES]

    zero = lambda n: 0.0
    one = lambda n: 1.0

    us5_ref[0] = _dot(h, wa_ref[:, 0:256])
    qf_ref[0] = spread(qknorm(_dot(h, wa_ref[:, 256:512]), 0), feat_q)
    kf_ref[0] = spread(qknorm(_dot(h, wa_ref[:, 512:768]), 1), feat_k)
    vf_ref[0] = spread(_dot(h, wa_ref[:, 768:1024]), one)
    conv_ref[0] = _dot(h, wb_ref[:, 0:512])
    qs_ref[0] = spread(qknorm(_dot(h, wb_ref[:, 512:768]), 2), zero)
    ks_ref[0] = spread(qknorm(_dot(h, wb_ref[:, 768:1024]), 3), zero)
    vs_ref[0] = _dot(h, wb_ref[:, 1024:1280]).astype(BF16)


def _inproj(x, g, wa, wb, wf, bfg, gains, sel, const):
    B, L, D = x.shape
    tm = min(TM_PROJ, L)
    c2 = lambda b, l: (0, 0)
    row_spec = lambda n: pl.BlockSpec((1, tm, n), lambda b, l: (b, l, 0))
    return pl.pallas_call(
        functools.partial(_inproj_kernel, tm=tm),
        grid=(B, L // tm),
        in_specs=[
            row_spec(D),
            pl.BlockSpec((1, D), c2),
            pl.BlockSpec(wa.shape, c2),
            pl.BlockSpec(wb.shape, c2),
            pl.BlockSpec((D, LANES), c2),
            pl.BlockSpec((1, LANES), c2),
            pl.BlockSpec((4, D_BRANCH), c2),
            pl.BlockSpec(sel.shape, c2),
            pl.BlockSpec(const.shape, c2),
        ],
        out_specs=[
            row_spec(D_BRANCH),
            row_spec(D_SPREAD), row_spec(D_SPREAD), row_spec(D_SPREAD),
            row_spec(2 * D_BRANCH),
            row_spec(D_SPREAD), row_spec(D_SPREAD), row_spec(D_BRANCH),
        ],
        out_shape=[
            jax.ShapeDtypeStruct((B, L, D_BRANCH), F32),
            jax.ShapeDtypeStruct((B, L, D_SPREAD), BF16),
            jax.ShapeDtypeStruct((B, L, D_SPREAD), BF16),
            jax.ShapeDtypeStruct((B, L, D_SPREAD), BF16),
            jax.ShapeDtypeStruct((B, L, 2 * D_BRANCH), F32),
            jax.ShapeDtypeStruct((B, L, D_SPREAD), BF16),
            jax.ShapeDtypeStruct((B, L, D_SPREAD), BF16),
            jax.ShapeDtypeStruct((B, L, D_BRANCH), BF16),
        ],
        scratch_shapes=[pltpu.VMEM((1, LANES), F32)],
        compiler_params=_cparams(("parallel", "arbitrary")),
        name="inproj",
    )(x, g, wa, wb, wf, bfg, gains, sel, const)


def _s5_kernel(u_ref, bmat_ref, cmat_ref, a_ref, d_ref, wglu_ref, y_ref, utb_ref, bu_ref, st_ref, *, lc, nb):
    ci = pl.program_id(0)

    @pl.when(ci == 0)
    def _():
        st_ref[...] = jnp.zeros_like(st_ref)

    n_half = D_BRANCH // LANES
    for b in range(nb):
        for j in range(n_half):
            utb_ref.at[j][pl.ds(b, lc, stride=nb), :] = u_ref[b, :, j * LANES:(j + 1) * LANES]
    u = jnp.concatenate([utb_ref[j] for j in range(n_half)], axis=1)
    bu_ref[...] = _dot(u.astype(BF16), bmat_ref[...])

    a_re = jnp.broadcast_to(a_ref[0:1, :], (nb, N_STATE))
    a_im = jnp.broadcast_to(a_ref[1:2, :], (nb, N_STATE))

    def step(t, carry):
        s_re, s_im = carry
        r0 = pl.multiple_of(t * nb, nb)
        b_re = bu_ref[pl.ds(r0, nb), 0:N_STATE]
        b_im = bu_ref[pl.ds(r0, nb), N_STATE:2 * N_STATE]
        n_re = a_re * s_re - a_im * s_im + b_re
        n_im = a_re * s_im + a_im * s_re + b_im
        bu_ref[pl.ds(r0, nb), 0:N_STATE] = n_re
        bu_ref[pl.ds(r0, nb), N_STATE:2 * N_STATE] = n_im
        return n_re, n_im

    s_re, s_im = lax.fori_loop(0, lc, step, (st_ref[:, 0:N_STATE], st_ref[:, N_STATE:2 * N_STATE]),
                               unroll=2)
    st_ref[:, 0:N_STATE] = s_re
    st_ref[:, N_STATE:2 * N_STATE] = s_im

    y = _dot(bu_ref[...].astype(BF16), cmat_ref[...]) + d_ref[...] * u
    y = jax.nn.gelu(y, approximate=True)
    y = y * _sigmoid(_dot(y.astype(BF16), wglu_ref[...]))
    for j in range(n_half):
        utb_ref[j] = y[:, j * LANES:(j + 1) * LANES]
    for b in range(nb):
        for j in range(n_half):
            y_ref[b, :, j * LANES:(j + 1) * LANES] = utb_ref.at[j][pl.ds(b, lc, stride=nb), :].astype(y_ref.dtype)


def _s5(u, bmat, cmat, a, d, wglu):
    nb, L, _ = u.shape
    lc = min(LC_S5, L)
    const = lambda c: (0, 0)
    return pl.pallas_call(
        functools.partial(_s5_kernel, lc=lc, nb=nb),
        grid=(L // lc,),
        in_specs=[
            pl.BlockSpec((nb, lc, D_BRANCH), lambda c: (0, c, 0)),
            pl.BlockSpec((D_BRANCH, 2 * N_STATE), const),
            pl.BlockSpec((2 * N_STATE, D_BRANCH), const),
            pl.BlockSpec((2, N_STATE), const),
            pl.BlockSpec((1, D_BRANCH), const),
            pl.BlockSpec((D_BRANCH, D_BRANCH), const),
        ],
        out_specs=pl.BlockSpec((nb, lc, D_BRANCH), lambda c: (0, c, 0)),
        out_shape=jax.ShapeDtypeStruct((nb, L, D_BRANCH), BF16),
        scratch_shapes=[pltpu.VMEM((D_BRANCH // LANES, lc * nb, LANES), F32), pltpu.VMEM((lc * nb, 2 * N_STATE), F32),
                        pltpu.VMEM((nb, 2 * N_STATE), F32)],
        compiler_params=_cparams(("arbitrary",)),
        name="s5",
    )(u, bmat, cmat, a, d, wglu)


def _fox_kernel(q_ref, k_ref, v_ref, o_ref, sa_ref, sb_ref, m_ref, acc_ref, *, tq):
    qi = pl.program_id(2)
    nrep = tq // LANES
    lane = lax.broadcasted_iota(jnp.int32, (tq, LANES), 1)
    row = lax.broadcasted_iota(jnp.int32, (tq, tq), 0)
    col = lax.broadcasted_iota(jnp.int32, (tq, tq), 1)

    def logits(ki, s_ref):
        k0 = pl.multiple_of(ki * tq, tq)
        for h in range(2):
            hs = slice(h * LANES, (h + 1) * LANES)
            s_ref[h] = _dot_nt(q_ref[0, :, hs], k_ref[0, pl.ds(k0, tq), hs])

    def update(ki, s_ref, diag):
        k0 = pl.multiple_of(ki * tq, tq)
        for h in range(2):
            hs = slice(h * LANES, (h + 1) * LANES)
            s = s_ref[h]
            if diag:
                s = jnp.where(col <= row, s, NEG_INF)
            m = m_ref[h]
            m_new = jnp.maximum(m, jnp.max(s, axis=1, keepdims=True))
            p = jnp.exp2(s - _lane_tile(m_new, nrep))
            acc_ref[h] = jnp.exp2(m - m_new) * acc_ref[h] + _dot(p.astype(BF16), v_ref[0, pl.ds(k0, tq), hs])
            m_ref[h] = m_new

    m_ref[...] = jnp.full(m_ref.shape, NEG_INF, F32)
    acc_ref[...] = jnp.zeros_like(acc_ref)
    logits(0, sa_ref)

    def pair(j, _):
        logits(2 * j + 1, sb_ref)
        update(2 * j, sa_ref, False)
        logits(2 * j + 2, sa_ref)
        update(2 * j + 1, sb_ref, False)
        return 0

    lax.fori_loop(0, qi // 2, pair, 0)

    @pl.when(qi % 2 == 0)
    def _():
        update(qi, sa_ref, True)

    @pl.when(qi % 2 == 1)
    def _():
        logits(qi, sb_ref)
        update(qi - 1, sa_ref, False)
        update(qi, sb_ref, True)

    res = [acc_ref[h] / pltpu.roll(acc_ref[h], HEAD_DIM, axis=1) for h in range(2)]
    o_ref[0] = jnp.where(lane < HEAD_DIM, res[0], pltpu.roll(res[1], HEAD_DIM, axis=1)).astype(o_ref.dtype)


def _fox(q, k, v):
    B, L, _ = q.shape
    tq = min(T_ATT, L)
    return pl.pallas_call(
        functools.partial(_fox_kernel, tq=tq),
        grid=(B, 2, L // tq),
        in_specs=[
            pl.BlockSpec((1, tq, 2 * LANES), lambda b, p, i: (b, i, p)),
            pl.BlockSpec((1, L, 2 * LANES), lambda b, p, i: (b, 0, p)),
            pl.BlockSpec((1, L, 2 * LANES), lambda b, p, i: (b, 0, p)),
        ],
        out_specs=pl.BlockSpec((1, tq, LANES), lambda b, p, i: (b, i, p)),
        out_shape=jax.ShapeDtypeStruct((B, L, D_BRANCH), BF16),
        scratch_shapes=[pltpu.VMEM((2, tq, tq), F32), pltpu.VMEM((2, tq, tq), F32),
                        pltpu.VMEM((2, tq, LANES), F32), pltpu.VMEM((2, tq, LANES), F32)],
        compiler_params=_cparams(("parallel", "parallel", "arbitrary")),
        name="fox",
    )(q, k, v)


def _sb_kernel(q_ref, k_ref, v_ref, o_ref, za_ref, zb_ref, run_ref, acc_ref, *, tq, blk):
    qi = pl.program_id(2)
    n_blk = tq // blk
    lane = lax.broadcasted_iota(jnp.int32, (tq, LANES), 1)
    row = lax.broadcasted_iota(jnp.int32, (tq, tq), 0)
    col = lax.broadcasted_iota(jnp.int32, (tq, tq), 1)
    ur = lax.broadcasted_iota(jnp.int32, (blk, blk), 0)
    uc = lax.broadcasted_iota(jnp.int32, (blk, blk), 1)
    upper = jnp.where(ur > uc, 1.0, 0.0).astype(BF16)

    def logits(ki, z_ref):
        k0 = pl.multiple_of(jnp.maximum(ki, 0) * tq, tq)
        for h in range(2):
            hs = slice(h * LANES, (h + 1) * LANES)
            z_ref[h] = _dot_nt(q_ref[0, :, hs], k_ref[0, pl.ds(k0, tq), hs])

    def update(ki, z_ref, diag):
        k0 = pl.multiple_of(ki * tq, tq)
        vt = v_ref[0, pl.ds(k0, tq), :]
        for h in range(2):
            run = run_ref[h]
            z = z_ref[h]
            neg_abs = pltpu.bitcast(pltpu.bitcast(z, jnp.uint32) | jnp.uint32(0x80000000), F32)
            l1p = jnp.log(1.0 + jnp.exp2(neg_abs)) * LOG2E
            log_beta = jnp.minimum(z, 0.0) - l1p
            log_keep = log_beta - z
            if diag:
                log_keep = jnp.where(col < row, log_keep, 0.0)
            keep16 = log_keep.astype(BF16)
            after = [None] * n_blk
            for c in reversed(range(n_blk)):
                cs = slice(c * blk, (c + 1) * blk)
                raw = _dot(keep16[:, cs], upper)
                after[c] = raw + _lane_tile(run, blk // LANES)
                run = run + (raw[:, 0:1] + log_keep[:, c * blk:c * blk + 1])
            a = jnp.exp2(log_beta + jnp.concatenate(after, axis=1))
            if diag:
                a = jnp.where(col < row, a, 0.0)
            acc_ref[h] += _dot(a.astype(BF16), vt)
            run_ref[h] = run

    run_ref[...] = jnp.zeros_like(run_ref)
    acc_ref[...] = jnp.zeros_like(acc_ref)
    logits(qi, za_ref)
    logits(qi - 1, zb_ref)
    update(qi, za_ref, True)

    def pair(j, _):
        t = qi - 1 - 2 * j
        logits(t - 1, za_ref)
        update(t, zb_ref, False)
        logits(t - 2, zb_ref)
        update(t - 1, za_ref, False)
        return 0

    lax.fori_loop(0, qi // 2, pair, 0)

    @pl.when(qi % 2 == 1)
    def _():
        update(0, zb_ref, False)

    o_ref[0] = jnp.where(lane < HEAD_DIM, acc_ref[0], acc_ref[1]).astype(o_ref.dtype)


def _sb(q, k, v):
    B, L, _ = q.shape
    tq = min(T_ATT, L)
    return pl.pallas_call(
        functools.partial(_sb_kernel, tq=tq, blk=min(SB_BLOCK, tq)),
        grid=(B, 2, L // tq),
        in_specs=[
            pl.BlockSpec((1, tq, 2 * LANES), lambda b, p, i: (b, i, p)),
            pl.BlockSpec((1, L, 2 * LANES), lambda b, p, i: (b, 0, p)),
            pl.BlockSpec((1, L, LANES), lambda b, p, i: (b, 0, p)),
        ],
        out_specs=pl.BlockSpec((1, tq, LANES), lambda b, p, i: (b, i, p)),
        out_shape=jax.ShapeDtypeStruct((B, L, D_BRANCH), BF16),
        scratch_shapes=[pltpu.VMEM((2, tq, tq), F32), pltpu.VMEM((2, tq, tq), F32),
                        pltpu.VMEM((2, tq, LANES), F32), pltpu.VMEM((2, tq, LANES), F32)],
        compiler_params=_cparams(("parallel", "parallel", "arbitrary")),
        name="stickbreak",
    )(q, k, v)


def _conv_kernel(ab_ref, w_ref, b_ref, g_ref, beta_ref, o_ref, pad_ref, sh_ref, *, lc):
    li = pl.program_id(1)

    @pl.when(li == 0)
    def _():
        pad_ref[0:CONV_HALO, :] = jnp.zeros((CONV_HALO, D_BRANCH), F32)

    ab = ab_ref[0]
    pad_ref[CONV_HALO:CONV_HALO + lc, :] = ab[:, 0:D_BRANCH] * _sigmoid(ab[:, D_BRANCH:2 * D_BRANCH])
    span = lc + CONV_HALO - SUBLANES
    for ph in range(1, SUBLANES):
        sh_ref[ph - 1] = pad_ref[ph:ph + span, :]
    off = CONV_HALO - (CONV_WIDTH - 1)
    acc = jnp.zeros((lc, D_BRANCH), F32) + b_ref[...]
    for j in range(CONV_WIDTH):
        ph, base = (off + j) % SUBLANES, (off + j) // SUBLANES * SUBLANES
        tap = pad_ref[base:base + lc, :] if ph == 0 else sh_ref[ph - 1, base:base + lc, :]
        acc = acc + w_ref[j:j + 1, :] * tap
    pad_ref[0:CONV_HALO, :] = pad_ref[lc:lc + CONV_HALO, :]
    mu = jnp.mean(acc, axis=-1, keepdims=True)
    xc = acc - mu
    var = jnp.mean(xc * xc, axis=-1, keepdims=True)
    y = xc * lax.rsqrt(var + EPS) * g_ref[...] + beta_ref[...]
    o_ref[0] = (y * _sigmoid(y)).astype(o_ref.dtype)


def _conv(ab, w, b, g, beta):
    B, L, _ = ab.shape
    lc = min(LC_CONV, L)
    const = lambda b_, l: (0, 0)
    return pl.pallas_call(
        functools.partial(_conv_kernel, lc=lc),
        grid=(B, L // lc),
        in_specs=[
            pl.BlockSpec((1, lc, 2 * D_BRANCH), lambda b_, l: (b_, l, 0)),
            pl.BlockSpec((CONV_WIDTH, D_BRANCH), const),
            pl.BlockSpec((1, D_BRANCH), const),
            pl.BlockSpec((1, D_BRANCH), const),
            pl.BlockSpec((1, D_BRANCH), const),
        ],
        out_specs=pl.BlockSpec((1, lc, D_BRANCH), lambda b_, l: (b_, l, 0)),
        out_shape=jax.ShapeDtypeStruct((B, L, D_BRANCH), BF16),
        scratch_shapes=[pltpu.VMEM((CONV_HALO + lc, D_BRANCH), F32),
                        pltpu.VMEM((SUBLANES - 1, CONV_HALO + lc - SUBLANES, D_BRANCH), F32)],
        compiler_params=_cparams(("parallel", "arbitrary")),
        name="conv",
    )(ab, w, b, g, beta)


def _merge_kernel(x_ref, g_ref, ys5_ref, yfox_ref, yconv_ref, ysb_ref, wg_ref, wb_ref, wo_ref, o_ref):
    x = x_ref[0]
    D = x.shape[-1]
    h = _rmsnorm(x, g_ref[...]).astype(BF16)
    ys = (ys5_ref[0], yfox_ref[0], yconv_ref[0], ysb_ref[0])
    merged = None
    for n in range(4):
        gate = _sigmoid(_dot(h, wg_ref[:, n * D:(n + 1) * D]))
        term = gate * _dot(ys[n], wb_ref[n])
        merged = term if merged is None else merged + term
    o_ref[0] = x + _dot(merged.astype(BF16), wo_ref[...])


def _merge(x, g, ys5, yfox, yconv, ysb, wg, wb, wo):
    B, L, D = x.shape
    tm = min(TM_MERGE, L)
    c2 = lambda b, l: (0, 0)
    yspec = pl.BlockSpec((1, tm, D_BRANCH), lambda b, l: (b, l, 0))
    return pl.pallas_call(
        _merge_kernel,
        grid=(B, L // tm),
        in_specs=[
            pl.BlockSpec((1, tm, D), lambda b, l: (b, l, 0)),
            pl.BlockSpec((1, D), c2),
            yspec, yspec, yspec, yspec,
            pl.BlockSpec((D, 4 * D), c2),
            pl.BlockSpec((4, D_BRANCH, D), lambda b, l: (0, 0, 0)),
            pl.BlockSpec((D, D), c2),
        ],
        out_specs=pl.BlockSpec((1, tm, D), lambda b, l: (b, l, 0)),
        out_shape=jax.ShapeDtypeStruct((B, L, D), F32),
        compiler_params=_cparams(("parallel", "parallel")),
        name="merge",
    )(x, g, ys5, yfox, yconv, ysb, wg, wb, wo)


def _swiglu(h, w1_ref, w3_ref, w2_ref, lead, fc):
    acc = None
    for c in range(w1_ref.shape[-1] // fc):
        cs = slice(c * fc, (c + 1) * fc)
        a = _dot(h, w1_ref[lead + (slice(None), cs)])
        b = _dot(h, w3_ref[lead + (slice(None), cs)])
        part = _dot((a * _sigmoid(a) * b).astype(BF16), w2_ref[lead + (cs, slice(None))])
        acc = part if acc is None else acc + part
    return acc


def _ffn_kernel(x_ref, g_ref, w1_ref, w3_ref, w2_ref, o_ref, *, fc):
    x = x_ref[...]
    h = _rmsnorm(x, g_ref[...]).astype(BF16)
    o_ref[...] = x + _swiglu(h, w1_ref, w3_ref, w2_ref, (), fc)


def _ffn(x2, g, w1, w3, w2):
    T, D = x2.shape
    F = w1.shape[1]
    tm = min(TM_FFN, T)
    c2 = lambda i: (0, 0)
    return pl.pallas_call(
        functools.partial(_ffn_kernel, fc=FC_FFN),
        grid=(T // tm,),
        in_specs=[
            pl.BlockSpec((tm, D), lambda i: (i, 0)),
            pl.BlockSpec((1, D), c2),
            pl.BlockSpec((D, F), c2), pl.BlockSpec((D, F), c2), pl.BlockSpec((F, D), c2),
        ],
        out_specs=pl.BlockSpec((tm, D), lambda i: (i, 0)),
        out_shape=jax.ShapeDtypeStruct((T, D), F32),
        compiler_params=pltpu.CompilerParams(dimension_semantics=("parallel",),
                                             vmem_limit_bytes=VMEM_LIMIT_FFN),
        name="dense_ffn",
    )(x2, g, w1, w3, w2)


def _router_kernel(x_ref, g_ref, wr_ref, br_ref, meta_ref, cnt_ref, carry_ref, *, tm):
    i = pl.program_id(0)

    @pl.when(i == 0)
    def _():
        carry_ref[...] = jnp.zeros_like(carry_ref)

    h = _rmsnorm(x_ref[...], g_ref[...])
    logits = jnp.dot(h, wr_ref[...], preferred_element_type=F32, precision=lax.Precision.HIGHEST) + br_ref[...]
    lane = lax.broadcasted_iota(jnp.int32, logits.shape, 1).astype(F32)
    m1 = jnp.max(logits, axis=1, keepdims=True)
    i1 = jnp.min(jnp.where(logits == m1, lane, float(LANES)), axis=1, keepdims=True)
    rest = jnp.where(lane == i1, NEG_INF, logits)
    m2 = jnp.max(rest, axis=1, keepdims=True)
    i2 = jnp.min(jnp.where(rest == m2, lane, float(LANES)), axis=1, keepdims=True)
    e2 = jnp.exp(m2 - m1)
    p1 = 1.0 / (1.0 + e2)
    p2 = e2 * p1
    onehot = jnp.where(lane == i1, 1.0, 0.0) + jnp.where(lane == i2, 1.0, 0.0)
    rr = lax.broadcasted_iota(jnp.int32, (tm, tm), 0)
    cc = lax.broadcasted_iota(jnp.int32, (tm, tm), 1)
    before = jnp.where(cc < rr, 1.0, 0.0).astype(BF16)
    prefix = _dot(before, onehot.astype(BF16)) + carry_ref[...]
    r1 = jnp.sum(jnp.where(lane == i1, prefix, 0.0), axis=1, keepdims=True)
    r2 = jnp.sum(jnp.where(lane == i2, prefix, 0.0), axis=1, keepdims=True)
    meta = jnp.zeros_like(logits)
    for n, col in enumerate((i1, i2, p1, p2, r1, r2)):
        meta = jnp.where(lane == float(n), col, meta)
    meta_ref[...] = meta
    total = carry_ref[...] + jnp.sum(onehot, axis=0, keepdims=True)
    carry_ref[...] = total
    cnt_ref[...] = total


def _router(x2, g, wr, br):
    T, D = x2.shape
    tm = min(TM_FFN, T)
    c2 = lambda i: (0, 0)
    return pl.pallas_call(
        functools.partial(_router_kernel, tm=tm),
        grid=(T // tm,),
        in_specs=[
            pl.BlockSpec((tm, D), lambda i: (i, 0)),
            pl.BlockSpec((1, D), c2),
            pl.BlockSpec((D, LANES), c2),
            pl.BlockSpec((1, LANES), c2),
        ],
        out_specs=[pl.BlockSpec((tm, LANES), lambda i: (i, 0)), pl.BlockSpec((1, LANES), c2)],
        out_shape=[jax.ShapeDtypeStruct((T, LANES), F32), jax.ShapeDtypeStruct((1, LANES), F32)],
        scratch_shapes=[pltpu.VMEM((1, LANES), F32)],
        compiler_params=_cparams(("arbitrary",)),
        name="router",
    )(x2, g, wr, br)


def _dispatch_kernel(dst_ref, ztile_ref, x_hbm, xs_hbm, zero_ref, sem_ref, zsem_ref, *, tt, tm, n_steps):
    i = pl.program_id(0)

    @pl.when(i == 0)
    def _():
        zero_ref[...] = jnp.zeros_like(zero_ref)
        for e in range(2 * N_EXPERTS):
            @pl.when(ztile_ref[e] >= 0)
            def _():
                start = pl.multiple_of(ztile_ref[e], tm)
                fill = pltpu.make_async_copy(zero_ref, xs_hbm.at[pl.ds(start, tm), :], zsem_ref)
                fill.start()
                fill.wait()

    def body(r, _):
        t = i * tt + r
        for k in range(2):
            pltpu.make_async_copy(x_hbm.at[pl.ds(t, 1), :], xs_hbm.at[pl.ds(dst_ref[2 * t + k], 1), :],
                                  sem_ref).start()
        return 0

    lax.fori_loop(0, tt, body, 0, unroll=4)

    def wait_one_step():
        for _ in range(2):
            pltpu.make_async_copy(x_hbm.at[pl.ds(0, tt), :], xs_hbm.at[pl.ds(0, tt), :], sem_ref).wait()

    @pl.when(i > 0)
    def _():
        wait_one_step()

    @pl.when(i == n_steps - 1)
    def _():
        wait_one_step()


def _dispatch(x2, dst, ztile, R):
    T, D = x2.shape
    tt = min(TT_MOE, T)
    n_steps = T // tt
    grid_spec = pltpu.PrefetchScalarGridSpec(
        num_scalar_prefetch=2,
        grid=(n_steps,),
        in_specs=[pl.BlockSpec(memory_space=pl.ANY)],
        out_specs=pl.BlockSpec(memory_space=pl.ANY),
        scratch_shapes=[pltpu.VMEM((TM_MOE, D), F32), pltpu.SemaphoreType.DMA(()), pltpu.SemaphoreType.DMA(())],
    )
    return pl.pallas_call(
        functools.partial(_dispatch_kernel, tt=tt, tm=TM_MOE, n_steps=n_steps),
        grid_spec=grid_spec,
        out_shape=jax.ShapeDtypeStruct((R, D), F32),
        compiler_params=_cparams(("arbitrary",)),
        name="dispatch",
    )(dst, ztile, x2)


def _experts_kernel(texp_ref, nused_ref, xs_ref, g_ref, w1_ref, w3_ref, w2_ref, y_ref, *, fc):
    i = pl.program_id(0)

    @pl.when(i < nused_ref[0])
    def _():
        h = _rmsnorm(xs_ref[...], g_ref[...]).astype(BF16)
        y_ref[...] = _swiglu(h, w1_ref, w3_ref, w2_ref, (0,), fc)

    @pl.when(i >= nused_ref[0])
    def _():
        y_ref[...] = jnp.zeros_like(y_ref)


def _experts(xs, g, tile_expert, n_used, w1, w3, w2):
    R, D = xs.shape
    F = w1.shape[2]
    tm = TM_MOE
    wspec = lambda shape: pl.BlockSpec(shape, lambda i, te, n: (te[i], 0, 0))
    grid_spec = pltpu.PrefetchScalarGridSpec(
        num_scalar_prefetch=2,
        grid=(R // tm,),
        in_specs=[
            pl.BlockSpec((tm, D), lambda i, te, n: (jnp.minimum(i, n[0] - 1), 0)),
            pl.BlockSpec((1, D), lambda i, te, n: (0, 0)),
            wspec((1, D, F)), wspec((1, D, F)), wspec((1, F, D)),
        ],
        out_specs=pl.BlockSpec((tm, D), lambda i, te, n: (i, 0)),
    )
    return pl.pallas_call(
        functools.partial(_experts_kernel, fc=FC_FFN),
        grid_spec=grid_spec,
        out_shape=jax.ShapeDtypeStruct((R, D), F32),
        compiler_params=pltpu.CompilerParams(dimension_semantics=("arbitrary",),
                                             vmem_limit_bytes=VMEM_LIMIT_FFN),
        name="experts",
    )(tile_expert, n_used, xs, g, w1, w3, w2)


def _combine_kernel(dst_ref, x_ref, p_ref, y_hbm, o_ref, buf_ref, sem_ref, *, tc, n_tiles):
    i = pl.program_id(0)
    slot = lax.rem(i, 2)

    def issue(tile, s):
        base = tile * (2 * tc)

        def body(r, _):
            for k in range(2):
                pltpu.make_async_copy(y_hbm.at[pl.ds(dst_ref[base + 2 * r + k], 1), :],
                                      buf_ref.at[s, k, pl.ds(r, 1), :], sem_ref.at[s]).start(priority=k)
            return 0

        lax.fori_loop(0, tc, body, 0, unroll=4)

    @pl.when(i == 0)
    def _():
        issue(0, 0)

    @pl.when(i + 1 < n_tiles)
    def _():
        issue(i + 1, 1 - slot)

    for k in range(2):
        pltpu.make_async_copy(y_hbm.at[pl.ds(0, tc), :], buf_ref.at[slot, k], sem_ref.at[slot]).wait()
    o_ref[...] = x_ref[...] + p_ref[:, 0:1] * buf_ref[slot, 0] + p_ref[:, 1:2] * buf_ref[slot, 1]


def _combine(x2, prob, y, dst):
    T, D = x2.shape
    tc = min(TC_MOE, T)
    n_tiles = T // tc
    grid_spec = pltpu.PrefetchScalarGridSpec(
        num_scalar_prefetch=1,
        grid=(n_tiles,),
        in_specs=[pl.BlockSpec((tc, D), lambda i, d: (i, 0)), pl.BlockSpec((tc, 2), lambda i, d: (i, 0)),
                  pl.BlockSpec(memory_space=pl.ANY)],
        out_specs=pl.BlockSpec((tc, D), lambda i, d: (i, 0)),
        scratch_shapes=[pltpu.VMEM((2, 2, tc, D), F32), pltpu.SemaphoreType.DMA((2,))],
    )
    return pl.pallas_call(
        functools.partial(_combine_kernel, tc=tc, n_tiles=n_tiles),
        grid_spec=grid_spec,
        out_shape=jax.ShapeDtypeStruct((T, D), F32),
        compiler_params=_cparams(("arbitrary",)),
        name="combine",
    )(dst, x2, prob, y)


def _moe(x2, g, wr, br, w1, w3, w2):
    T, D = x2.shape
    tm = TM_MOE
    meta, cnt = _router(x2, g, wr, br)
    expert = meta[:, 0:2].astype(jnp.int32)
    prob = meta[:, 2:4]
    rank = meta[:, 4:6].astype(jnp.int32)
    counts = cnt[0, :N_EXPERTS].astype(jnp.int32)
    padded = (counts + (tm - 1)) // tm * tm
    ends = jnp.cumsum(padded)
    dst = ((ends - padded)[expert] + rank).reshape(-1)
    R = 2 * T + N_EXPERTS * tm
    tile_start = jnp.arange(R // tm, dtype=jnp.int32) * tm
    tile_expert = jnp.minimum(jnp.searchsorted(ends, tile_start, side="right"), N_EXPERTS - 1).astype(jnp.int32)
    n_used = (ends[-1] // tm).astype(jnp.int32).reshape(1)
    last_tile = jnp.where(padded > 0, ends - tm, -1)
    tail = ends[-1] + jnp.arange(N_EXPERTS, dtype=jnp.int32) * tm
    ztile = jnp.concatenate([last_tile, jnp.where(tail < R, tail, -1)]).astype(jnp.int32)
    xs = _dispatch(x2, dst, ztile, R)
    y = _experts(xs, g, tile_expert, n_used, w1, w3, w2)
    return _combine(x2, prob, y, dst)


def _s5_params(lam_re, lam_im, log_dt, b_re, b_im, c_re, c_im):
    G, P, H = S5_GROUPS, S5_STATE, S5_GROUP
    dt = jnp.exp(log_dt)[:, None]
    mag = jnp.exp(lam_re * dt)
    ab_re = mag * jnp.cos(lam_im * dt)
    ab_im = mag * jnp.sin(lam_im * dt)
    nr, ni = ab_re - 1.0, ab_im
    den = lam_re * lam_re + lam_im * lam_im
    k_re = (nr * lam_re + ni * lam_im) / den
    k_im = (ni * lam_re - nr * lam_im) / den
    bb_re = k_re[..., None] * b_re - k_im[..., None] * b_im
    bb_im = k_re[..., None] * b_im + k_im[..., None] * b_re
    eye = jnp.eye(G, dtype=F32)
    bm_re = jnp.einsum("gph,gk->ghkp", bb_re, eye).reshape(G * H, G * P)
    bm_im = jnp.einsum("gph,gk->ghkp", bb_im, eye).reshape(G * H, G * P)
    bmat = jnp.concatenate([bm_re, bm_im], axis=1).astype(BF16)
    cm_re = jnp.einsum("ghp,gk->kpgh", c_re, eye).reshape(G * P, G * H)
    cm_im = jnp.einsum("ghp,gk->kpgh", c_im, eye).reshape(G * P, G * H)
    cmat = jnp.concatenate([cm_re, -cm_im], axis=0).astype(BF16)
    a = jnp.stack([ab_re.reshape(-1), ab_im.reshape(-1)], axis=0)
    return bmat, cmat, a


def kernel(x, g_mix, w_in, b_forget, fox_q_gain, fox_k_gain, sb_q_gain, sb_k_gain, s5_lam_re, s5_lam_im, s5_log_dt, s5_b_re, s5_b_im, s5_c_re, s5_c_im, s5_d, s5_w_glu, conv_w, conv_b, conv_ln_g, conv_ln_b, w_branch, w_out, g_ffn, ffn_w1, ffn_w3, ffn_w2, router_w, router_b, moe_w1, moe_w3, moe_w2):
    B, L, D = x.shape
    depth = g_mix.shape[0]
    assert B == SUBLANES, "the S5 scan lays the batch along the sublanes"
    n_qkv = 3 * D_BRANCH
    o_f = D_BRANCH + n_qkv
    o_c = o_f + N_HEADS
    o_g = o_c + 2 * D_BRANCH + n_qkv
    scale = 1.0 / math.sqrt(HEAD_DIM)
    sel_np, const_np = _forget_feature_tables()
    sel = jnp.asarray(sel_np, BF16)
    const = jnp.asarray(const_np, F32)

    w_a = w_in[:, :, :o_f].astype(BF16)
    w_b = w_in[:, :, o_c:o_g].astype(BF16)
    w_f = jnp.pad(w_in[:, :, o_f:o_c], ((0, 0), (0, 0), (0, LANES - N_HEADS))).astype(BF16)
    w_g = w_in[:, :, o_g:].astype(BF16)

    for i in range(depth):
        bfg = jnp.pad(b_forget[i], (0, LANES - N_HEADS)).reshape(1, LANES)
        gains = jnp.stack([jnp.tile(fox_q_gain[i] * (scale * LOG2E), N_HEADS), jnp.tile(fox_k_gain[i], N_HEADS),
                           jnp.tile(sb_q_gain[i] * (scale * LOG2E), N_HEADS), jnp.tile(sb_k_gain[i], N_HEADS)],
                          axis=0)
        us5, qf, kf, vf, conv_ab, qs, ks, vs = _inproj(x, g_mix[i].reshape(1, D), w_a[i], w_b[i], w_f[i], bfg,
                                                       gains, sel, const)

        bmat, cmat, a = _s5_params(s5_lam_re[i], s5_lam_im[i], s5_log_dt[i], s5_b_re[i], s5_b_im[i],
                                   s5_c_re[i], s5_c_im[i])
        ys5 = _s5(us5, bmat, cmat, a, s5_d[i].reshape(1, D_BRANCH), s5_w_glu[i].astype(BF16))
        yfox = _fox(qf, kf, vf)
        yconv = _conv(conv_ab, conv_w[i], conv_b[i].reshape(1, -1), conv_ln_g[i].reshape(1, -1),
                      conv_ln_b[i].reshape(1, -1))
        ysb = _sb(qs, ks, vs)

        x = _merge(x, g_mix[i].reshape(1, D), ys5, yfox, yconv, ysb, w_g[i],
                   w_branch[i].astype(BF16), w_out[i].astype(BF16))

        j = i // 2
        gf = g_ffn[i].reshape(1, D)
        if i % 2 == 0:
            x2 = _ffn(x.reshape(B * L, D), gf, ffn_w1[j].astype(BF16), ffn_w3[j].astype(BF16),
                      ffn_w2[j].astype(BF16))
        else:
            wr = jnp.pad(router_w[j], ((0, 0), (0, LANES - N_EXPERTS)))
            br = jnp.pad(router_b[j], (0, LANES - N_EXPERTS), constant_values=-1e30).reshape(1, LANES)
            x2 = _moe(x.reshape(B * L, D), gf, wr, br, moe_w1[j].astype(BF16), moe_w3[j].astype(BF16),
                      moe_w2[j].astype(BF16))
        x = x2.reshape(B, L, D)
    return x
```

```python
import functools
import math

import numpy as np

import jax
import jax.numpy as jnp
from jax import lax
from jax.experimental import pallas as pl
from jax.experimental.pallas import tpu as pltpu

F32 = jnp.float32
BF16 = jnp.bfloat16
EPS = 1e-6
LOG2E = math.log2(math.e)

D_BRANCH = 256
HEAD_DIM = 64
N_HEADS = D_BRANCH // HEAD_DIM
S5_GROUP = 16
S5_GROUPS = D_BRANCH // S5_GROUP
S5_STATE = 64
N_STATE = S5_GROUPS * S5_STATE
CONV_WIDTH = 31
CONV_HALO = 32
N_EXPERTS = 8
LANES = 128
SUBLANES = 8
VMEM_LIMIT = 56 * 1024 * 1024
D_SPREAD = N_HEADS * LANES
N_SPLIT = 3

TM_PROJ = 512
TM_MERGE = 512
TM_FFN = 512
T_ATT = 512
SB_BLOCK = 256
LC_S5 = 128
LC_CONV = 512
TM_MOE = 512
FC_FFN = 256
TT_MOE = 1024
TC_MOE = 256
VMEM_LIMIT_FFN = 60 * 1024 * 1024

NEG_INF = float("-inf")


def _cparams(sem):
    return pltpu.CompilerParams(dimension_semantics=sem, vmem_limit_bytes=VMEM_LIMIT)


def _dot(a, b):
    return jnp.dot(a, b, preferred_element_type=F32)


def _dot_nt(a, b):
    return lax.dot_general(a, b, (((1,), (1,)), ((), ())), preferred_element_type=F32)


def _split2(x):
    hi = x.astype(BF16)
    lo = (x - hi.astype(F32)).astype(BF16)
    return hi, lo


def _rmsnorm(x, g):
    ms = jnp.mean(x * x, axis=-1, keepdims=True)
    return x * lax.rsqrt(ms + EPS) * g


def _sigmoid(x):
    return 1.0 / (1.0 + jnp.exp(-x))


def _log_sigmoid(x):
    return jnp.minimum(x, 0.0) - jnp.log(1.0 + jnp.exp(-jnp.abs(x)))


def _lane_cat(parts):
    return parts[0] if len(parts) == 1 else jnp.concatenate(parts, axis=1)


def _lane_tile(x, n):
    return _lane_cat([x] * n)


def _forget_feature_tables():
    sel = np.zeros((LANES, 2 * D_SPREAD), np.float32)
    const = np.zeros((1, 2 * D_SPREAD), np.float32)
    for h in range(N_HEADS):
        for j in range(N_SPLIT):
            sel[j * N_HEADS + h, h * LANES + HEAD_DIM + j] = 1.0
            sel[j * N_HEADS + h, D_SPREAD + h * LANES + HEAD_DIM + N_SPLIT + j] = -1.0
            const[0, h * LANES + HEAD_DIM + N_SPLIT + j] = 1.0
            const[0, D_SPREAD + h * LANES + HEAD_DIM + j] = 1.0
    return sel, const


def _inproj_kernel(x_ref, g_ref, wa_ref, wb_ref, wf_ref, bf_ref, gain_ref, sel_ref, const_ref,
                   us5_ref, qf_ref, kf_ref, vf_ref, conv_ref, qs_ref, ks_ref, vs_ref, carry_ref, *, tm):
    li = pl.program_id(1)
    x = x_ref[0]
    h = _rmsnorm(x, g_ref[...]).astype(BF16)
    lane = lax.broadcasted_iota(jnp.int32, (tm, LANES), 1)

    r = lax.broadcasted_iota(jnp.int32, (D_BRANCH, D_BRANCH), 0) // HEAD_DIM
    c = lax.broadcasted_iota(jnp.int32, (D_BRANCH, D_BRANCH), 1) // HEAD_DIM
    ones_bd = jnp.where(r == c, 1.0, 0.0).astype(BF16)

    def qknorm(t, gi):
        ss = _dot((t * t).astype(BF16), ones_bd)
        return t * lax.rsqrt(ss * (1.0 / HEAD_DIM) + EPS) * gain_ref[gi:gi + 1, :]

    def spread(t, fill):
        blocks = []
        for hp in range(2):
            pair = t[:, hp * LANES:(hp + 1) * LANES]
            blocks += [pair, pltpu.roll(pair, HEAD_DIM, axis=1)]
        out = [jnp.where(lane < HEAD_DIM, blocks[n], fill(n)) for n in range(N_HEADS)]
        return jnp.concatenate(out, axis=1).astype(BF16)

    @pl.when(li == 0)
    def _():
        carry_ref[...] = jnp.zeros_like(carry_ref)

    lf = _log_sigmoid(_dot(h, wf_ref[...]) + bf_ref[...]) * LOG2E
    rr = lax.broadcasted_iota(jnp.int32, (tm, tm), 0)
    cc = lax.broadcasted_iota(jnp.int32, (tm, tm), 1)
    tri = jnp.where(cc <= rr, 1.0, 0.0).astype(BF16)

    def split3(v):
        hi = v.astype(BF16).astype(F32)
        r1 = v - hi
        mid = r1.astype(BF16).astype(F32)
        lo = (r1 - mid).astype(BF16).astype(F32)
        return hi, mid, lo

    hi, mid, lo = split3(lf)
    cum = _dot(tri, hi.astype(BF16)) + _dot(tri, mid.astype(BF16)) + _dot(tri, lo.astype(BF16)) + carry_ref[...]
    carry_ref[...] = cum[tm - 1:tm, :]
    hi, mid, lo = split3(jnp.where(lane < N_HEADS, cum, 0.0))
    packed = hi + pltpu.roll(mid, N_HEADS, axis=1) + pltpu.roll(lo, 2 * N_HEADS, axis=1)
    feat = _dot(packed.astype(BF16), sel_ref[...]) + const_ref[...]

    def feat_q(n):
        return feat[:, n * LANES:(n + 1) * LANES]

    def feat_k(n):
        return feat[:, D_SPREAD + n * LANES:D_SPREAD + (n + 1) * LANES]

    zero = lambda n: 0.0
    one = lambda n: 1.0

    us5_ref[0] = _dot(h, wa_ref[:, 0:256])
    qf_ref[0] = spread(qknorm(_dot(h, wa_ref[:, 256:512]), 0), feat_q)
    kf_ref[0] = spread(qknorm(_dot(h, wa_ref[:, 512:768]), 1), feat_k)
    vf_ref[0] = spread(_dot(h, wa_ref[:, 768:1024]), one)
    conv_ref[0] = _dot(h, wb_ref[:, 0:512])
    qs_ref[0] = spread(qknorm(_dot(h, wb_ref[:, 512:768]), 2), zero)
    ks_ref[0] = spread(qknorm(_dot(h, wb_ref[:, 768:1024]), 3), zero)
    vs_ref[0] = _dot(h, wb_ref[:, 1024:1280]).astype(BF16)


def _inproj(x, g, wa, wb, wf, bfg, gains, sel, const):
    B, L, D = x.shape
    tm = min(TM_PROJ, L)
    c2 = lambda b, l: (0, 0)
    row_spec = lambda n: pl.BlockSpec((1, tm, n), lambda b, l: (b, l, 0))
    return pl.pallas_call(
        functools.partial(_inproj_kernel, tm=tm),
        grid=(B, L // tm),
        in_specs=[
            row_spec(D),
            pl.BlockSpec((1, D), c2),
            pl.BlockSpec(wa.shape, c2),
            pl.BlockSpec(wb.shape, c2),
            pl.BlockSpec((D, LANES), c2),
            pl.BlockSpec((1, LANES), c2),
            pl.BlockSpec((4, D_BRANCH), c2),
            pl.BlockSpec(sel.shape, c2),
            pl.BlockSpec(const.shape, c2),
        ],
        out_specs=[
            row_spec(D_BRANCH),
            row_spec(D_SPREAD), row_spec(D_SPREAD), row_spec(D_SPREAD),
            row_spec(2 * D_BRANCH),
            row_spec(D_SPREAD), row_spec(D_SPREAD), row_spec(D_BRANCH),
        ],
        out_shape=[
            jax.ShapeDtypeStruct((B, L, D_BRANCH), F32),
            jax.ShapeDtypeStruct((B, L, D_SPREAD), BF16),
            jax.ShapeDtypeStruct((B, L, D_SPREAD), BF16),
            jax.ShapeDtypeStruct((B, L, D_SPREAD), BF16),
            jax.ShapeDtypeStruct((B, L, 2 * D_BRANCH), F32),
            jax.ShapeDtypeStruct((B, L, D_SPREAD), BF16),
            jax.ShapeDtypeStruct((B, L, D_SPREAD), BF16),
            jax.ShapeDtypeStruct((B, L, D_BRANCH), BF16),
        ],
        scratch_shapes=[pltpu.VMEM((1, LANES), F32)],
        compiler_params=_cparams(("parallel", "arbitrary")),
        name="inproj",
    )(x, g, wa, wb, wf, bfg, gains, sel, const)


def _s5_kernel(u_ref, bmat_ref, cmat_ref, a_ref, d_ref, wglu_ref, y_ref, utb_ref, bu_ref, st_ref, *, lc, nb):
    ci = pl.program_id(0)

    @pl.when(ci == 0)
    def _():
        st_ref[...] = jnp.zeros_like(st_ref)

    n_half = D_BRANCH // LANES
    for b in range(nb):
        for j in range(n_half):
            utb_ref.at[j][pl.ds(b, lc, stride=nb), :] = u_ref[b, :, j * LANES:(j + 1) * LANES]
    u = jnp.concatenate([utb_ref[j] for j in range(n_half)], axis=1)
    bu_ref[...] = _dot(u.astype(BF16), bmat_ref[...])

    a_re = jnp.broadcast_to(a_ref[0:1, :], (nb, N_STATE))
    a_im = jnp.broadcast_to(a_ref[1:2, :], (nb, N_STATE))

    def step(t, carry):
        s_re, s_im = carry
        r0 = pl.multiple_of(t * nb, nb)
        b_re = bu_ref[pl.ds(r0, nb), 0:N_STATE]
        b_im = bu_ref[pl.ds(r0, nb), N_STATE:2 * N_STATE]
        n_re = a_re * s_re - a_im * s_im + b_re
        n_im = a_re * s_im + a_im * s_re + b_im
        bu_ref[pl.ds(r0, nb), 0:N_STATE] = n_re
        bu_ref[pl.ds(r0, nb), N_STATE:2 * N_STATE] = n_im
        return n_re, n_im

    s_re, s_im = lax.fori_loop(0, lc, step, (st_ref[:, 0:N_STATE], st_ref[:, N_STATE:2 * N_STATE]),
                               unroll=2)
    st_ref[:, 0:N_STATE] = s_re
    st_ref[:, N_STATE:2 * N_STATE] = s_im

    y = _dot(bu_ref[...].astype(BF16), cmat_ref[...]) + d_ref[...] * u
    y = jax.nn.gelu(y, approximate=True)
    y = y * _sigmoid(_dot(y.astype(BF16), wglu_ref[...]))
    for j in range(n_half):
        utb_ref[j] = y[:, j * LANES:(j + 1) * LANES]
    for b in range(nb):
        for j in range(n_half):
            y_ref[b, :, j * LANES:(j + 1) * LANES] = utb_ref.at[j][pl.ds(b, lc, stride=nb), :].astype(y_ref.dtype)


def _s5(u, bmat, cmat, a, d, wglu):
    nb, L, _ = u.shape
    lc = min(LC_S5, L)
    const = lambda c: (0, 0)
    return pl.pallas_call(
        functools.partial(_s5_kernel, lc=lc, nb=nb),
        grid=(L // lc,),
        in_specs=[
            pl.BlockSpec((nb, lc, D_BRANCH), lambda c: (0, c, 0)),
            pl.BlockSpec((D_BRANCH, 2 * N_STATE), const),
            pl.BlockSpec((2 * N_STATE, D_BRANCH), const),
            pl.BlockSpec((2, N_STATE), const),
            pl.BlockSpec((1, D_BRANCH), const),
            pl.BlockSpec((D_BRANCH, D_BRANCH), const),
        ],
        out_specs=pl.BlockSpec((nb, lc, D_BRANCH), lambda c: (0, c, 0)),
        out_shape=jax.ShapeDtypeStruct((nb, L, D_BRANCH), BF16),
        scratch_shapes=[pltpu.VMEM((D_BRANCH // LANES, lc * nb, LANES), F32), pltpu.VMEM((lc * nb, 2 * N_STATE), F32),
                        pltpu.VMEM((nb, 2 * N_STATE), F32)],
        compiler_params=_cparams(("arbitrary",)),
        name="s5",
    )(u, bmat, cmat, a, d, wglu)


def _fox_kernel(q_ref, k_ref, v_ref, o_ref, sa_ref, sb_ref, m_ref, acc_ref, *, tq):
    qi = pl.program_id(2)
    lane = lax.broadcasted_iota(jnp.int32, (tq, LANES), 1)

    def logits(ki, s_ref):
        k0 = pl.multiple_of(ki * tq, tq)
        for h in range(2):
            hs = slice(h * LANES, (h + 1) * LANES)
            s_ref[h] = _dot_nt(q_ref[0, :, hs], k_ref[0, pl.ds(k0, tq), hs])

    def update_rows(ki, s_ref, r0, nr, nc, masked):
        k0 = pl.multiple_of(ki * tq, tq)
        rs = slice(r0, r0 + nr)
        for h in range(2):
            hs = slice(h * LANES, (h + 1) * LANES)
            s = s_ref[h, rs, 0:nc]
            if masked:
                row = lax.broadcasted_iota(jnp.int32, (nr, nc), 0) + r0
                col = lax.broadcasted_iota(jnp.int32, (nr, nc), 1)
                s = jnp.where(col <= row, s, NEG_INF)
            m = m_ref[h, rs]
            m_new = jnp.maximum(m, jnp.max(s, axis=1, keepdims=True))
            p = jnp.exp2(s - _lane_tile(m_new, nc // LANES))
            acc_ref[h, rs] = (jnp.exp2(m - m_new) * acc_ref[h, rs]
                              + _dot(p.astype(BF16), v_ref[0, pl.ds(k0, nc), hs]))
            m_ref[h, rs] = m_new

    def update(ki, s_ref, diag):
        if diag:
            half = tq // 2
            update_rows(ki, s_ref, 0, half, half, True)
            update_rows(ki, s_ref, half, half, tq, True)
        else:
            update_rows(ki, s_ref, 0, tq, tq, False)

    m_ref[...] = jnp.full(m_ref.shape, NEG_INF, F32)
    acc_ref[...] = jnp.zeros_like(acc_ref)
    logits(0, sa_ref)

    def pair(j, _):
        logits(2 * j + 1, sb_ref)
        update(2 * j, sa_ref, False)
        logits(2 * j + 2, sa_ref)
        update(2 * j + 1, sb_ref, False)
        return 0

    lax.fori_loop(0, qi // 2, pair, 0)

    @pl.when(qi % 2 == 0)
    def _():
        update(qi, sa_ref, True)

    @pl.when(qi % 2 == 1)
    def _():
        logits(qi, sb_ref)
        update(qi - 1, sa_ref, False)
        update(qi, sb_ref, True)

    res = [acc_ref[h] / pltpu.roll(acc_ref[h], HEAD_DIM, axis=1) for h in range(2)]
    o_ref[0] = jnp.where(lane < HEAD_DIM, res[0], pltpu.roll(res[1], HEAD_DIM, axis=1)).astype(o_ref.dtype)


def _fox(q, k, v):
    B, L, _ = q.shape
    tq = min(T_ATT, L)
    return pl.pallas_call(
        functools.partial(_fox_kernel, tq=tq),
        grid=(B, 2, L // tq),
        in_specs=[
            pl.BlockSpec((1, tq, 2 * LANES), lambda b, p, i: (b, i, p)),
            pl.BlockSpec((1, L, 2 * LANES), lambda b, p, i: (b, 0, p)),
            pl.BlockSpec((1, L, 2 * LANES), lambda b, p, i: (b, 0, p)),
        ],
        out_specs=pl.BlockSpec((1, tq, LANES), lambda b, p, i: (b, i, p)),
        out_shape=jax.ShapeDtypeStruct((B, L, D_BRANCH), BF16),
        scratch_shapes=[pltpu.VMEM((2, tq, tq), F32), pltpu.VMEM((2, tq, tq), F32),
                        pltpu.VMEM((2, tq, LANES), F32), pltpu.VMEM((2, tq, LANES), F32)],
        compiler_params=_cparams(("parallel", "parallel", "arbitrary")),
        name="fox",
    )(q, k, v)


def _sb_kernel(q_ref, k_ref, v_ref, o_ref, za_ref, zb_ref, run_ref, acc_ref, *, tq, blk):
    qi = pl.program_id(2)
    lane = lax.broadcasted_iota(jnp.int32, (tq, LANES), 1)
    ur = lax.broadcasted_iota(jnp.int32, (blk, blk), 0)
    uc = lax.broadcasted_iota(jnp.int32, (blk, blk), 1)
    upper = jnp.where(ur > uc, 1.0, 0.0).astype(BF16)

    def logits(ki, z_ref):
        k0 = pl.multiple_of(jnp.maximum(ki, 0) * tq, tq)
        for h in range(2):
            hs = slice(h * LANES, (h + 1) * LANES)
            z_ref[h] = _dot_nt(q_ref[0, :, hs], k_ref[0, pl.ds(k0, tq), hs])

    def update_rows(ki, z_ref, r0, nr, nc, masked):
        k0 = pl.multiple_of(ki * tq, tq)
        rs = slice(r0, r0 + nr)
        vt = v_ref[0, pl.ds(k0, nc), :]
        if masked:
            past = (lax.broadcasted_iota(jnp.int32, (nr, nc), 1)
                    < lax.broadcasted_iota(jnp.int32, (nr, nc), 0) + r0)
        for h in range(2):
            run = run_ref[h, rs]
            z = z_ref[h, rs, 0:nc]
            neg_abs = pltpu.bitcast(pltpu.bitcast(z, jnp.uint32) | jnp.uint32(0x80000000), F32)
            l1p = jnp.log(1.0 + jnp.exp2(neg_abs)) * LOG2E
            log_beta = jnp.minimum(z, 0.0) - l1p
            log_keep = log_beta - z
            if masked:
                log_keep = jnp.where(past, log_keep, 0.0)
            keep16 = log_keep.astype(BF16)
            n_c = nc // blk
            after = [None] * n_c
            for c in reversed(range(n_c)):
                cs = slice(c * blk, (c + 1) * blk)
                raw = _dot(keep16[:, cs], upper)
                after[c] = raw + _lane_tile(run, blk // LANES)
                run = run + (raw[:, 0:1] + log_keep[:, c * blk:c * blk + 1])
            a = jnp.exp2(log_beta + _lane_cat(after))
            if masked:
                a = jnp.where(past, a, 0.0)
            acc_ref[h, rs] += _dot(a.astype(BF16), vt)
            run_ref[h, rs] = run

    def update(ki, z_ref, diag):
        if diag:
            half = tq // 2
            update_rows(ki, z_ref, 0, half, half, True)
            update_rows(ki, z_ref, half, half, tq, True)
        else:
            update_rows(ki, z_ref, 0, tq, tq, False)

    run_ref[...] = jnp.zeros_like(run_ref)
    acc_ref[...] = jnp.zeros_like(acc_ref)
    logits(qi, za_ref)
    logits(qi - 1, zb_ref)
    update(qi, za_ref, True)

    def pair(j, _):
        t = qi - 1 - 2 * j
        logits(t - 1, za_ref)
        update(t, zb_ref, False)
        logits(t - 2, zb_ref)
        update(t - 1, za_ref, False)
        return 0

    lax.fori_loop(0, qi // 2, pair, 0)

    @pl.when(qi % 2 == 1)
    def _():
        update(0, zb_ref, False)

    o_ref[0] = jnp.where(lane < HEAD_DIM, acc_ref[0], acc_ref[1]).astype(o_ref.dtype)


def _sb(q, k, v):
    B, L, _ = q.shape
    tq = min(T_ATT, L)
    return pl.pallas_call(
        functools.partial(_sb_kernel, tq=tq, blk=min(SB_BLOCK, tq)),
        grid=(B, 2, L // tq),
        in_specs=[
            pl.BlockSpec((1, tq, 2 * LANES), lambda b, p, i: (b, i, p)),
            pl.BlockSpec((1, L, 2 * LANES), lambda b, p, i: (b, 0, p)),
            pl.BlockSpec((1, L, LANES), lambda b, p, i: (b, 0, p)),
        ],
        out_specs=pl.BlockSpec((1, tq, LANES), lambda b, p, i: (b, i, p)),
        out_shape=jax.ShapeDtypeStruct((B, L, D_BRANCH), BF16),
        scratch_shapes=[pltpu.VMEM((2, tq, tq), F32), pltpu.VMEM((2, tq, tq), F32),
                        pltpu.VMEM((2, tq, LANES), F32), pltpu.VMEM((2, tq, LANES), F32)],
        compiler_params=_cparams(("parallel", "parallel", "arbitrary")),
        name="stickbreak",
    )(q, k, v)


def _conv_kernel(ab_ref, w_ref, b_ref, g_ref, beta_ref, o_ref, pad_ref, sh_ref, *, lc):
    li = pl.program_id(1)

    @pl.when(li == 0)
    def _():
        pad_ref[0:CONV_HALO, :] = jnp.zeros((CONV_HALO, D_BRANCH), F32)

    ab = ab_ref[0]
    pad_ref[CONV_HALO:CONV_HALO + lc, :] = ab[:, 0:D_BRANCH] * _sigmoid(ab[:, D_BRANCH:2 * D_BRANCH])
    span = lc + CONV_HALO - SUBLANES
    for ph in range(1, SUBLANES):
        sh_ref[ph - 1] = pad_ref[ph:ph + span, :]
    off = CONV_HALO - (CONV_WIDTH - 1)
    acc = jnp.zeros((lc, D_BRANCH), F32) + b_ref[...]
    for j in range(CONV_WIDTH):
        ph, base = (off + j) % SUBLANES, (off + j) // SUBLANES * SUBLANES
        tap = pad_ref[base:base + lc, :] if ph == 0 else sh_ref[ph - 1, base:base + lc, :]
        acc = acc + w_ref[j:j + 1, :] * tap
    pad_ref[0:CONV_HALO, :] = pad_ref[lc:lc + CONV_HALO, :]
    mu = jnp.mean(acc, axis=-1, keepdims=True)
    xc = acc - mu
    var = jnp.mean(xc * xc, axis=-1, keepdims=True)
    y = xc * lax.rsqrt(var + EPS) * g_ref[...] + beta_ref[...]
    o_ref[0] = (y * _sigmoid(y)).astype(o_ref.dtype)


def _conv(ab, w, b, g, beta):
    B, L, _ = ab.shape
    lc = min(LC_CONV, L)
    const = lambda b_, l: (0, 0)
    return pl.pallas_call(
        functools.partial(_conv_kernel, lc=lc),
        grid=(B, L // lc),
        in_specs=[
            pl.BlockSpec((1, lc, 2 * D_BRANCH), lambda b_, l: (b_, l, 0)),
            pl.BlockSpec((CONV_WIDTH, D_BRANCH), const),
            pl.BlockSpec((1, D_BRANCH), const),
            pl.BlockSpec((1, D_BRANCH), const),
            pl.BlockSpec((1, D_BRANCH), const),
        ],
        out_specs=pl.BlockSpec((1, lc, D_BRANCH), lambda b_, l: (b_, l, 0)),
        out_shape=jax.ShapeDtypeStruct((B, L, D_BRANCH), BF16),
        scratch_shapes=[pltpu.VMEM((CONV_HALO + lc, D_BRANCH), F32),
                        pltpu.VMEM((SUBLANES - 1, CONV_HALO + lc - SUBLANES, D_BRANCH), F32)],
        compiler_params=_cparams(("parallel", "arbitrary")),
        name="conv",
    )(ab, w, b, g, beta)


def _merge_kernel(x_ref, g_ref, ys5_ref, yfox_ref, yconv_ref, ysb_ref, wg_ref, wb_ref, wo_ref, o_ref):
    x = x_ref[0]
    D = x.shape[-1]
    h = _rmsnorm(x, g_ref[...]).astype(BF16)
    ys = (ys5_ref[0], yfox_ref[0], yconv_ref[0], ysb_ref[0])
    merged = None
    for n in range(4):
        gate = _sigmoid(_dot(h, wg_ref[:, n * D:(n + 1) * D]))
        term = gate * _dot(ys[n], wb_ref[n])
        merged = term if merged is None else merged + term
    o_ref[0] = x + _dot(merged.astype(BF16), wo_ref[...])


def _merge(x, g, ys5, yfox, yconv, ysb, wg, wb, wo):
    B, L, D = x.shape
    tm = min(TM_MERGE, L)
    c2 = lambda b, l: (0, 0)
    yspec = pl.BlockSpec((1, tm, D_BRANCH), lambda b, l: (b, l, 0))
    return pl.pallas_call(
        _merge_kernel,
        grid=(B, L // tm),
        in_specs=[
            pl.BlockSpec((1, tm, D), lambda b, l: (b, l, 0)),
            pl.BlockSpec((1, D), c2),
            yspec, yspec, yspec, yspec,
            pl.BlockSpec((D, 4 * D), c2),
            pl.BlockSpec((4, D_BRANCH, D), lambda b, l: (0, 0, 0)),
            pl.BlockSpec((D, D), c2),
        ],
        out_specs=pl.BlockSpec((1, tm, D), lambda b, l: (b, l, 0)),
        out_shape=jax.ShapeDtypeStruct((B, L, D), F32),
        compiler_params=_cparams(("parallel", "parallel")),
        name="merge",
    )(x, g, ys5, yfox, yconv, ysb, wg, wb, wo)


def _swiglu(h, w1_ref, w3_ref, w2_ref, lead, fc):
    acc = None
    for c in range(w1_ref.shape[-1] // fc):
        cs = slice(c * fc, (c + 1) * fc)
        a = _dot(h, w1_ref[lead + (slice(None), cs)])
        b = _dot(h, w3_ref[lead + (slice(None), cs)])
        part = _dot((a * _sigmoid(a) * b).astype(BF16), w2_ref[lead + (cs, slice(None))])
        acc = part if acc is None else acc + part
    return acc


def _ffn_kernel(x_ref, g_ref, w1_ref, w3_ref, w2_ref, o_ref, *, fc):
    x = x_ref[...]
    h = _rmsnorm(x, g_ref[...]).astype(BF16)
    o_ref[...] = x + _swiglu(h, w1_ref, w3_ref, w2_ref, (), fc)


def _ffn(x2, g, w1, w3, w2):
    T, D = x2.shape
    F = w1.shape[1]
    tm = min(TM_FFN, T)
    c2 = lambda i: (0, 0)
    return pl.pallas_call(
        functools.partial(_ffn_kernel, fc=FC_FFN),
        grid=(T // tm,),
        in_specs=[
            pl.BlockSpec((tm, D), lambda i: (i, 0)),
            pl.BlockSpec((1, D), c2),
            pl.BlockSpec((D, F), c2), pl.BlockSpec((D, F), c2), pl.BlockSpec((F, D), c2),
        ],
        out_specs=pl.BlockSpec((tm, D), lambda i: (i, 0)),
        out_shape=jax.ShapeDtypeStruct((T, D), F32),
        compiler_params=pltpu.CompilerParams(dimension_semantics=("parallel",),
                                             vmem_limit_bytes=VMEM_LIMIT_FFN),
        name="dense_ffn",
    )(x2, g, w1, w3, w2)


def _router_kernel(x_ref, g_ref, wr_ref, br_ref, meta_ref, cnt_ref, carry_ref, *, tm):
    i = pl.program_id(0)

    @pl.when(i == 0)
    def _():
        carry_ref[...] = jnp.zeros_like(carry_ref)

    h = _rmsnorm(x_ref[...], g_ref[...])
    logits = jnp.dot(h, wr_ref[...], preferred_element_type=F32, precision=lax.Precision.HIGHEST) + br_ref[...]
    lane = lax.broadcasted_iota(jnp.int32, logits.shape, 1).astype(F32)
    m1 = jnp.max(logits, axis=1, keepdims=True)
    i1 = jnp.min(jnp.where(logits == m1, lane, float(LANES)), axis=1, keepdims=True)
    rest = jnp.where(lane == i1, NEG_INF, logits)
    m2 = jnp.max(rest, axis=1, keepdims=True)
    i2 = jnp.min(jnp.where(rest == m2, lane, float(LANES)), axis=1, keepdims=True)
    e2 = jnp.exp(m2 - m1)
    p1 = 1.0 / (1.0 + e2)
    p2 = e2 * p1
    onehot = jnp.where(lane == i1, 1.0, 0.0) + jnp.where(lane == i2, 1.0, 0.0)
    rr = lax.broadcasted_iota(jnp.int32, (tm, tm), 0)
    cc = lax.broadcasted_iota(jnp.int32, (tm, tm), 1)
    before = jnp.where(cc < rr, 1.0, 0.0).astype(BF16)
    prefix = _dot(before, onehot.astype(BF16)) + carry_ref[...]
    r1 = jnp.sum(jnp.where(lane == i1, prefix, 0.0), axis=1, keepdims=True)
    r2 = jnp.sum(jnp.where(lane == i2, prefix, 0.0), axis=1, keepdims=True)
    meta = jnp.zeros_like(logits)
    for n, col in enumerate((i1, i2, p1, p2, r1, r2)):
        meta = jnp.where(lane == float(n), col, meta)
    meta_ref[...] = meta
    total = carry_ref[...] + jnp.sum(onehot, axis=0, keepdims=True)
    carry_ref[...] = total
    cnt_ref[...] = total


def _router(x2, g, wr, br):
    T, D = x2.shape
    tm = min(TM_FFN, T)
    c2 = lambda i: (0, 0)
    return pl.pallas_call(
        functools.partial(_router_kernel, tm=tm),
        grid=(T // tm,),
        in_specs=[
            pl.BlockSpec((tm, D), lambda i: (i, 0)),
            pl.BlockSpec((1, D), c2),
            pl.BlockSpec((D, LANES), c2),
            pl.BlockSpec((1, LANES), c2),
        ],
        out_specs=[pl.BlockSpec((tm, LANES), lambda i: (i, 0)), pl.BlockSpec((1, LANES), c2)],
        out_shape=[jax.ShapeDtypeStruct((T, LANES), F32), jax.ShapeDtypeStruct((1, LANES), F32)],
        scratch_shapes=[pltpu.VMEM((1, LANES), F32)],
        compiler_params=_cparams(("arbitrary",)),
        name="router",
    )(x2, g, wr, br)


def _dispatch_kernel(dst_ref, ztile_ref, x_ref, xs_hbm, zero_ref, sem_ref, zsem_ref, *, tt, tm):
    i = pl.program_id(0)

    @pl.when(i == 0)
    def _():
        zero_ref[...] = jnp.zeros_like(zero_ref)
        for e in range(2 * N_EXPERTS):
            @pl.when(ztile_ref[e] >= 0)
            def _():
                start = pl.multiple_of(ztile_ref[e], tm)
                fill = pltpu.make_async_copy(zero_ref, xs_hbm.at[pl.ds(start, tm), :], zsem_ref)
                fill.start()
                fill.wait()

    def body(r, _):
        for k in range(2):
            pltpu.make_async_copy(x_ref.at[pl.ds(r, 1), :], xs_hbm.at[pl.ds(dst_ref[2 * (i * tt + r) + k], 1), :],
                                  sem_ref).start(priority=k)
        return 0

    lax.fori_loop(0, tt, body, 0, unroll=4)
    for _ in range(2):
        pltpu.make_async_copy(x_ref, xs_hbm.at[pl.ds(0, tt), :], sem_ref).wait()


def _dispatch(x2, dst, ztile, R):
    T, D = x2.shape
    tt = min(TT_MOE, T)
    grid_spec = pltpu.PrefetchScalarGridSpec(
        num_scalar_prefetch=2,
        grid=(T // tt,),
        in_specs=[pl.BlockSpec((tt, D), lambda i, d, z: (i, 0))],
        out_specs=pl.BlockSpec(memory_space=pl.ANY),
        scratch_shapes=[pltpu.VMEM((TM_MOE, D), F32), pltpu.SemaphoreType.DMA(()), pltpu.SemaphoreType.DMA(())],
    )
    return pl.pallas_call(
        functools.partial(_dispatch_kernel, tt=tt, tm=TM_MOE),
        grid_spec=grid_spec,
        out_shape=jax.ShapeDtypeStruct((R, D), F32),
        compiler_params=_cparams(("arbitrary",)),
        name="dispatch",
    )(dst, ztile, x2)


def _experts_kernel(texp_ref, nused_ref, xs_ref, g_ref, w1_ref, w3_ref, w2_ref, y_ref, *, fc):
    i = pl.program_id(0)

    @pl.when(i < nused_ref[0])
    def _():
        h = _rmsnorm(xs_ref[...], g_ref[...]).astype(BF16)
        y_ref[...] = _swiglu(h, w1_ref, w3_ref, w2_ref, (0,), fc)

    @pl.when(i >= nused_ref[0])
    def _():
        y_ref[...] = jnp.zeros_like(y_ref)


def _experts(xs, g, tile_expert, n_used, w1, w3, w2):
    R, D = xs.shape
    F = w1.shape[2]
    tm = TM_MOE
    wspec = lambda shape: pl.BlockSpec(shape, lambda i, te, n: (te[i], 0, 0))
    grid_spec = pltpu.PrefetchScalarGridSpec(
        num_scalar_prefetch=2,
        grid=(R // tm,),
        in_specs=[
            pl.BlockSpec((tm, D), lambda i, te, n: (jnp.minimum(i, n[0] - 1), 0)),
            pl.BlockSpec((1, D), lambda i, te, n: (0, 0)),
            wspec((1, D, F)), wspec((1, D, F)), wspec((1, F, D)),
        ],
        out_specs=pl.BlockSpec((tm, D), lambda i, te, n: (i, 0)),
    )
    return pl.pallas_call(
        functools.partial(_experts_kernel, fc=FC_FFN),
        grid_spec=grid_spec,
        out_shape=jax.ShapeDtypeStruct((R, D), F32),
        compiler_params=pltpu.CompilerParams(dimension_semantics=("arbitrary",),
                                             vmem_limit_bytes=VMEM_LIMIT_FFN),
        name="experts",
    )(tile_expert, n_used, xs, g, w1, w3, w2)


def _combine_kernel(dst_ref, x_ref, p_ref, y_hbm, o_ref, buf_ref, sem_ref, *, tc, n_tiles):
    i = pl.program_id(0)
    slot = lax.rem(i, 2)

    def issue(tile, s):
        base = tile * (2 * tc)

        def body(r, _):
            for k in range(2):
                pltpu.make_async_copy(y_hbm.at[pl.ds(dst_ref[base + 2 * r + k], 1), :],
                                      buf_ref.at[s, k, pl.ds(r, 1), :], sem_ref.at[s]).start(priority=k)
            return 0

        lax.fori_loop(0, tc, body, 0, unroll=4)

    @pl.when(i == 0)
    def _():
        issue(0, 0)

    @pl.when(i + 1 < n_tiles)
    def _():
        issue(i + 1, 1 - slot)

    for k in range(2):
        pltpu.make_async_copy(y_hbm.at[pl.ds(0, tc), :], buf_ref.at[slot, k], sem_ref.at[slot]).wait()
    o_ref[...] = x_ref[...] + p_ref[:, 0:1] * buf_ref[slot, 0] + p_ref[:, 1:2] * buf_ref[slot, 1]


def _combine(x2, prob, y, dst):
    T, D = x2.shape
    tc = min(TC_MOE, T)
    n_tiles = T // tc
    grid_spec = pltpu.PrefetchScalarGridSpec(
        num_scalar_prefetch=1,
        grid=(n_tiles,),
        in_specs=[pl.BlockSpec((tc, D), lambda i, d: (i, 0)), pl.BlockSpec((tc, 2), lambda i, d: (i, 0)),
                  pl.BlockSpec(memory_space=pl.ANY)],
        out_specs=pl.BlockSpec((tc, D), lambda i, d: (i, 0)),
        scratch_shapes=[pltpu.VMEM((2, 2, tc, D), F32), pltpu.SemaphoreType.DMA((2,))],
    )
    return pl.pallas_call(
        functools.partial(_combine_kernel, tc=tc, n_tiles=n_tiles),
        grid_spec=grid_spec,
        out_shape=jax.ShapeDtypeStruct((T, D), F32),
        compiler_params=_cparams(("arbitrary",)),
        name="combine",
    )(dst, x2, prob, y)


def _moe(x2, g, wr, br, w1, w3, w2):
    T, D = x2.shape
    tm = TM_MOE
    meta, cnt = _router(x2, g, wr, br)
    expert = meta[:, 0:2].astype(jnp.int32)
    prob = meta[:, 2:4]
    rank = meta[:, 4:6].astype(jnp.int32)
    counts = cnt[0, :N_EXPERTS].astype(jnp.int32)
    padded = (counts + (tm - 1)) // tm * tm
    ends = jnp.cumsum(padded)
    dst = ((ends - padded)[expert] + rank).reshape(-1)
    R = 2 * T + N_EXPERTS * tm
    tile_start = jnp.arange(R // tm, dtype=jnp.int32) * tm
    tile_expert = jnp.minimum(jnp.searchsorted(ends, tile_start, side="right"), N_EXPERTS - 1).astype(jnp.int32)
    n_used = (ends[-1] // tm).astype(jnp.int32).reshape(1)
    last_tile = jnp.where(padded > 0, ends - tm, -1)
    tail = ends[-1] + jnp.arange(N_EXPERTS, dtype=jnp.int32) * tm
    ztile = jnp.concatenate([last_tile, jnp.where(tail < R, tail, -1)]).astype(jnp.int32)
    xs = _dispatch(x2, dst, ztile, R)
    y = _experts(xs, g, tile_expert, n_used, w1, w3, w2)
    return _combine(x2, prob, y, dst)


def _s5_params(lam_re, lam_im, log_dt, b_re, b_im, c_re, c_im):
    G, P, H = S5_GROUPS, S5_STATE, S5_GROUP
    dt = jnp.exp(log_dt)[:, None]
    mag = jnp.exp(lam_re * dt)
    ab_re = mag * jnp.cos(lam_im * dt)
    ab_im = mag * jnp.sin(lam_im * dt)
    nr, ni = ab_re - 1.0, ab_im
    den = lam_re * lam_re + lam_im * lam_im
    k_re = (nr * lam_re + ni * lam_im) / den
    k_im = (ni * lam_re - nr * lam_im) / den
    bb_re = k_re[..., None] * b_re - k_im[..., None] * b_im
    bb_im = k_re[..., None] * b_im + k_im[..., None] * b_re
    eye = jnp.eye(G, dtype=F32)
    bm_re = jnp.einsum("gph,gk->ghkp", bb_re, eye).reshape(G * H, G * P)
    bm_im = jnp.einsum("gph,gk->ghkp", bb_im, eye).reshape(G * H, G * P)
    bmat = jnp.concatenate([bm_re, bm_im], axis=1).astype(BF16)
    cm_re = jnp.einsum("ghp,gk->kpgh", c_re, eye).reshape(G * P, G * H)
    cm_im = jnp.einsum("ghp,gk->kpgh", c_im, eye).reshape(G * P, G * H)
    cmat = jnp.concatenate([cm_re, -cm_im], axis=0).astype(BF16)
    a = jnp.stack([ab_re.reshape(-1), ab_im.reshape(-1)], axis=0)
    return bmat, cmat, a


def kernel(x, g_mix, w_in, b_forget, fox_q_gain, fox_k_gain, sb_q_gain, sb_k_gain, s5_lam_re, s5_lam_im, s5_log_dt, s5_b_re, s5_b_im, s5_c_re, s5_c_im, s5_d, s5_w_glu, conv_w, conv_b, conv_ln_g, conv_ln_b, w_branch, w_out, g_ffn, ffn_w1, ffn_w3, ffn_w2, router_w, router_b, moe_w1, moe_w3, moe_w2):
    B, L, D = x.shape
    depth = g_mix.shape[0]
    assert B == SUBLANES, "the S5 scan lays the batch along the sublanes"
    n_qkv = 3 * D_BRANCH
    o_f = D_BRANCH + n_qkv
    o_c = o_f + N_HEADS
    o_g = o_c + 2 * D_BRANCH + n_qkv
    scale = 1.0 / math.sqrt(HEAD_DIM)
    sel_np, const_np = _forget_feature_tables()
    sel = jnp.asarray(sel_np, BF16)
    const = jnp.asarray(const_np, F32)

    w_a = w_in[:, :, :o_f].astype(BF16)
    w_b = w_in[:, :, o_c:o_g].astype(BF16)
    w_f = jnp.pad(w_in[:, :, o_f:o_c], ((0, 0), (0, 0), (0, LANES - N_HEADS))).astype(BF16)
    w_g = w_in[:, :, o_g:].astype(BF16)

    for i in range(depth):
        bfg = jnp.pad(b_forget[i], (0, LANES - N_HEADS)).reshape(1, LANES)
        gains = jnp.stack([jnp.tile(fox_q_gain[i] * (scale * LOG2E), N_HEADS), jnp.tile(fox_k_gain[i], N_HEADS),
                           jnp.tile(sb_q_gain[i] * (scale * LOG2E), N_HEADS), jnp.tile(sb_k_gain[i], N_HEADS)],
                          axis=0)
        us5, qf, kf, vf, conv_ab, qs, ks, vs = _inproj(x, g_mix[i].reshape(1, D), w_a[i], w_b[i], w_f[i], bfg,
                                                       gains, sel, const)

        bmat, cmat, a = _s5_params(s5_lam_re[i], s5_lam_im[i], s5_log_dt[i], s5_b_re[i], s5_b_im[i],
                                   s5_c_re[i], s5_c_im[i])
        ys5 = _s5(us5, bmat, cmat, a, s5_d[i].reshape(1, D_BRANCH), s5_w_glu[i].astype(BF16))
        yfox = _fox(qf, kf, vf)
        yconv = _conv(conv_ab, conv_w[i], conv_b[i].reshape(1, -1), conv_ln_g[i].reshape(1, -1),
                      conv_ln_b[i].reshape(1, -1))
        ysb = _sb(qs, ks, vs)

        x = _merge(x, g_mix[i].reshape(1, D), ys5, yfox, yconv, ysb, w_g[i],
                   w_branch[i].astype(BF16), w_out[i].astype(BF16))

        j = i // 2
        gf = g_ffn[i].reshape(1, D)
        if i % 2 == 0:
            x2 = _ffn(x.reshape(B * L, D), gf, ffn_w1[j].astype(BF16), ffn_w3[j].astype(BF16),
                      ffn_w2[j].astype(BF16))
        else:
            wr = jnp.pad(router_w[j], ((0, 0), (0, LANES - N_EXPERTS)))
            br = jnp.pad(router_b[j], (0, LANES - N_EXPERTS), constant_values=-1e30).reshape(1, LANES)
            x2 = _moe(x.reshape(B * L, D), gf, wr, br, moe_w1[j].astype(BF16), moe_w3[j].astype(BF16),
                      moe_w2[j].astype(BF16))
        x = x2.reshape(B, L, D)
    return x
```

```python
import functools
import math

import numpy as np

import jax
import jax.numpy as jnp
from jax import lax
from jax.experimental import pallas as pl
from jax.experimental.pallas import tpu as pltpu

F32 = jnp.float32
BF16 = jnp.bfloat16
EPS = 1e-6
LOG2E = math.log2(math.e)

D_BRANCH = 256
HEAD_DIM = 64
N_HEADS = D_BRANCH // HEAD_DIM
S5_GROUP = 16
S5_GROUPS = D_BRANCH // S5_GROUP
S5_STATE = 64
N_STATE = S5_GROUPS * S5_STATE
CONV_WIDTH = 31
CONV_HALO = 32
N_EXPERTS = 8
LANES = 128
SUBLANES = 8
VMEM_LIMIT = 56 * 1024 * 1024
D_SPREAD = N_HEADS * LANES
N_SPLIT = 3

TM_PROJ = 512
TM_MERGE = 512
TM_FFN = 512
T_ATT = 512
SB_BLOCK = 256
SB_RUN_FLOOR = -160.0
LC_S5 = 128
LC_CONV = 512
TM_MOE = 512
FC_FFN = 256
TT_MOE = 1024
TC_MOE = 256
VMEM_LIMIT_FFN = 60 * 1024 * 1024

NEG_INF = float("-inf")


def _cparams(sem):
    return pltpu.CompilerParams(dimension_semantics=sem, vmem_limit_bytes=VMEM_LIMIT)


def _dot(a, b):
    return jnp.dot(a, b, preferred_element_type=F32)


def _dot_nt(a, b):
    return lax.dot_general(a, b, (((1,), (1,)), ((), ())), preferred_element_type=F32)


def _split2(x):
    hi = x.astype(BF16)
    lo = (x - hi.astype(F32)).astype(BF16)
    return hi, lo


def _rmsnorm(x, g):
    ms = jnp.mean(x * x, axis=-1, keepdims=True)
    return x * lax.rsqrt(ms + EPS) * g


def _sigmoid(x):
    return 1.0 / (1.0 + jnp.exp(-x))


def _log_sigmoid(x):
    return jnp.minimum(x, 0.0) - jnp.log(1.0 + jnp.exp(-jnp.abs(x)))


def _lane_cat(parts):
    return parts[0] if len(parts) == 1 else jnp.concatenate(parts, axis=1)


def _lane_tile(x, n):
    return _lane_cat([x] * n)


def _forget_feature_tables():
    sel = np.zeros((LANES, 2 * D_SPREAD), np.float32)
    const = np.zeros((1, 2 * D_SPREAD), np.float32)
    for h in range(N_HEADS):
        for j in range(N_SPLIT):
            sel[j * N_HEADS + h, h * LANES + HEAD_DIM + j] = 1.0
            sel[j * N_HEADS + h, D_SPREAD + h * LANES + HEAD_DIM + N_SPLIT + j] = -1.0
            const[0, h * LANES + HEAD_DIM + N_SPLIT + j] = 1.0
            const[0, D_SPREAD + h * LANES + HEAD_DIM + j] = 1.0
    return sel, const


def _inproj_kernel(x_ref, g_ref, wa_ref, wb_ref, wf_ref, bf_ref, gain_ref, sel_ref, const_ref,
                   us5_ref, qf_ref, kf_ref, vf_ref, conv_ref, qs_ref, ks_ref, vs_ref, carry_ref, *, tm):
    li = pl.program_id(1)
    x = x_ref[0]
    h = _rmsnorm(x, g_ref[...]).astype(BF16)
    lane = lax.broadcasted_iota(jnp.int32, (tm, LANES), 1)

    r = lax.broadcasted_iota(jnp.int32, (D_BRANCH, D_BRANCH), 0) // HEAD_DIM
    c = lax.broadcasted_iota(jnp.int32, (D_BRANCH, D_BRANCH), 1) // HEAD_DIM
    ones_bd = jnp.where(r == c, 1.0, 0.0).astype(BF16)

    def qknorm(t, gi):
        ss = _dot((t * t).astype(BF16), ones_bd)
        return t * lax.rsqrt(ss * (1.0 / HEAD_DIM) + EPS) * gain_ref[gi:gi + 1, :]

    def spread(t, fill):
        blocks = []
        for hp in range(2):
            pair = t[:, hp * LANES:(hp + 1) * LANES]
            blocks += [pair, pltpu.roll(pair, HEAD_DIM, axis=1)]
        out = [jnp.where(lane < HEAD_DIM, blocks[n], fill(n)) for n in range(N_HEADS)]
        return jnp.concatenate(out, axis=1).astype(BF16)

    @pl.when(li == 0)
    def _():
        carry_ref[...] = jnp.zeros_like(carry_ref)

    lf = _log_sigmoid(_dot(h, wf_ref[...]) + bf_ref[...]) * LOG2E
    rr = lax.broadcasted_iota(jnp.int32, (tm, tm), 0)
    cc = lax.broadcasted_iota(jnp.int32, (tm, tm), 1)
    tri = jnp.where(cc <= rr, 1.0, 0.0).astype(BF16)

    def split3(v):
        hi = v.astype(BF16).astype(F32)
        r1 = v - hi
        mid = r1.astype(BF16).astype(F32)
        lo = (r1 - mid).astype(BF16).astype(F32)
        return hi, mid, lo

    hi, mid, lo = split3(lf)
    cum = _dot(tri, hi.astype(BF16)) + _dot(tri, mid.astype(BF16)) + _dot(tri, lo.astype(BF16)) + carry_ref[...]
    carry_ref[...] = cum[tm - 1:tm, :]
    hi, mid, lo = split3(jnp.where(lane < N_HEADS, cum, 0.0))
    packed = hi + pltpu.roll(mid, N_HEADS, axis=1) + pltpu.roll(lo, 2 * N_HEADS, axis=1)
    feat = _dot(packed.astype(BF16), sel_ref[...]) + const_ref[...]

    def feat_q(n):
        return feat[:, n * LANES:(n + 1) * LANES]

    def feat_k(n):
        return feat[:, D_SPREAD + n * LANES:D_SPREAD + (n + 1) * LANES]

    zero = lambda n: 0.0
    one = lambda n: 1.0

    pa = _dot(h, wa_ref[...])
    us5_ref[0] = pa[:, 0:256]
    qf_ref[0] = spread(qknorm(pa[:, 256:512], 0), feat_q)
    kf_ref[0] = spread(qknorm(pa[:, 512:768], 1), feat_k)
    vf_ref[0] = spread(pa[:, 768:1024], one)
    pb = _dot(h, wb_ref[...])
    conv_ref[0] = pb[:, 0:512]
    qs_ref[0] = spread(qknorm(pb[:, 512:768], 2), zero)
    ks_ref[0] = spread(qknorm(pb[:, 768:1024], 3), zero)
    vs_ref[0] = pb[:, 1024:1280].astype(BF16)


def _inproj(x, g, wa, wb, wf, bfg, gains, sel, const):
    B, L, D = x.shape
    tm = min(TM_PROJ, L)
    c2 = lambda b, l: (0, 0)
    row_spec = lambda n: pl.BlockSpec((1, tm, n), lambda b, l: (b, l, 0))
    return pl.pallas_call(
        functools.partial(_inproj_kernel, tm=tm),
        grid=(B, L // tm),
        in_specs=[
            row_spec(D),
            pl.BlockSpec((1, D), c2),
            pl.BlockSpec(wa.shape, c2),
            pl.BlockSpec(wb.shape, c2),
            pl.BlockSpec((D, LANES), c2),
            pl.BlockSpec((1, LANES), c2),
            pl.BlockSpec((4, D_BRANCH), c2),
            pl.BlockSpec(sel.shape, c2),
            pl.BlockSpec(const.shape, c2),
        ],
        out_specs=[
            row_spec(D_BRANCH),
            row_spec(D_SPREAD), row_spec(D_SPREAD), row_spec(D_SPREAD),
            row_spec(2 * D_BRANCH),
            row_spec(D_SPREAD), row_spec(D_SPREAD), row_spec(D_BRANCH),
        ],
        out_shape=[
            jax.ShapeDtypeStruct((B, L, D_BRANCH), F32),
            jax.ShapeDtypeStruct((B, L, D_SPREAD), BF16),
            jax.ShapeDtypeStruct((B, L, D_SPREAD), BF16),
            jax.ShapeDtypeStruct((B, L, D_SPREAD), BF16),
            jax.ShapeDtypeStruct((B, L, 2 * D_BRANCH), F32),
            jax.ShapeDtypeStruct((B, L, D_SPREAD), BF16),
            jax.ShapeDtypeStruct((B, L, D_SPREAD), BF16),
            jax.ShapeDtypeStruct((B, L, D_BRANCH), BF16),
        ],
        scratch_shapes=[pltpu.VMEM((1, LANES), F32)],
        compiler_params=_cparams(("parallel", "arbitrary")),
        name="inproj",
    )(x, g, wa, wb, wf, bfg, gains, sel, const)


def _s5_kernel(u_ref, bmat_ref, cmat_ref, a_ref, d_ref, wglu_ref, y_ref, utb_ref, bu_ref, st_ref, *, lc, nb):
    ci = pl.program_id(0)

    @pl.when(ci == 0)
    def _():
        st_ref[...] = jnp.zeros_like(st_ref)

    n_half = D_BRANCH // LANES
    for b in range(nb):
        for j in range(n_half):
            utb_ref.at[j][pl.ds(b, lc, stride=nb), :] = u_ref[b, :, j * LANES:(j + 1) * LANES]
    u = jnp.concatenate([utb_ref[j] for j in range(n_half)], axis=1)
    bu_ref[...] = _dot(u.astype(BF16), bmat_ref[...])

    a_re = jnp.broadcast_to(a_ref[0:1, :], (nb, N_STATE))
    a_im = jnp.broadcast_to(a_ref[1:2, :], (nb, N_STATE))

    def step(t, carry):
        s_re, s_im = carry
        r0 = pl.multiple_of(t * nb, nb)
        b_re = bu_ref[pl.ds(r0, nb), 0:N_STATE]
        b_im = bu_ref[pl.ds(r0, nb), N_STATE:2 * N_STATE]
        n_re = a_re * s_re - a_im * s_im + b_re
        n_im = a_re * s_im + a_im * s_re + b_im
        bu_ref[pl.ds(r0, nb), 0:N_STATE] = n_re
        bu_ref[pl.ds(r0, nb), N_STATE:2 * N_STATE] = n_im
        return n_re, n_im

    s_re, s_im = lax.fori_loop(0, lc, step, (st_ref[:, 0:N_STATE], st_ref[:, N_STATE:2 * N_STATE]),
                               unroll=2)
    st_ref[:, 0:N_STATE] = s_re
    st_ref[:, N_STATE:2 * N_STATE] = s_im

    y = _dot(bu_ref[...].astype(BF16), cmat_ref[...]) + d_ref[...] * u
    y = jax.nn.gelu(y, approximate=True)
    y = y * _sigmoid(_dot(y.astype(BF16), wglu_ref[...]))
    for j in range(n_half):
        utb_ref[j] = y[:, j * LANES:(j + 1) * LANES]
    for b in range(nb):
        for j in range(n_half):
            y_ref[b, :, j * LANES:(j + 1) * LANES] = utb_ref.at[j][pl.ds(b, lc, stride=nb), :].astype(y_ref.dtype)


def _s5(u, bmat, cmat, a, d, wglu):
    nb, L, _ = u.shape
    lc = min(LC_S5, L)
    const = lambda c: (0, 0)
    return pl.pallas_call(
        functools.partial(_s5_kernel, lc=lc, nb=nb),
        grid=(L // lc,),
        in_specs=[
            pl.BlockSpec((nb, lc, D_BRANCH), lambda c: (0, c, 0)),
            pl.BlockSpec((D_BRANCH, 2 * N_STATE), const),
            pl.BlockSpec((2 * N_STATE, D_BRANCH), const),
            pl.BlockSpec((2, N_STATE), const),
            pl.BlockSpec((1, D_BRANCH), const),
            pl.BlockSpec((D_BRANCH, D_BRANCH), const),
        ],
        out_specs=pl.BlockSpec((nb, lc, D_BRANCH), lambda c: (0, c, 0)),
        out_shape=jax.ShapeDtypeStruct((nb, L, D_BRANCH), BF16),
        scratch_shapes=[pltpu.VMEM((D_BRANCH // LANES, lc * nb, LANES), F32), pltpu.VMEM((lc * nb, 2 * N_STATE), F32),
                        pltpu.VMEM((nb, 2 * N_STATE), F32)],
        compiler_params=_cparams(("arbitrary",)),
        name="s5",
    )(u, bmat, cmat, a, d, wglu)


def _fox_kernel(q_ref, k_ref, v_ref, o_ref, sa_ref, sb_ref, m_ref, acc_ref, *, tq):
    qi = pl.program_id(2)
    lane = lax.broadcasted_iota(jnp.int32, (tq, LANES), 1)

    def logits(ki, s_ref):
        k0 = pl.multiple_of(ki * tq, tq)
        for h in range(2):
            hs = slice(h * LANES, (h + 1) * LANES)
            s_ref[h] = _dot_nt(q_ref[0, :, hs], k_ref[0, pl.ds(k0, tq), hs])

    def update_rows(ki, s_ref, r0, nr, nc, masked):
        k0 = pl.multiple_of(ki * tq, tq)
        rs = slice(r0, r0 + nr)
        for h in range(2):
            hs = slice(h * LANES, (h + 1) * LANES)
            s = s_ref[h, rs, 0:nc]
            if masked:
                row = lax.broadcasted_iota(jnp.int32, (nr, nc), 0) + r0
                col = lax.broadcasted_iota(jnp.int32, (nr, nc), 1)
                s = jnp.where(col <= row, s, NEG_INF)
            m = m_ref[h, rs]
            m_new = jnp.maximum(m, jnp.max(s, axis=1, keepdims=True))
            p = jnp.exp2(s - _lane_tile(m_new, nc // LANES))
            acc_ref[h, rs] = (jnp.exp2(m - m_new) * acc_ref[h, rs]
                              + _dot(p.astype(BF16), v_ref[0, pl.ds(k0, nc), hs]))
            m_ref[h, rs] = m_new

    def update(ki, s_ref, diag):
        if diag:
            half = tq // 2
            update_rows(ki, s_ref, 0, half, half, True)
            update_rows(ki, s_ref, half, half, tq, True)
        else:
            update_rows(ki, s_ref, 0, tq, tq, False)

    m_ref[...] = jnp.full(m_ref.shape, NEG_INF, F32)
    acc_ref[...] = jnp.zeros_like(acc_ref)
    logits(0, sa_ref)

    def pair(j, _):
        logits(2 * j + 1, sb_ref)
        update(2 * j, sa_ref, False)
        logits(2 * j + 2, sa_ref)
        update(2 * j + 1, sb_ref, False)
        return 0

    lax.fori_loop(0, qi // 2, pair, 0)

    @pl.when(qi % 2 == 0)
    def _():
        update(qi, sa_ref, True)

    @pl.when(qi % 2 == 1)
    def _():
        logits(qi, sb_ref)
        update(qi - 1, sa_ref, False)
        update(qi, sb_ref, True)

    res = [acc_ref[h] / pltpu.roll(acc_ref[h], HEAD_DIM, axis=1) for h in range(2)]
    o_ref[0] = jnp.where(lane < HEAD_DIM, res[0], pltpu.roll(res[1], HEAD_DIM, axis=1)).astype(o_ref.dtype)


def _fox(q, k, v):
    B, L, _ = q.shape
    tq = min(T_ATT, L)
    return pl.pallas_call(
        functools.partial(_fox_kernel, tq=tq),
        grid=(B, 2, L // tq),
        in_specs=[
            pl.BlockSpec((1, tq, 2 * LANES), lambda b, p, i: (b, i, p)),
            pl.BlockSpec((1, L, 2 * LANES), lambda b, p, i: (b, 0, p)),
            pl.BlockSpec((1, L, 2 * LANES), lambda b, p, i: (b, 0, p)),
        ],
        out_specs=pl.BlockSpec((1, tq, LANES), lambda b, p, i: (b, i, p)),
        out_shape=jax.ShapeDtypeStruct((B, L, D_BRANCH), BF16),
        scratch_shapes=[pltpu.VMEM((2, tq, tq), F32), pltpu.VMEM((2, tq, tq), F32),
                        pltpu.VMEM((2, tq, LANES), F32), pltpu.VMEM((2, tq, LANES), F32)],
        compiler_params=_cparams(("parallel", "parallel", "arbitrary")),
        name="fox",
    )(q, k, v)


def _sb_kernel(q_ref, k_ref, v_ref, o_ref, za_ref, zb_ref, run_ref, acc_ref, *, tq, blk):
    qi = pl.program_id(2)
    lane = lax.broadcasted_iota(jnp.int32, (tq, LANES), 1)
    ur = lax.broadcasted_iota(jnp.int32, (blk, blk), 0)
    uc = lax.broadcasted_iota(jnp.int32, (blk, blk), 1)
    upper = jnp.where(ur > uc, 1.0, 0.0).astype(BF16)

    def logits(ki, z_ref):
        k0 = pl.multiple_of(jnp.maximum(ki, 0) * tq, tq)
        for h in range(2):
            hs = slice(h * LANES, (h + 1) * LANES)
            z_ref[h] = _dot_nt(q_ref[0, :, hs], k_ref[0, pl.ds(k0, tq), hs])

    def update_rows(ki, z_ref, r0, nr, nc, masked):
        k0 = pl.multiple_of(ki * tq, tq)
        rs = slice(r0, r0 + nr)
        vt = v_ref[0, pl.ds(k0, nc), :]
        if masked:
            past = (lax.broadcasted_iota(jnp.int32, (nr, nc), 1)
                    < lax.broadcasted_iota(jnp.int32, (nr, nc), 0) + r0)
        for h in range(2):
            run = run_ref[h, rs]
            z = z_ref[h, rs, 0:nc]
            neg_abs = pltpu.bitcast(pltpu.bitcast(z, jnp.uint32) | jnp.uint32(0x80000000), F32)
            l1p = jnp.log(1.0 + jnp.exp2(neg_abs)) * LOG2E
            log_beta = jnp.minimum(z, 0.0) - l1p
            log_keep = log_beta - z
            if masked:
                log_keep = jnp.where(past, log_keep, 0.0)
            keep16 = log_keep.astype(BF16)
            n_c = nc // blk
            after = [None] * n_c
            for c in reversed(range(n_c)):
                cs = slice(c * blk, (c + 1) * blk)
                raw = _dot(keep16[:, cs], upper)
                after[c] = raw + _lane_tile(run, blk // LANES)
                run = run + (raw[:, 0:1] + log_keep[:, c * blk:c * blk + 1])
            a = jnp.exp2(log_beta + _lane_cat(after))
            if masked:
                a = jnp.where(past, a, 0.0)
            acc_ref[h, rs] += _dot(a.astype(BF16), vt)
            run_ref[h, rs] = run

    def update(ki, z_ref, diag):
        if diag:
            half = tq // 2
            update_rows(ki, z_ref, 0, half, half, True)
            update_rows(ki, z_ref, half, half, tq, True)
        else:
            update_rows(ki, z_ref, 0, tq, tq, False)

    run_ref[...] = jnp.zeros_like(run_ref)
    acc_ref[...] = jnp.zeros_like(acc_ref)
    logits(qi, za_ref)
    logits(qi - 1, zb_ref)
    update(qi, za_ref, True)

    def alive():
        return jnp.max(run_ref[...]) > SB_RUN_FLOOR

    def pair(carry):
        j, _ = carry
        t = qi - 1 - 2 * j
        logits(t - 1, za_ref)
        update(t, zb_ref, False)

        @pl.when(alive())
        def _():
            logits(t - 2, zb_ref)
            update(t - 1, za_ref, False)

        return j + 1, alive()

    n_pairs = qi // 2
    _, live = lax.while_loop(lambda c: (c[0] < n_pairs) & c[1], pair, (jnp.int32(0), alive()))

    @pl.when((qi % 2 == 1) & live)
    def _():
        update(0, zb_ref, False)

    o_ref[0] = jnp.where(lane < HEAD_DIM, acc_ref[0], acc_ref[1]).astype(o_ref.dtype)


def _sb(q, k, v):
    B, L, _ = q.shape
    tq = min(T_ATT, L)
    return pl.pallas_call(
        functools.partial(_sb_kernel, tq=tq, blk=min(SB_BLOCK, tq)),
        grid=(B, 2, L // tq),
        in_specs=[
            pl.BlockSpec((1, tq, 2 * LANES), lambda b, p, i: (b, i, p)),
            pl.BlockSpec((1, L, 2 * LANES), lambda b, p, i: (b, 0, p)),
            pl.BlockSpec((1, L, LANES), lambda b, p, i: (b, 0, p)),
        ],
        out_specs=pl.BlockSpec((1, tq, LANES), lambda b, p, i: (b, i, p)),
        out_shape=jax.ShapeDtypeStruct((B, L, D_BRANCH), BF16),
        scratch_shapes=[pltpu.VMEM((2, tq, tq), F32), pltpu.VMEM((2, tq, tq), F32),
                        pltpu.VMEM((2, tq, LANES), F32), pltpu.VMEM((2, tq, LANES), F32)],
        compiler_params=_cparams(("parallel", "parallel", "arbitrary")),
        name="stickbreak",
    )(q, k, v)


def _conv_kernel(ab_ref, w_ref, b_ref, g_ref, beta_ref, o_ref, pad_ref, sh_ref, *, lc):
    li = pl.program_id(1)

    @pl.when(li == 0)
    def _():
        pad_ref[0:CONV_HALO, :] = jnp.zeros((CONV_HALO, D_BRANCH), F32)

    ab = ab_ref[0]
    pad_ref[CONV_HALO:CONV_HALO + lc, :] = ab[:, 0:D_BRANCH] * _sigmoid(ab[:, D_BRANCH:2 * D_BRANCH])
    span = lc + CONV_HALO - SUBLANES
    for ph in range(1, SUBLANES):
        sh_ref[ph - 1] = pad_ref[ph:ph + span, :]
    off = CONV_HALO - (CONV_WIDTH - 1)
    acc = jnp.zeros((lc, D_BRANCH), F32) + b_ref[...]
    for j in range(CONV_WIDTH):
        ph, base = (off + j) % SUBLANES, (off + j) // SUBLANES * SUBLANES
        tap = pad_ref[base:base + lc, :] if ph == 0 else sh_ref[ph - 1, base:base + lc, :]
        acc = acc + w_ref[j:j + 1, :] * tap
    pad_ref[0:CONV_HALO, :] = pad_ref[lc:lc + CONV_HALO, :]
    mu = jnp.mean(acc, axis=-1, keepdims=True)
    xc = acc - mu
    var = jnp.mean(xc * xc, axis=-1, keepdims=True)
    y = xc * lax.rsqrt(var + EPS) * g_ref[...] + beta_ref[...]
    o_ref[0] = (y * _sigmoid(y)).astype(o_ref.dtype)


def _conv(ab, w, b, g, beta):
    B, L, _ = ab.shape
    lc = min(LC_CONV, L)
    const = lambda b_, l: (0, 0)
    return pl.pallas_call(
        functools.partial(_conv_kernel, lc=lc),
        grid=(B, L // lc),
        in_specs=[
            pl.BlockSpec((1, lc, 2 * D_BRANCH), lambda b_, l: (b_, l, 0)),
            pl.BlockSpec((CONV_WIDTH, D_BRANCH), const),
            pl.BlockSpec((1, D_BRANCH), const),
            pl.BlockSpec((1, D_BRANCH), const),
            pl.BlockSpec((1, D_BRANCH), const),
        ],
        out_specs=pl.BlockSpec((1, lc, D_BRANCH), lambda b_, l: (b_, l, 0)),
        out_shape=jax.ShapeDtypeStruct((B, L, D_BRANCH), BF16),
        scratch_shapes=[pltpu.VMEM((CONV_HALO + lc, D_BRANCH), F32),
                        pltpu.VMEM((SUBLANES - 1, CONV_HALO + lc - SUBLANES, D_BRANCH), F32)],
        compiler_params=_cparams(("parallel", "arbitrary")),
        name="conv",
    )(ab, w, b, g, beta)


def _merge_kernel(x_ref, g_ref, ys5_ref, yfox_ref, yconv_ref, ysb_ref, wg_ref, wb_ref, wo_ref, o_ref):
    x = x_ref[0]
    D = x.shape[-1]
    h = _rmsnorm(x, g_ref[...]).astype(BF16)
    ys = (ys5_ref[0], yfox_ref[0], yconv_ref[0], ysb_ref[0])
    merged = None
    for n in range(4):
        gate = _sigmoid(_dot(h, wg_ref[:, n * D:(n + 1) * D]))
        term = gate * _dot(ys[n], wb_ref[n])
        merged = term if merged is None else merged + term
    o_ref[0] = x + _dot(merged.astype(BF16), wo_ref[...])


def _merge(x, g, ys5, yfox, yconv, ysb, wg, wb, wo):
    B, L, D = x.shape
    tm = min(TM_MERGE, L)
    c2 = lambda b, l: (0, 0)
    yspec = pl.BlockSpec((1, tm, D_BRANCH), lambda b, l: (b, l, 0))
    return pl.pallas_call(
        _merge_kernel,
        grid=(B, L // tm),
        in_specs=[
            pl.BlockSpec((1, tm, D), lambda b, l: (b, l, 0)),
            pl.BlockSpec((1, D), c2),
            yspec, yspec, yspec, yspec,
            pl.BlockSpec((D, 4 * D), c2),
            pl.BlockSpec((4, D_BRANCH, D), lambda b, l: (0, 0, 0)),
            pl.BlockSpec((D, D), c2),
        ],
        out_specs=pl.BlockSpec((1, tm, D), lambda b, l: (b, l, 0)),
        out_shape=jax.ShapeDtypeStruct((B, L, D), F32),
        compiler_params=_cparams(("parallel", "parallel")),
        name="merge",
    )(x, g, ys5, yfox, yconv, ysb, wg, wb, wo)


def _swiglu(h, w1_ref, w3_ref, w2_ref, lead, fc):
    acc = None
    for c in range(w1_ref.shape[-1] // fc):
        cs = slice(c * fc, (c + 1) * fc)
        a = _dot(h, w1_ref[lead + (slice(None), cs)])
        b = _dot(h, w3_ref[lead + (slice(None), cs)])
        part = _dot((a * _sigmoid(a) * b).astype(BF16), w2_ref[lead + (cs, slice(None))])
        acc = part if acc is None else acc + part
    return acc


def _ffn_kernel(x_ref, g_ref, w1_ref, w3_ref, w2_ref, o_ref, *, fc):
    x = x_ref[...]
    h = _rmsnorm(x, g_ref[...]).astype(BF16)
    o_ref[...] = x + _swiglu(h, w1_ref, w3_ref, w2_ref, (), fc)


def _ffn(x2, g, w1, w3, w2):
    T, D = x2.shape
    F = w1.shape[1]
    tm = min(TM_FFN, T)
    c2 = lambda i: (0, 0)
    return pl.pallas_call(
        functools.partial(_ffn_kernel, fc=FC_FFN),
        grid=(T // tm,),
        in_specs=[
            pl.BlockSpec((tm, D), lambda i: (i, 0)),
            pl.BlockSpec((1, D), c2),
            pl.BlockSpec((D, F), c2), pl.BlockSpec((D, F), c2), pl.BlockSpec((F, D), c2),
        ],
        out_specs=pl.BlockSpec((tm, D), lambda i: (i, 0)),
        out_shape=jax.ShapeDtypeStruct((T, D), F32),
        compiler_params=pltpu.CompilerParams(dimension_semantics=("parallel",),
                                             vmem_limit_bytes=VMEM_LIMIT_FFN),
        name="dense_ffn",
    )(x2, g, w1, w3, w2)


def _router_kernel(x_ref, g_ref, wr_ref, br_ref, meta_ref, cnt_ref, carry_ref, *, tm):
    i = pl.program_id(0)

    @pl.when(i == 0)
    def _():
        carry_ref[...] = jnp.zeros_like(carry_ref)

    h = _rmsnorm(x_ref[...], g_ref[...])
    logits = jnp.dot(h, wr_ref[...], preferred_element_type=F32, precision=lax.Precision.HIGHEST) + br_ref[...]
    lane = lax.broadcasted_iota(jnp.int32, logits.shape, 1).astype(F32)
    m1 = jnp.max(logits, axis=1, keepdims=True)
    i1 = jnp.min(jnp.where(logits == m1, lane, float(LANES)), axis=1, keepdims=True)
    rest = jnp.where(lane == i1, NEG_INF, logits)
    m2 = jnp.max(rest, axis=1, keepdims=True)
    i2 = jnp.min(jnp.where(rest == m2, lane, float(LANES)), axis=1, keepdims=True)
    e2 = jnp.exp(m2 - m1)
    p1 = 1.0 / (1.0 + e2)
    p2 = e2 * p1
    onehot = jnp.where(lane == i1, 1.0, 0.0) + jnp.where(lane == i2, 1.0, 0.0)
    rr = lax.broadcasted_iota(jnp.int32, (tm, tm), 0)
    cc = lax.broadcasted_iota(jnp.int32, (tm, tm), 1)
    before = jnp.where(cc < rr, 1.0, 0.0).astype(BF16)
    prefix = _dot(before, onehot.astype(BF16)) + carry_ref[...]
    r1 = jnp.sum(jnp.where(lane == i1, prefix, 0.0), axis=1, keepdims=True)
    r2 = jnp.sum(jnp.where(lane == i2, prefix, 0.0), axis=1, keepdims=True)
    meta = jnp.zeros_like(logits)
    for n, col in enumerate((i1, i2, p1, p2, r1, r2)):
        meta = jnp.where(lane == float(n), col, meta)
    meta_ref[...] = meta
    total = carry_ref[...] + jnp.sum(onehot, axis=0, keepdims=True)
    carry_ref[...] = total
    cnt_ref[...] = total


def _router(x2, g, wr, br):
    T, D = x2.shape
    tm = min(TM_FFN, T)
    c2 = lambda i: (0, 0)
    return pl.pallas_call(
        functools.partial(_router_kernel, tm=tm),
        grid=(T // tm,),
        in_specs=[
            pl.BlockSpec((tm, D), lambda i: (i, 0)),
            pl.BlockSpec((1, D), c2),
            pl.BlockSpec((D, LANES), c2),
            pl.BlockSpec((1, LANES), c2),
        ],
        out_specs=[pl.BlockSpec((tm, LANES), lambda i: (i, 0)), pl.BlockSpec((1, LANES), c2)],
        out_shape=[jax.ShapeDtypeStruct((T, LANES), F32), jax.ShapeDtypeStruct((1, LANES), F32)],
        scratch_shapes=[pltpu.VMEM((1, LANES), F32)],
        compiler_params=_cparams(("arbitrary",)),
        name="router",
    )(x2, g, wr, br)


def _dispatch_kernel(dst_ref, ztile_ref, x_ref, xs_hbm, zero_ref, sem_ref, zsem_ref, *, tt, tm):
    i = pl.program_id(0)

    @pl.when(i == 0)
    def _():
        zero_ref[...] = jnp.zeros_like(zero_ref)
        for e in range(2 * N_EXPERTS):
            @pl.when(ztile_ref[e] >= 0)
            def _():
                start = pl.multiple_of(ztile_ref[e], tm)
                fill = pltpu.make_async_copy(zero_ref, xs_hbm.at[pl.ds(start, tm), :], zsem_ref)
                fill.start()
                fill.wait()

    def body(r, _):
        for k in range(2):
            pltpu.make_async_copy(x_ref.at[pl.ds(r, 1), :], xs_hbm.at[pl.ds(dst_ref[2 * (i * tt + r) + k], 1), :],
                                  sem_ref).start(priority=k)
        return 0

    lax.fori_loop(0, tt, body, 0, unroll=4)
    for _ in range(2):
        pltpu.make_async_copy(x_ref, xs_hbm.at[pl.ds(0, tt), :], sem_ref).wait()


def _dispatch(x2, dst, ztile, R):
    T, D = x2.shape
    tt = min(TT_MOE, T)
    grid_spec = pltpu.PrefetchScalarGridSpec(
        num_scalar_prefetch=2,
        grid=(T // tt,),
        in_specs=[pl.BlockSpec((tt, D), lambda i, d, z: (i, 0))],
        out_specs=pl.BlockSpec(memory_space=pl.ANY),
        scratch_shapes=[pltpu.VMEM((TM_MOE, D), F32), pltpu.SemaphoreType.DMA(()), pltpu.SemaphoreType.DMA(())],
    )
    return pl.pallas_call(
        functools.partial(_dispatch_kernel, tt=tt, tm=TM_MOE),
        grid_spec=grid_spec,
        out_shape=jax.ShapeDtypeStruct((R, D), F32),
        compiler_params=_cparams(("arbitrary",)),
        name="dispatch",
    )(dst, ztile, x2)


def _experts_kernel(texp_ref, nused_ref, xs_ref, g_ref, w1_ref, w3_ref, w2_ref, y_ref, *, fc):
    i = pl.program_id(0)

    @pl.when(i < nused_ref[0])
    def _():
        h = _rmsnorm(xs_ref[...], g_ref[...]).astype(BF16)
        y_ref[...] = _swiglu(h, w1_ref, w3_ref, w2_ref, (0,), fc)

    @pl.when(i >= nused_ref[0])
    def _():
        y_ref[...] = jnp.zeros_like(y_ref)


def _experts(xs, g, tile_expert, n_used, w1, w3, w2):
    R, D = xs.shape
    F = w1.shape[2]
    tm = TM_MOE
    wspec = lambda shape: pl.BlockSpec(shape, lambda i, te, n: (te[i], 0, 0))
    grid_spec = pltpu.PrefetchScalarGridSpec(
        num_scalar_prefetch=2,
        grid=(R // tm,),
        in_specs=[
            pl.BlockSpec((tm, D), lambda i, te, n: (jnp.minimum(i, n[0] - 1), 0)),
            pl.BlockSpec((1, D), lambda i, te, n: (0, 0)),
            wspec((1, D, F)), wspec((1, D, F)), wspec((1, F, D)),
        ],
        out_specs=pl.BlockSpec((tm, D), lambda i, te, n: (i, 0)),
    )
    return pl.pallas_call(
        functools.partial(_experts_kernel, fc=FC_FFN),
        grid_spec=grid_spec,
        out_shape=jax.ShapeDtypeStruct((R, D), F32),
        compiler_params=pltpu.CompilerParams(dimension_semantics=("arbitrary",),
                                             vmem_limit_bytes=VMEM_LIMIT_FFN),
        name="experts",
    )(tile_expert, n_used, xs, g, w1, w3, w2)


def _combine_kernel(dst_ref, x_ref, p_ref, y_hbm, o_ref, buf_ref, sem_ref, *, tc, n_tiles):
    i = pl.program_id(0)
    slot = lax.rem(i, 2)

    def issue(tile, s):
        base = tile * (2 * tc)

        def body(r, _):
            for k in range(2):
                pltpu.make_async_copy(y_hbm.at[pl.ds(dst_ref[base + 2 * r + k], 1), :],
                                      buf_ref.at[s, k, pl.ds(r, 1), :], sem_ref.at[s]).start(priority=k)
            return 0

        lax.fori_loop(0, tc, body, 0, unroll=4)

    @pl.when(i == 0)
    def _():
        issue(0, 0)

    @pl.when(i + 1 < n_tiles)
    def _():
        issue(i + 1, 1 - slot)

    for k in range(2):
        pltpu.make_async_copy(y_hbm.at[pl.ds(0, tc), :], buf_ref.at[slot, k], sem_ref.at[slot]).wait()
    o_ref[...] = x_ref[...] + p_ref[:, 0:1] * buf_ref[slot, 0] + p_ref[:, 1:2] * buf_ref[slot, 1]


def _combine(x2, prob, y, dst):
    T, D = x2.shape
    tc = min(TC_MOE, T)
    n_tiles = T // tc
    grid_spec = pltpu.PrefetchScalarGridSpec(
        num_scalar_prefetch=1,
        grid=(n_tiles,),
        in_specs=[pl.BlockSpec((tc, D), lambda i, d: (i, 0)), pl.BlockSpec((tc, 2), lambda i, d: (i, 0)),
                  pl.BlockSpec(memory_space=pl.ANY)],
        out_specs=pl.BlockSpec((tc, D), lambda i, d: (i, 0)),
        scratch_shapes=[pltpu.VMEM((2, 2, tc, D), F32), pltpu.SemaphoreType.DMA((2,))],
    )
    return pl.pallas_call(
        functools.partial(_combine_kernel, tc=tc, n_tiles=n_tiles),
        grid_spec=grid_spec,
        out_shape=jax.ShapeDtypeStruct((T, D), F32),
        compiler_params=_cparams(("arbitrary",)),
        name="combine",
    )(dst, x2, prob, y)


def _moe(x2, g, wr, br, w1, w3, w2):
    T, D = x2.shape
    tm = TM_MOE
    meta, cnt = _router(x2, g, wr, br)
    expert = meta[:, 0:2].astype(jnp.int32)
    prob = meta[:, 2:4]
    rank = meta[:, 4:6].astype(jnp.int32)
    counts = cnt[0, :N_EXPERTS].astype(jnp.int32)
    padded = (counts + (tm - 1)) // tm * tm
    ends = jnp.cumsum(padded)
    dst = ((ends - padded)[expert] + rank).reshape(-1)
    R = 2 * T + N_EXPERTS * tm
    tile_start = jnp.arange(R // tm, dtype=jnp.int32) * tm
    tile_expert = jnp.minimum(jnp.searchsorted(ends, tile_start, side="right"), N_EXPERTS - 1).astype(jnp.int32)
    n_used = (ends[-1] // tm).astype(jnp.int32).reshape(1)
    last_tile = jnp.where(padded > 0, ends - tm, -1)
    tail = ends[-1] + jnp.arange(N_EXPERTS, dtype=jnp.int32) * tm
    ztile = jnp.concatenate([last_tile, jnp.where(tail < R, tail, -1)]).astype(jnp.int32)
    xs = _dispatch(x2, dst, ztile, R)
    y = _experts(xs, g, tile_expert, n_used, w1, w3, w2)
    return _combine(x2, prob, y, dst)


def _s5_params(lam_re, lam_im, log_dt, b_re, b_im, c_re, c_im):
    G, P, H = S5_GROUPS, S5_STATE, S5_GROUP
    dt = jnp.exp(log_dt)[:, None]
    mag = jnp.exp(lam_re * dt)
    ab_re = mag * jnp.cos(lam_im * dt)
    ab_im = mag * jnp.sin(lam_im * dt)
    nr, ni = ab_re - 1.0, ab_im
    den = lam_re * lam_re + lam_im * lam_im
    k_re = (nr * lam_re + ni * lam_im) / den
    k_im = (ni * lam_re - nr * lam_im) / den
    bb_re = k_re[..., None] * b_re - k_im[..., None] * b_im
    bb_im = k_re[..., None] * b_im + k_im[..., None] * b_re
    eye = jnp.eye(G, dtype=F32)
    bm_re = jnp.einsum("gph,gk->ghkp", bb_re, eye).reshape(G * H, G * P)
    bm_im = jnp.einsum("gph,gk->ghkp", bb_im, eye).reshape(G * H, G * P)
    bmat = jnp.concatenate([bm_re, bm_im], axis=1).astype(BF16)
    cm_re = jnp.einsum("ghp,gk->kpgh", c_re, eye).reshape(G * P, G * H)
    cm_im = jnp.einsum("ghp,gk->kpgh", c_im, eye).reshape(G * P, G * H)
    cmat = jnp.concatenate([cm_re, -cm_im], axis=0).astype(BF16)
    a = jnp.stack([ab_re.reshape(-1), ab_im.reshape(-1)], axis=0)
    return bmat, cmat, a


def kernel(x, g_mix, w_in, b_forget, fox_q_gain, fox_k_gain, sb_q_gain, sb_k_gain, s5_lam_re, s5_lam_im, s5_log_dt, s5_b_re, s5_b_im, s5_c_re, s5_c_im, s5_d, s5_w_glu, conv_w, conv_b, conv_ln_g, conv_ln_b, w_branch, w_out, g_ffn, ffn_w1, ffn_w3, ffn_w2, router_w, router_b, moe_w1, moe_w3, moe_w2):
    B, L, D = x.shape
    depth = g_mix.shape[0]
    assert B == SUBLANES, "the S5 scan lays the batch along the sublanes"
    n_qkv = 3 * D_BRANCH
    o_f = D_BRANCH + n_qkv
    o_c = o_f + N_HEADS
    o_g = o_c + 2 * D_BRANCH + n_qkv
    scale = 1.0 / math.sqrt(HEAD_DIM)
    sel_np, const_np = _forget_feature_tables()
    sel = jnp.asarray(sel_np, BF16)
    const = jnp.asarray(const_np, F32)

    w_a = w_in[:, :, :o_f].astype(BF16)
    w_b = w_in[:, :, o_c:o_g].astype(BF16)
    w_f = jnp.pad(w_in[:, :, o_f:o_c], ((0, 0), (0, 0), (0, LANES - N_HEADS))).astype(BF16)
    w_g = w_in[:, :, o_g:].astype(BF16)

    for i in range(depth):
        bfg = jnp.pad(b_forget[i], (0, LANES - N_HEADS)).reshape(1, LANES)
        gains = jnp.stack([jnp.tile(fox_q_gain[i] * (scale * LOG2E), N_HEADS), jnp.tile(fox_k_gain[i], N_HEADS),
                           jnp.tile(sb_q_gain[i] * (scale * LOG2E), N_HEADS), jnp.tile(sb_k_gain[i], N_HEADS)],
                          axis=0)
        us5, qf, kf, vf, conv_ab, qs, ks, vs = _inproj(x, g_mix[i].reshape(1, D), w_a[i], w_b[i], w_f[i], bfg,
                                                       gains, sel, const)

        bmat, cmat, a = _s5_params(s5_lam_re[i], s5_lam_im[i], s5_log_dt[i], s5_b_re[i], s5_b_im[i],
                                   s5_c_re[i], s5_c_im[i])
        ys5 = _s5(us5, bmat, cmat, a, s5_d[i].reshape(1, D_BRANCH), s5_w_glu[i].astype(BF16))
        yfox = _fox(qf, kf, vf)
        yconv = _conv(conv_ab, conv_w[i], conv_b[i].reshape(1, -1), conv_ln_g[i].reshape(1, -1),
                      conv_ln_b[i].reshape(1, -1))
        ysb = _sb(qs, ks, vs)

        x = _merge(x, g_mix[i].reshape(1, D), ys5, yfox, yconv, ysb, w_g[i],
                   w_branch[i].astype(BF16), w_out[i].astype(BF16))

        j = i // 2
        gf = g_ffn[i].reshape(1, D)
        if i % 2 == 0:
            x2 = _ffn(x.reshape(B * L, D), gf, ffn_w1[j].astype(BF16), ffn_w3[j].astype(BF16),
                      ffn_w2[j].astype(BF16))
        else:
            wr = jnp.pad(router_w[j], ((0, 0), (0, LANES - N_EXPERTS)))
            br = jnp.pad(router_b[j], (0, LANES - N_EXPERTS), constant_values=-1e30).reshape(1, LANES)
            x2 = _moe(x.reshape(B * L, D), gf, wr, br, moe_w1[j].astype(BF16), moe_w3[j].astype(BF16),
                      moe_w2[j].astype(BF16))
        x = x2.reshape(B, L, D)
    return x
```

```python
import functools
import math

import numpy as np

import jax
import jax.numpy as jnp
from jax import lax
from jax.experimental import pallas as pl
from jax.experimental.pallas import tpu as pltpu

F32 = jnp.float32
BF16 = jnp.bfloat16
EPS = 1e-6
LOG2E = math.log2(math.e)

D_BRANCH = 256
HEAD_DIM = 64
N_HEADS = D_BRANCH // HEAD_DIM
S5_GROUP = 16
S5_GROUPS = D_BRANCH // S5_GROUP
S5_STATE = 64
N_STATE = S5_GROUPS * S5_STATE
CONV_WIDTH = 31
CONV_HALO = 32
N_EXPERTS = 8
LANES = 128
SUBLANES = 8
VMEM_LIMIT = 56 * 1024 * 1024
D_SPREAD = N_HEADS * LANES
N_SPLIT = 3

TM_PROJ = 512
TM_MERGE = 512
TM_FFN = 512
T_ATT = 512
SB_BLOCK = 256
SB_RUN_FLOOR = -160.0
LC_S5 = 128
LC_CONV = 512
TM_MOE = 512
FC_FFN = 256
TT_MOE = 1024
TC_MOE = 256
VMEM_LIMIT_FFN = 60 * 1024 * 1024

NEG_INF = float("-inf")


def _cparams(sem):
    return pltpu.CompilerParams(dimension_semantics=sem, vmem_limit_bytes=VMEM_LIMIT)


def _dot(a, b):
    return jnp.dot(a, b, preferred_element_type=F32)


def _dot_nt(a, b):
    return lax.dot_general(a, b, (((1,), (1,)), ((), ())), preferred_element_type=F32)


def _split2(x):
    hi = x.astype(BF16)
    lo = (x - hi.astype(F32)).astype(BF16)
    return hi, lo


def _rmsnorm(x, g):
    ms = jnp.mean(x * x, axis=-1, keepdims=True)
    return x * lax.rsqrt(ms + EPS) * g


def _sigmoid(x):
    return 1.0 / (1.0 + jnp.exp(-x))


def _log_sigmoid(x):
    return jnp.minimum(x, 0.0) - jnp.log(1.0 + jnp.exp(-jnp.abs(x)))


def _lane_cat(parts):
    return parts[0] if len(parts) == 1 else jnp.concatenate(parts, axis=1)


def _lane_tile(x, n):
    return _lane_cat([x] * n)


def _forget_feature_tables():
    sel = np.zeros((LANES, 2 * D_SPREAD), np.float32)
    const = np.zeros((1, 2 * D_SPREAD), np.float32)
    for h in range(N_HEADS):
        for j in range(N_SPLIT):
            sel[j * N_HEADS + h, h * LANES + HEAD_DIM + j] = 1.0
            sel[j * N_HEADS + h, D_SPREAD + h * LANES + HEAD_DIM + N_SPLIT + j] = -1.0
            const[0, h * LANES + HEAD_DIM + N_SPLIT + j] = 1.0
            const[0, D_SPREAD + h * LANES + HEAD_DIM + j] = 1.0
    return sel, const


def _inproj_kernel(x_ref, g_ref, wa_ref, wb_ref, wf_ref, bf_ref, gain_ref, sel_ref, const_ref,
                   us5_ref, qf_ref, kf_ref, vf_ref, conv_ref, qs_ref, ks_ref, vs_ref, carry_ref, *, tm):
    li = pl.program_id(1)
    x = x_ref[0]
    h = _rmsnorm(x, g_ref[...]).astype(BF16)
    lane = lax.broadcasted_iota(jnp.int32, (tm, LANES), 1)

    r = lax.broadcasted_iota(jnp.int32, (D_BRANCH, D_BRANCH), 0) // HEAD_DIM
    c = lax.broadcasted_iota(jnp.int32, (D_BRANCH, D_BRANCH), 1) // HEAD_DIM
    ones_bd = jnp.where(r == c, 1.0, 0.0).astype(BF16)

    def qknorm(t, gi):
        ss = _dot((t * t).astype(BF16), ones_bd)
        return t * lax.rsqrt(ss * (1.0 / HEAD_DIM) + EPS) * gain_ref[gi:gi + 1, :]

    def spread(t, fill):
        blocks = []
        for hp in range(2):
            pair = t[:, hp * LANES:(hp + 1) * LANES]
            blocks += [pair, pltpu.roll(pair, HEAD_DIM, axis=1)]
        out = [jnp.where(lane < HEAD_DIM, blocks[n], fill(n)) for n in range(N_HEADS)]
        return jnp.concatenate(out, axis=1).astype(BF16)

    @pl.when(li == 0)
    def _():
        carry_ref[...] = jnp.zeros_like(carry_ref)

    lf = _log_sigmoid(_dot(h, wf_ref[...]) + bf_ref[...]) * LOG2E
    rr = lax.broadcasted_iota(jnp.int32, (tm, tm), 0)
    cc = lax.broadcasted_iota(jnp.int32, (tm, tm), 1)
    tri = jnp.where(cc <= rr, 1.0, 0.0).astype(BF16)

    def split3(v):
        hi = v.astype(BF16).astype(F32)
        r1 = v - hi
        mid = r1.astype(BF16).astype(F32)
        lo = (r1 - mid).astype(BF16).astype(F32)
        return hi, mid, lo

    hi, mid, lo = split3(lf)
    cum = _dot(tri, hi.astype(BF16)) + _dot(tri, mid.astype(BF16)) + _dot(tri, lo.astype(BF16)) + carry_ref[...]
    carry_ref[...] = cum[tm - 1:tm, :]
    hi, mid, lo = split3(jnp.where(lane < N_HEADS, cum, 0.0))
    packed = hi + pltpu.roll(mid, N_HEADS, axis=1) + pltpu.roll(lo, 2 * N_HEADS, axis=1)
    feat = _dot(packed.astype(BF16), sel_ref[...]) + const_ref[...]

    def feat_q(n):
        return feat[:, n * LANES:(n + 1) * LANES]

    def feat_k(n):
        return feat[:, D_SPREAD + n * LANES:D_SPREAD + (n + 1) * LANES]

    zero = lambda n: 0.0
    one = lambda n: 1.0

    pa = _dot(h, wa_ref[...])
    us5_ref[0] = pa[:, 0:256]
    qf_ref[0] = spread(qknorm(pa[:, 256:512], 0), feat_q)
    kf_ref[0] = spread(qknorm(pa[:, 512:768], 1), feat_k)
    vf_ref[0] = spread(pa[:, 768:1024], one)
    pb = _dot(h, wb_ref[...])
    conv_ref[0] = pb[:, 0:512]
    qs_ref[0] = spread(qknorm(pb[:, 512:768], 2), zero)
    ks_ref[0] = spread(qknorm(pb[:, 768:1024], 3), zero)
    vs_ref[0] = pb[:, 1024:1280].astype(BF16)


def _inproj(x, g, wa, wb, wf, bfg, gains, sel, const):
    B, L, D = x.shape
    tm = min(TM_PROJ, L)
    c2 = lambda b, l: (0, 0)
    row_spec = lambda n: pl.BlockSpec((1, tm, n), lambda b, l: (b, l, 0))
    return pl.pallas_call(
        functools.partial(_inproj_kernel, tm=tm),
        grid=(B, L // tm),
        in_specs=[
            row_spec(D),
            pl.BlockSpec((1, D), c2),
            pl.BlockSpec(wa.shape, c2),
            pl.BlockSpec(wb.shape, c2),
            pl.BlockSpec((D, LANES), c2),
            pl.BlockSpec((1, LANES), c2),
            pl.BlockSpec((4, D_BRANCH), c2),
            pl.BlockSpec(sel.shape, c2),
            pl.BlockSpec(const.shape, c2),
        ],
        out_specs=[
            row_spec(D_BRANCH),
            row_spec(D_SPREAD), row_spec(D_SPREAD), row_spec(D_SPREAD),
            row_spec(2 * D_BRANCH),
            row_spec(D_SPREAD), row_spec(D_SPREAD), row_spec(D_BRANCH),
        ],
        out_shape=[
            jax.ShapeDtypeStruct((B, L, D_BRANCH), F32),
            jax.ShapeDtypeStruct((B, L, D_SPREAD), BF16),
            jax.ShapeDtypeStruct((B, L, D_SPREAD), BF16),
            jax.ShapeDtypeStruct((B, L, D_SPREAD), BF16),
            jax.ShapeDtypeStruct((B, L, 2 * D_BRANCH), F32),
            jax.ShapeDtypeStruct((B, L, D_SPREAD), BF16),
            jax.ShapeDtypeStruct((B, L, D_SPREAD), BF16),
            jax.ShapeDtypeStruct((B, L, D_BRANCH), BF16),
        ],
        scratch_shapes=[pltpu.VMEM((1, LANES), F32)],
        compiler_params=_cparams(("parallel", "arbitrary")),
        name="inproj",
    )(x, g, wa, wb, wf, bfg, gains, sel, const)


def _s5_kernel(u_ref, bmat_ref, cmat_ref, a_ref, d_ref, wglu_ref, y_ref, utb_ref, bu_ref, st_ref, *, lc, nb):
    ci = pl.program_id(0)

    @pl.when(ci == 0)
    def _():
        st_ref[...] = jnp.zeros_like(st_ref)

    n_half = D_BRANCH // LANES
    for b in range(nb):
        for j in range(n_half):
            utb_ref.at[j][pl.ds(b, lc, stride=nb), :] = u_ref[b, :, j * LANES:(j + 1) * LANES]
    u = jnp.concatenate([utb_ref[j] for j in range(n_half)], axis=1)
    bu_ref[...] = _dot(u.astype(BF16), bmat_ref[...])

    a_re = jnp.broadcast_to(a_ref[0:1, :], (nb, N_STATE))
    a_im = jnp.broadcast_to(a_ref[1:2, :], (nb, N_STATE))

    def step(t, carry):
        s_re, s_im = carry
        r0 = pl.multiple_of(t * nb, nb)
        b_re = bu_ref[pl.ds(r0, nb), 0:N_STATE]
        b_im = bu_ref[pl.ds(r0, nb), N_STATE:2 * N_STATE]
        n_re = a_re * s_re - a_im * s_im + b_re
        n_im = a_re * s_im + a_im * s_re + b_im
        bu_ref[pl.ds(r0, nb), 0:N_STATE] = n_re
        bu_ref[pl.ds(r0, nb), N_STATE:2 * N_STATE] = n_im
        return n_re, n_im

    s_re, s_im = lax.fori_loop(0, lc, step, (st_ref[:, 0:N_STATE], st_ref[:, N_STATE:2 * N_STATE]),
                               unroll=2)
    st_ref[:, 0:N_STATE] = s_re
    st_ref[:, N_STATE:2 * N_STATE] = s_im

    y = _dot(bu_ref[...].astype(BF16), cmat_ref[...]) + d_ref[...] * u
    y = jax.nn.gelu(y, approximate=True)
    y = y * _sigmoid(_dot(y.astype(BF16), wglu_ref[...]))
    for j in range(n_half):
        utb_ref[j] = y[:, j * LANES:(j + 1) * LANES]
    for b in range(nb):
        for j in range(n_half):
            y_ref[b, :, j * LANES:(j + 1) * LANES] = utb_ref.at[j][pl.ds(b, lc, stride=nb), :].astype(y_ref.dtype)


def _s5(u, bmat, cmat, a, d, wglu):
    nb, L, _ = u.shape
    lc = min(LC_S5, L)
    const = lambda c: (0, 0)
    return pl.pallas_call(
        functools.partial(_s5_kernel, lc=lc, nb=nb),
        grid=(L // lc,),
        in_specs=[
            pl.BlockSpec((nb, lc, D_BRANCH), lambda c: (0, c, 0)),
            pl.BlockSpec((D_BRANCH, 2 * N_STATE), const),
            pl.BlockSpec((2 * N_STATE, D_BRANCH), const),
            pl.BlockSpec((2, N_STATE), const),
            pl.BlockSpec((1, D_BRANCH), const),
            pl.BlockSpec((D_BRANCH, D_BRANCH), const),
        ],
        out_specs=pl.BlockSpec((nb, lc, D_BRANCH), lambda c: (0, c, 0)),
        out_shape=jax.ShapeDtypeStruct((nb, L, D_BRANCH), BF16),
        scratch_shapes=[pltpu.VMEM((D_BRANCH // LANES, lc * nb, LANES), F32), pltpu.VMEM((lc * nb, 2 * N_STATE), F32),
                        pltpu.VMEM((nb, 2 * N_STATE), F32)],
        compiler_params=_cparams(("arbitrary",)),
        name="s5",
    )(u, bmat, cmat, a, d, wglu)


def _fox_kernel(q_ref, k_ref, v_ref, o_ref, sa_ref, sb_ref, m_ref, acc_ref, *, tq):
    qi = pl.program_id(2)
    lane = lax.broadcasted_iota(jnp.int32, (tq, LANES), 1)

    def logits(ki, s_ref):
        k0 = pl.multiple_of(ki * tq, tq)
        for h in range(2):
            hs = slice(h * LANES, (h + 1) * LANES)
            s_ref[h] = _dot_nt(q_ref[0, :, hs], k_ref[0, pl.ds(k0, tq), hs])

    def update_rows(ki, s_ref, r0, nr, nc, masked):
        k0 = pl.multiple_of(ki * tq, tq)
        rs = slice(r0, r0 + nr)
        for h in range(2):
            hs = slice(h * LANES, (h + 1) * LANES)
            s = s_ref[h, rs, 0:nc]
            if masked:
                row = lax.broadcasted_iota(jnp.int32, (nr, nc), 0) + r0
                col = lax.broadcasted_iota(jnp.int32, (nr, nc), 1)
                s = jnp.where(col <= row, s, NEG_INF)
            m = m_ref[h, rs]
            m_new = jnp.maximum(m, jnp.max(s, axis=1, keepdims=True))
            p = jnp.exp2(s - _lane_tile(m_new, nc // LANES))
            acc_ref[h, rs] = (jnp.exp2(m - m_new) * acc_ref[h, rs]
                              + _dot(p.astype(BF16), v_ref[0, pl.ds(k0, nc), hs]))
            m_ref[h, rs] = m_new

    def update(ki, s_ref, diag):
        if diag:
            half = tq // 2
            update_rows(ki, s_ref, 0, half, half, True)
            update_rows(ki, s_ref, half, half, tq, True)
        else:
            update_rows(ki, s_ref, 0, tq, tq, False)

    m_ref[...] = jnp.full(m_ref.shape, NEG_INF, F32)
    acc_ref[...] = jnp.zeros_like(acc_ref)
    logits(0, sa_ref)

    def pair(j, _):
        logits(2 * j + 1, sb_ref)
        update(2 * j, sa_ref, False)
        logits(2 * j + 2, sa_ref)
        update(2 * j + 1, sb_ref, False)
        return 0

    lax.fori_loop(0, qi // 2, pair, 0)

    @pl.when(qi % 2 == 0)
    def _():
        update(qi, sa_ref, True)

    @pl.when(qi % 2 == 1)
    def _():
        logits(qi, sb_ref)
        update(qi - 1, sa_ref, False)
        update(qi, sb_ref, True)

    res = [acc_ref[h] / pltpu.roll(acc_ref[h], HEAD_DIM, axis=1) for h in range(2)]
    o_ref[0] = jnp.where(lane < HEAD_DIM, res[0], pltpu.roll(res[1], HEAD_DIM, axis=1)).astype(o_ref.dtype)


def _fox(q, k, v):
    B, L, _ = q.shape
    tq = min(T_ATT, L)
    return pl.pallas_call(
        functools.partial(_fox_kernel, tq=tq),
        grid=(B, 2, L // tq),
        in_specs=[
            pl.BlockSpec((1, tq, 2 * LANES), lambda b, p, i: (b, i, p)),
            pl.BlockSpec((1, L, 2 * LANES), lambda b, p, i: (b, 0, p)),
            pl.BlockSpec((1, L, 2 * LANES), lambda b, p, i: (b, 0, p)),
        ],
        out_specs=pl.BlockSpec((1, tq, LANES), lambda b, p, i: (b, i, p)),
        out_shape=jax.ShapeDtypeStruct((B, L, D_BRANCH), BF16),
        scratch_shapes=[pltpu.VMEM((2, tq, tq), F32), pltpu.VMEM((2, tq, tq), F32),
                        pltpu.VMEM((2, tq, LANES), F32), pltpu.VMEM((2, tq, LANES), F32)],
        compiler_params=_cparams(("parallel", "parallel", "arbitrary")),
        name="fox",
    )(q, k, v)


def _sb_kernel(q_ref, k_ref, v_ref, o_ref, za_ref, zb_ref, run_ref, acc_ref, *, tq, blk):
    qi = pl.program_id(2)
    lane = lax.broadcasted_iota(jnp.int32, (tq, LANES), 1)
    ur = lax.broadcasted_iota(jnp.int32, (blk, blk), 0)
    uc = lax.broadcasted_iota(jnp.int32, (blk, blk), 1)
    upper = jnp.where(ur > uc, 1.0, 0.0).astype(BF16)

    def logits(ki, z_ref):
        k0 = pl.multiple_of(jnp.maximum(ki, 0) * tq, tq)
        for h in range(2):
            hs = slice(h * LANES, (h + 1) * LANES)
            z_ref[h] = _dot_nt(q_ref[0, :, hs], k_ref[0, pl.ds(k0, tq), hs])

    def update_rows(ki, z_ref, r0, nr, nc, masked):
        k0 = pl.multiple_of(ki * tq, tq)
        rs = slice(r0, r0 + nr)
        vt = v_ref[0, pl.ds(k0, nc), :]
        if masked:
            past = (lax.broadcasted_iota(jnp.int32, (nr, nc), 1)
                    < lax.broadcasted_iota(jnp.int32, (nr, nc), 0) + r0)
        for h in range(2):
            run = run_ref[h, rs]
            z = z_ref[h, rs, 0:nc]
            neg_abs = pltpu.bitcast(pltpu.bitcast(z, jnp.uint32) | jnp.uint32(0x80000000), F32)
            l1p = jnp.log(1.0 + jnp.exp2(neg_abs)) * LOG2E
            log_beta = jnp.minimum(z, 0.0) - l1p
            log_keep = log_beta - z
            if masked:
                log_keep = jnp.where(past, log_keep, 0.0)
            keep16 = log_keep.astype(BF16)
            n_c = nc // blk
            after = [None] * n_c
            for c in reversed(range(n_c)):
                cs = slice(c * blk, (c + 1) * blk)
                raw = _dot(keep16[:, cs], upper)
                after[c] = raw + _lane_tile(run, blk // LANES)
                run = run + (raw[:, 0:1] + log_keep[:, c * blk:c * blk + 1])
            a = jnp.exp2(log_beta + _lane_cat(after))
            if masked:
                a = jnp.where(past, a, 0.0)
            acc_ref[h, rs] += _dot(a.astype(BF16), vt)
            run_ref[h, rs] = run

    def update(ki, z_ref, diag):
        if diag:
            half = tq // 2
            update_rows(ki, z_ref, 0, half, half, True)
            update_rows(ki, z_ref, half, half, tq, True)
        else:
            update_rows(ki, z_ref, 0, tq, tq, False)

    run_ref[...] = jnp.zeros_like(run_ref)
    acc_ref[...] = jnp.zeros_like(acc_ref)
    logits(qi, za_ref)
    logits(qi - 1, zb_ref)
    update(qi, za_ref, True)

    def alive():
        return jnp.max(run_ref[...]) > SB_RUN_FLOOR

    def pair(carry):
        j, _ = carry
        t = qi - 1 - 2 * j
        logits(t - 1, za_ref)
        update(t, zb_ref, False)

        @pl.when(alive())
        def _():
            logits(t - 2, zb_ref)
            update(t - 1, za_ref, False)

        return j + 1, alive()

    n_pairs = qi // 2
    _, live = lax.while_loop(lambda c: (c[0] < n_pairs) & c[1], pair, (jnp.int32(0), alive()))

    @pl.when((qi % 2 == 1) & live)
    def _():
        update(0, zb_ref, False)

    o_ref[0] = jnp.where(lane < HEAD_DIM, acc_ref[0], acc_ref[1]).astype(o_ref.dtype)


def _sb(q, k, v):
    B, L, _ = q.shape
    tq = min(T_ATT, L)
    return pl.pallas_call(
        functools.partial(_sb_kernel, tq=tq, blk=min(SB_BLOCK, tq)),
        grid=(B, 2, L // tq),
        in_specs=[
            pl.BlockSpec((1, tq, 2 * LANES), lambda b, p, i: (b, i, p)),
            pl.BlockSpec((1, L, 2 * LANES), lambda b, p, i: (b, 0, p)),
            pl.BlockSpec((1, L, LANES), lambda b, p, i: (b, 0, p)),
        ],
        out_specs=pl.BlockSpec((1, tq, LANES), lambda b, p, i: (b, i, p)),
        out_shape=jax.ShapeDtypeStruct((B, L, D_BRANCH), BF16),
        scratch_shapes=[pltpu.VMEM((2, tq, tq), F32), pltpu.VMEM((2, tq, tq), F32),
                        pltpu.VMEM((2, tq, LANES), F32), pltpu.VMEM((2, tq, LANES), F32)],
        compiler_params=_cparams(("parallel", "parallel", "arbitrary")),
        name="stickbreak",
    )(q, k, v)


def _conv_kernel(ab_ref, w_ref, b_ref, g_ref, beta_ref, o_ref, pad_ref, sh_ref, *, lc):
    li = pl.program_id(1)

    @pl.when(li == 0)
    def _():
        pad_ref[0:CONV_HALO, :] = jnp.zeros((CONV_HALO, D_BRANCH), F32)

    ab = ab_ref[0]
    pad_ref[CONV_HALO:CONV_HALO + lc, :] = ab[:, 0:D_BRANCH] * _sigmoid(ab[:, D_BRANCH:2 * D_BRANCH])
    span = lc + CONV_HALO - SUBLANES
    for ph in range(1, SUBLANES):
        sh_ref[ph - 1] = pad_ref[ph:ph + span, :]
    off = CONV_HALO - (CONV_WIDTH - 1)
    acc = jnp.zeros((lc, D_BRANCH), F32) + b_ref[...]
    for j in range(CONV_WIDTH):
        ph, base = (off + j) % SUBLANES, (off + j) // SUBLANES * SUBLANES
        tap = pad_ref[base:base + lc, :] if ph == 0 else sh_ref[ph - 1, base:base + lc, :]
        acc = acc + w_ref[j:j + 1, :] * tap
    pad_ref[0:CONV_HALO, :] = pad_ref[lc:lc + CONV_HALO, :]
    mu = jnp.mean(acc, axis=-1, keepdims=True)
    xc = acc - mu
    var = jnp.mean(xc * xc, axis=-1, keepdims=True)
    y = xc * lax.rsqrt(var + EPS) * g_ref[...] + beta_ref[...]
    o_ref[0] = (y * _sigmoid(y)).astype(o_ref.dtype)


def _conv(ab, w, b, g, beta):
    B, L, _ = ab.shape
    lc = min(LC_CONV, L)
    const = lambda b_, l: (0, 0)
    return pl.pallas_call(
        functools.partial(_conv_kernel, lc=lc),
        grid=(B, L // lc),
        in_specs=[
            pl.BlockSpec((1, lc, 2 * D_BRANCH), lambda b_, l: (b_, l, 0)),
            pl.BlockSpec((CONV_WIDTH, D_BRANCH), const),
            pl.BlockSpec((1, D_BRANCH), const),
            pl.BlockSpec((1, D_BRANCH), const),
            pl.BlockSpec((1, D_BRANCH), const),
        ],
        out_specs=pl.BlockSpec((1, lc, D_BRANCH), lambda b_, l: (b_, l, 0)),
        out_shape=jax.ShapeDtypeStruct((B, L, D_BRANCH), BF16),
        scratch_shapes=[pltpu.VMEM((CONV_HALO + lc, D_BRANCH), F32),
                        pltpu.VMEM((SUBLANES - 1, CONV_HALO + lc - SUBLANES, D_BRANCH), F32)],
        compiler_params=_cparams(("parallel", "arbitrary")),
        name="conv",
    )(ab, w, b, g, beta)


def _merge_kernel(x_ref, g_ref, ys5_ref, yfox_ref, yconv_ref, ysb_ref, wg_ref, wb_ref, wo_ref, o_ref):
    x = x_ref[0]
    D = x.shape[-1]
    h = _rmsnorm(x, g_ref[...]).astype(BF16)
    ys = (ys5_ref[0], yfox_ref[0], yconv_ref[0], ysb_ref[0])
    merged = None
    for n in range(4):
        gate = _sigmoid(_dot(h, wg_ref[:, n * D:(n + 1) * D]))
        term = gate * _dot(ys[n], wb_ref[n])
        merged = term if merged is None else merged + term
    o_ref[0] = x + _dot(merged.astype(BF16), wo_ref[...])


def _merge(x, g, ys5, yfox, yconv, ysb, wg, wb, wo):
    B, L, D = x.shape
    tm = min(TM_MERGE, L)
    c2 = lambda b, l: (0, 0)
    yspec = pl.BlockSpec((1, tm, D_BRANCH), lambda b, l: (b, l, 0))
    return pl.pallas_call(
        _merge_kernel,
        grid=(B, L // tm),
        in_specs=[
            pl.BlockSpec((1, tm, D), lambda b, l: (b, l, 0)),
            pl.BlockSpec((1, D), c2),
            yspec, yspec, yspec, yspec,
            pl.BlockSpec((D, 4 * D), c2),
            pl.BlockSpec((4, D_BRANCH, D), lambda b, l: (0, 0, 0)),
            pl.BlockSpec((D, D), c2),
        ],
        out_specs=pl.BlockSpec((1, tm, D), lambda b, l: (b, l, 0)),
        out_shape=jax.ShapeDtypeStruct((B, L, D), F32),
        compiler_params=_cparams(("parallel", "parallel")),
        name="merge",
    )(x, g, ys5, yfox, yconv, ysb, wg, wb, wo)


def _swiglu(h, w1_ref, w3_ref, w2_ref, lead, fc):
    acc = None
    for c in range(w1_ref.shape[-1] // fc):
        cs = slice(c * fc, (c + 1) * fc)
        a = _dot(h, w1_ref[lead + (slice(None), cs)])
        b = _dot(h, w3_ref[lead + (slice(None), cs)])
        part = _dot((a * _sigmoid(a) * b).astype(BF16), w2_ref[lead + (cs, slice(None))])
        acc = part if acc is None else acc + part
    return acc


def _ffn_kernel(x_ref, g_ref, w1_ref, w3_ref, w2_ref, o_ref, *, fc):
    x = x_ref[...]
    h = _rmsnorm(x, g_ref[...]).astype(BF16)
    o_ref[...] = x + _swiglu(h, w1_ref, w3_ref, w2_ref, (), fc)


def _ffn(x2, g, w1, w3, w2):
    T, D = x2.shape
    F = w1.shape[1]
    tm = min(TM_FFN, T)
    c2 = lambda i: (0, 0)
    return pl.pallas_call(
        functools.partial(_ffn_kernel, fc=FC_FFN),
        grid=(T // tm,),
        in_specs=[
            pl.BlockSpec((tm, D), lambda i: (i, 0)),
            pl.BlockSpec((1, D), c2),
            pl.BlockSpec((D, F), c2), pl.BlockSpec((D, F), c2), pl.BlockSpec((F, D), c2),
        ],
        out_specs=pl.BlockSpec((tm, D), lambda i: (i, 0)),
        out_shape=jax.ShapeDtypeStruct((T, D), F32),
        compiler_params=pltpu.CompilerParams(dimension_semantics=("parallel",),
                                             vmem_limit_bytes=VMEM_LIMIT_FFN),
        name="dense_ffn",
    )(x2, g, w1, w3, w2)


def _router_kernel(x_ref, g_ref, wr_ref, br_ref, meta_ref, cnt_ref, carry_ref, *, tm):
    i = pl.program_id(0)

    @pl.when(i == 0)
    def _():
        carry_ref[...] = jnp.zeros_like(carry_ref)

    h = _rmsnorm(x_ref[...], g_ref[...])
    logits = jnp.dot(h, wr_ref[...], preferred_element_type=F32, precision=lax.Precision.HIGHEST) + br_ref[...]
    lane = lax.broadcasted_iota(jnp.int32, logits.shape, 1).astype(F32)
    m1 = jnp.max(logits, axis=1, keepdims=True)
    i1 = jnp.min(jnp.where(logits == m1, lane, float(LANES)), axis=1, keepdims=True)
    rest = jnp.where(lane == i1, NEG_INF, logits)
    m2 = jnp.max(rest, axis=1, keepdims=True)
    i2 = jnp.min(jnp.where(rest == m2, lane, float(LANES)), axis=1, keepdims=True)
    e2 = jnp.exp(m2 - m1)
    p1 = 1.0 / (1.0 + e2)
    p2 = e2 * p1
    onehot = jnp.where(lane == i1, 1.0, 0.0) + jnp.where(lane == i2, 1.0, 0.0)
    rr = lax.broadcasted_iota(jnp.int32, (tm, tm), 0)
    cc = lax.broadcasted_iota(jnp.int32, (tm, tm), 1)
    before = jnp.where(cc < rr, 1.0, 0.0).astype(BF16)
    prefix = _dot(before, onehot.astype(BF16)) + carry_ref[...]
    r1 = jnp.sum(jnp.where(lane == i1, prefix, 0.0), axis=1, keepdims=True)
    r2 = jnp.sum(jnp.where(lane == i2, prefix, 0.0), axis=1, keepdims=True)
    meta = jnp.zeros_like(logits)
    for n, col in enumerate((i1, i2, p1, p2, r1, r2)):
        meta = jnp.where(lane == float(n), col, meta)
    meta_ref[...] = meta
    total = carry_ref[...] + jnp.sum(onehot, axis=0, keepdims=True)
    carry_ref[...] = total
    cnt_ref[...] = total


def _router(x2, g, wr, br):
    T, D = x2.shape
    tm = min(TM_FFN, T)
    c2 = lambda i: (0, 0)
    return pl.pallas_call(
        functools.partial(_router_kernel, tm=tm),
        grid=(T // tm,),
        in_specs=[
            pl.BlockSpec((tm, D), lambda i: (i, 0)),
            pl.BlockSpec((1, D), c2),
            pl.BlockSpec((D, LANES), c2),
            pl.BlockSpec((1, LANES), c2),
        ],
        out_specs=[pl.BlockSpec((tm, LANES), lambda i: (i, 0)), pl.BlockSpec((1, LANES), c2)],
        out_shape=[jax.ShapeDtypeStruct((T, LANES), F32), jax.ShapeDtypeStruct((1, LANES), F32)],
        scratch_shapes=[pltpu.VMEM((1, LANES), F32)],
        compiler_params=_cparams(("arbitrary",)),
        name="router",
    )(x2, g, wr, br)


def _to_slabs(x):
    return x.reshape(x.shape[0], SUBLANES, x.shape[1] // SUBLANES)


def _from_slabs(x):
    return x.reshape(x.shape[0], x.shape[1] * x.shape[2])


def _dispatch_kernel(dst_ref, ztile_ref, x_ref, xs_hbm, xt_ref, zero_ref, sem_ref, zsem_ref, *, tt, tm):
    i = pl.program_id(0)

    @pl.when(i == 0)
    def _():
        zero_ref[...] = jnp.zeros_like(zero_ref)
        for e in range(2 * N_EXPERTS):
            @pl.when(ztile_ref[e] >= 0)
            def _():
                start = pl.multiple_of(ztile_ref[e], tm)
                fill = pltpu.make_async_copy(zero_ref, xs_hbm.at[pl.ds(start, tm)], zsem_ref)
                fill.start()
                fill.wait()

    xt_ref[...] = _to_slabs(x_ref[...])

    def body(r, _):
        for k in range(2):
            pltpu.make_async_copy(xt_ref.at[r], xs_hbm.at[dst_ref[2 * (i * tt + r) + k]],
                                  sem_ref).start(priority=k)
        return 0

    lax.fori_loop(0, tt, body, 0, unroll=4)
    for _ in range(2):
        pltpu.make_async_copy(xt_ref, xs_hbm.at[pl.ds(0, tt)], sem_ref).wait()


def _dispatch(x2, dst, ztile, R):
    T, D = x2.shape
    tt = min(TT_MOE, T)
    slab = (SUBLANES, D // SUBLANES)
    grid_spec = pltpu.PrefetchScalarGridSpec(
        num_scalar_prefetch=2,
        grid=(T // tt,),
        in_specs=[pl.BlockSpec((tt, D), lambda i, d, z: (i, 0))],
        out_specs=pl.BlockSpec(memory_space=pl.ANY),
        scratch_shapes=[pltpu.VMEM((tt,) + slab, F32), pltpu.VMEM((TM_MOE,) + slab, F32),
                        pltpu.SemaphoreType.DMA(()), pltpu.SemaphoreType.DMA(())],
    )
    return pl.pallas_call(
        functools.partial(_dispatch_kernel, tt=tt, tm=TM_MOE),
        grid_spec=grid_spec,
        out_shape=jax.ShapeDtypeStruct((R,) + slab, F32),
        compiler_params=_cparams(("arbitrary",)),
        name="dispatch",
    )(dst, ztile, x2)


def _experts_kernel(texp_ref, nused_ref, xs_ref, g_ref, w1_ref, w3_ref, w2_ref, y_ref, *, fc):
    i = pl.program_id(0)

    @pl.when(i < nused_ref[0])
    def _():
        h = _rmsnorm(_from_slabs(xs_ref[...]), g_ref[...]).astype(BF16)
        y_ref[...] = _to_slabs(_swiglu(h, w1_ref, w3_ref, w2_ref, (0,), fc))

    @pl.when(i >= nused_ref[0])
    def _():
        y_ref[...] = jnp.zeros_like(y_ref)


def _experts(xs, g, tile_expert, n_used, w1, w3, w2):
    R = xs.shape[0]
    slab = xs.shape[1:]
    D, F = w1.shape[1:]
    tm = TM_MOE
    wspec = lambda shape: pl.BlockSpec(shape, lambda i, te, n: (te[i], 0, 0))
    grid_spec = pltpu.PrefetchScalarGridSpec(
        num_scalar_prefetch=2,
        grid=(R // tm,),
        in_specs=[
            pl.BlockSpec((tm,) + slab, lambda i, te, n: (jnp.minimum(i, n[0] - 1), 0, 0)),
            pl.BlockSpec((1, D), lambda i, te, n: (0, 0)),
            wspec((1, D, F)), wspec((1, D, F)), wspec((1, F, D)),
        ],
        out_specs=pl.BlockSpec((tm,) + slab, lambda i, te, n: (i, 0, 0)),
    )
    return pl.pallas_call(
        functools.partial(_experts_kernel, fc=FC_FFN),
        grid_spec=grid_spec,
        out_shape=jax.ShapeDtypeStruct((R,) + slab, F32),
        compiler_params=pltpu.CompilerParams(dimension_semantics=("arbitrary",),
                                             vmem_limit_bytes=VMEM_LIMIT_FFN),
        name="experts",
    )(tile_expert, n_used, xs, g, w1, w3, w2)


def _combine_kernel(dst_ref, x_ref, p_ref, y_hbm, o_ref, buf_ref, sem_ref, *, tc, n_tiles):
    i = pl.program_id(0)
    slot = lax.rem(i, 2)

    def issue(tile, s):
        base = tile * (2 * tc)

        def body(r, _):
            for k in range(2):
                pltpu.make_async_copy(y_hbm.at[dst_ref[base + 2 * r + k]], buf_ref.at[s, k, r],
                                      sem_ref.at[s]).start(priority=k)
            return 0

        lax.fori_loop(0, tc, body, 0, unroll=4)

    @pl.when(i == 0)
    def _():
        issue(0, 0)

    @pl.when(i + 1 < n_tiles)
    def _():
        issue(i + 1, 1 - slot)

    for k in range(2):
        pltpu.make_async_copy(y_hbm.at[pl.ds(0, tc)], buf_ref.at[slot, k], sem_ref.at[slot]).wait()
    o_ref[...] = (x_ref[...] + p_ref[:, 0:1] * _from_slabs(buf_ref[slot, 0])
                  + p_ref[:, 1:2] * _from_slabs(buf_ref[slot, 1]))


def _combine(x2, prob, y, dst):
    T, D = x2.shape
    tc = min(TC_MOE, T)
    n_tiles = T // tc
    grid_spec = pltpu.PrefetchScalarGridSpec(
        num_scalar_prefetch=1,
        grid=(n_tiles,),
        in_specs=[pl.BlockSpec((tc, D), lambda i, d: (i, 0)), pl.BlockSpec((tc, 2), lambda i, d: (i, 0)),
                  pl.BlockSpec(memory_space=pl.ANY)],
        out_specs=pl.BlockSpec((tc, D), lambda i, d: (i, 0)),
        scratch_shapes=[pltpu.VMEM((2, 2, tc) + y.shape[1:], F32), pltpu.SemaphoreType.DMA((2,))],
    )
    return pl.pallas_call(
        functools.partial(_combine_kernel, tc=tc, n_tiles=n_tiles),
        grid_spec=grid_spec,
        out_shape=jax.ShapeDtypeStruct((T, D), F32),
        compiler_params=_cparams(("arbitrary",)),
        name="combine",
    )(dst, x2, prob, y)


def _moe(x2, g, wr, br, w1, w3, w2):
    T, D = x2.shape
    tm = TM_MOE
    meta, cnt = _router(x2, g, wr, br)
    expert = meta[:, 0:2].astype(jnp.int32)
    prob = meta[:, 2:4]
    rank = meta[:, 4:6].astype(jnp.int32)
    counts = cnt[0, :N_EXPERTS].astype(jnp.int32)
    padded = (counts + (tm - 1)) // tm * tm
    ends = jnp.cumsum(padded)
    dst = ((ends - padded)[expert] + rank).reshape(-1)
    R = 2 * T + N_EXPERTS * tm
    tile_start = jnp.arange(R // tm, dtype=jnp.int32) * tm
    tile_expert = jnp.minimum(jnp.searchsorted(ends, tile_start, side="right"), N_EXPERTS - 1).astype(jnp.int32)
    n_used = (ends[-1] // tm).astype(jnp.int32).reshape(1)
    last_tile = jnp.where(padded > 0, ends - tm, -1)
    tail = ends[-1] + jnp.arange(N_EXPERTS, dtype=jnp.int32) * tm
    ztile = jnp.concatenate([last_tile, jnp.where(tail < R, tail, -1)]).astype(jnp.int32)
    xs = _dispatch(x2, dst, ztile, R)
    y = _experts(xs, g, tile_expert, n_used, w1, w3, w2)
    return _combine(x2, prob, y, dst)


def _s5_params(lam_re, lam_im, log_dt, b_re, b_im, c_re, c_im):
    G, P, H = S5_GROUPS, S5_STATE, S5_GROUP
    dt = jnp.exp(log_dt)[:, None]
    mag = jnp.exp(lam_re * dt)
    ab_re = mag * jnp.cos(lam_im * dt)
    ab_im = mag * jnp.sin(lam_im * dt)
    nr, ni = ab_re - 1.0, ab_im
    den = lam_re * lam_re + lam_im * lam_im
    k_re = (nr * lam_re + ni * lam_im) / den
    k_im = (ni * lam_re - nr * lam_im) / den
    bb_re = k_re[..., None] * b_re - k_im[..., None] * b_im
    bb_im = k_re[..., None] * b_im + k_im[..., None] * b_re
    eye = jnp.eye(G, dtype=F32)
    bm_re = jnp.einsum("gph,gk->ghkp", bb_re, eye).reshape(G * H, G * P)
    bm_im = jnp.einsum("gph,gk->ghkp", bb_im, eye).reshape(G * H, G * P)
    bmat = jnp.concatenate([bm_re, bm_im], axis=1).astype(BF16)
    cm_re = jnp.einsum("ghp,gk->kpgh", c_re, eye).reshape(G * P, G * H)
    cm_im = jnp.einsum("ghp,gk->kpgh", c_im, eye).reshape(G * P, G * H)
    cmat = jnp.concatenate([cm_re, -cm_im], axis=0).astype(BF16)
    a = jnp.stack([ab_re.reshape(-1), ab_im.reshape(-1)], axis=0)
    return bmat, cmat, a


def kernel(x, g_mix, w_in, b_forget, fox_q_gain, fox_k_gain, sb_q_gain, sb_k_gain, s5_lam_re, s5_lam_im, s5_log_dt, s5_b_re, s5_b_im, s5_c_re, s5_c_im, s5_d, s5_w_glu, conv_w, conv_b, conv_ln_g, conv_ln_b, w_branch, w_out, g_ffn, ffn_w1, ffn_w3, ffn_w2, router_w, router_b, moe_w1, moe_w3, moe_w2):
    B, L, D = x.shape
    depth = g_mix.shape[0]
    assert B == SUBLANES, "the S5 scan lays the batch along the sublanes"
    n_qkv = 3 * D_BRANCH
    o_f = D_BRANCH + n_qkv
    o_c = o_f + N_HEADS
    o_g = o_c + 2 * D_BRANCH + n_qkv
    scale = 1.0 / math.sqrt(HEAD_DIM)
    sel_np, const_np = _forget_feature_tables()
    sel = jnp.asarray(sel_np, BF16)
    const = jnp.asarray(const_np, F32)

    w_a = w_in[:, :, :o_f].astype(BF16)
    w_b = w_in[:, :, o_c:o_g].astype(BF16)
    w_f = jnp.pad(w_in[:, :, o_f:o_c], ((0, 0), (0, 0), (0, LANES - N_HEADS))).astype(BF16)
    w_g = w_in[:, :, o_g:].astype(BF16)

    for i in range(depth):
        bfg = jnp.pad(b_forget[i], (0, LANES - N_HEADS)).reshape(1, LANES)
        gains = jnp.stack([jnp.tile(fox_q_gain[i] * (scale * LOG2E), N_HEADS), jnp.tile(fox_k_gain[i], N_HEADS),
                           jnp.tile(sb_q_gain[i] * (scale * LOG2E), N_HEADS), jnp.tile(sb_k_gain[i], N_HEADS)],
                          axis=0)
        us5, qf, kf, vf, conv_ab, qs, ks, vs = _inproj(x, g_mix[i].reshape(1, D), w_a[i], w_b[i], w_f[i], bfg,
                                                       gains, sel, const)

        bmat, cmat, a = _s5_params(s5_lam_re[i], s5_lam_im[i], s5_log_dt[i], s5_b_re[i], s5_b_im[i],
                                   s5_c_re[i], s5_c_im[i])
        ys5 = _s5(us5, bmat, cmat, a, s5_d[i].reshape(1, D_BRANCH), s5_w_glu[i].astype(BF16))
        yfox = _fox(qf, kf, vf)
        yconv = _conv(conv_ab, conv_w[i], conv_b[i].reshape(1, -1), conv_ln_g[i].reshape(1, -1),
                      conv_ln_b[i].reshape(1, -1))
        ysb = _sb(qs, ks, vs)

        x = _merge(x, g_mix[i].reshape(1, D), ys5, yfox, yconv, ysb, w_g[i],
                   w_branch[i].astype(BF16), w_out[i].astype(BF16))

        j = i // 2
        gf = g_ffn[i].reshape(1, D)
        if i % 2 == 0:
            x2 = _ffn(x.reshape(B * L, D), gf, ffn_w1[j].astype(BF16), ffn_w3[j].astype(BF16),
                      ffn_w2[j].astype(BF16))
        else:
            wr = jnp.pad(router_w[j], ((0, 0), (0, LANES - N_EXPERTS)))
            br = jnp.pad(router_b[j], (0, LANES - N_EXPERTS), constant_values=-1e30).reshape(1, LANES)
            x2 = _moe(x.reshape(B * L, D), gf, wr, br, moe_w1[j].astype(BF16), moe_w3[j].astype(BF16),
                      moe_w2[j].astype(BF16))
        x = x2.reshape(B, L, D)
    return x
```

```python
import functools
import math

import numpy as np

import jax
import jax.numpy as jnp
from jax import lax
from jax.experimental import pallas as pl
from jax.experimental.pallas import tpu as pltpu

F32 = jnp.float32
BF16 = jnp.bfloat16
EPS = 1e-6
LOG2E = math.log2(math.e)

D_BRANCH = 256
HEAD_DIM = 64
N_HEADS = D_BRANCH // HEAD_DIM
S5_GROUP = 16
S5_GROUPS = D_BRANCH // S5_GROUP
S5_STATE = 64
N_STATE = S5_GROUPS * S5_STATE
CONV_WIDTH = 31
CONV_HALO = 32
N_EXPERTS = 8
LANES = 128
SUBLANES = 8
VMEM_LIMIT = 56 * 1024 * 1024
D_SPREAD = N_HEADS * LANES
N_SPLIT = 3

TM_PROJ = 512
TM_MERGE = 512
TM_FFN = 512
T_ATT = 512
T_FOX = 1024
SB_BLOCK = 256
SB_RUN_FLOOR = -160.0
LC_S5 = 128
LC_CONV = 512
TM_MOE = 512
FC_FFN = 256
TT_MOE = 1024
TC_MOE = 256
VMEM_LIMIT_FFN = 60 * 1024 * 1024

NEG_INF = float("-inf")


def _cparams(sem):
    return pltpu.CompilerParams(dimension_semantics=sem, vmem_limit_bytes=VMEM_LIMIT)


def _dot(a, b):
    return jnp.dot(a, b, preferred_element_type=F32)


def _dot_nt(a, b):
    return lax.dot_general(a, b, (((1,), (1,)), ((), ())), preferred_element_type=F32)


def _split2(x):
    hi = x.astype(BF16)
    lo = (x - hi.astype(F32)).astype(BF16)
    return hi, lo


def _rmsnorm(x, g):
    ms = jnp.mean(x * x, axis=-1, keepdims=True)
    return x * lax.rsqrt(ms + EPS) * g


def _sigmoid(x):
    return 1.0 / (1.0 + jnp.exp(-x))


def _log_sigmoid(x):
    return jnp.minimum(x, 0.0) - jnp.log(1.0 + jnp.exp(-jnp.abs(x)))


def _lane_cat(parts):
    return parts[0] if len(parts) == 1 else jnp.concatenate(parts, axis=1)


def _lane_tile(x, n):
    return _lane_cat([x] * n)


def _forget_feature_tables():
    sel = np.zeros((LANES, 2 * D_SPREAD), np.float32)
    const = np.zeros((1, 2 * D_SPREAD), np.float32)
    for h in range(N_HEADS):
        for j in range(N_SPLIT):
            sel[j * N_HEADS + h, h * LANES + HEAD_DIM + j] = 1.0
            sel[j * N_HEADS + h, D_SPREAD + h * LANES + HEAD_DIM + N_SPLIT + j] = -1.0
            const[0, h * LANES + HEAD_DIM + N_SPLIT + j] = 1.0
            const[0, D_SPREAD + h * LANES + HEAD_DIM + j] = 1.0
    return sel, const


def _inproj_kernel(x_ref, g_ref, wa_ref, wb_ref, wf_ref, bf_ref, gain_ref, sel_ref, const_ref,
                   us5_ref, qf_ref, kf_ref, vf_ref, conv_ref, qs_ref, ks_ref, vs_ref, carry_ref, *, tm):
    li = pl.program_id(1)
    x = x_ref[0]
    h = _rmsnorm(x, g_ref[...]).astype(BF16)
    lane = lax.broadcasted_iota(jnp.int32, (tm, LANES), 1)

    r = lax.broadcasted_iota(jnp.int32, (D_BRANCH, D_BRANCH), 0) // HEAD_DIM
    c = lax.broadcasted_iota(jnp.int32, (D_BRANCH, D_BRANCH), 1) // HEAD_DIM
    ones_bd = jnp.where(r == c, 1.0, 0.0).astype(BF16)

    def qknorm(t, gi):
        ss = _dot((t * t).astype(BF16), ones_bd)
        return t * lax.rsqrt(ss * (1.0 / HEAD_DIM) + EPS) * gain_ref[gi:gi + 1, :]

    def spread(t, fill):
        blocks = []
        for hp in range(2):
            pair = t[:, hp * LANES:(hp + 1) * LANES]
            blocks += [pair, pltpu.roll(pair, HEAD_DIM, axis=1)]
        out = [jnp.where(lane < HEAD_DIM, blocks[n], fill(n)) for n in range(N_HEADS)]
        return jnp.concatenate(out, axis=1).astype(BF16)

    @pl.when(li == 0)
    def _():
        carry_ref[...] = jnp.zeros_like(carry_ref)

    lf = _log_sigmoid(_dot(h, wf_ref[...]) + bf_ref[...]) * LOG2E
    rr = lax.broadcasted_iota(jnp.int32, (tm, tm), 0)
    cc = lax.broadcasted_iota(jnp.int32, (tm, tm), 1)
    tri = jnp.where(cc <= rr, 1.0, 0.0).astype(BF16)

    def split3(v):
        hi = v.astype(BF16).astype(F32)
        r1 = v - hi
        mid = r1.astype(BF16).astype(F32)
        lo = (r1 - mid).astype(BF16).astype(F32)
        return hi, mid, lo

    hi, mid, lo = split3(lf)
    cum = _dot(tri, hi.astype(BF16)) + _dot(tri, mid.astype(BF16)) + _dot(tri, lo.astype(BF16)) + carry_ref[...]
    carry_ref[...] = cum[tm - 1:tm, :]
    hi, mid, lo = split3(jnp.where(lane < N_HEADS, cum, 0.0))
    packed = hi + pltpu.roll(mid, N_HEADS, axis=1) + pltpu.roll(lo, 2 * N_HEADS, axis=1)
    feat = _dot(packed.astype(BF16), sel_ref[...]) + const_ref[...]

    def feat_q(n):
        return feat[:, n * LANES:(n + 1) * LANES]

    def feat_k(n):
        return feat[:, D_SPREAD + n * LANES:D_SPREAD + (n + 1) * LANES]

    zero = lambda n: 0.0
    one = lambda n: 1.0

    pa = _dot(h, wa_ref[...])
    us5_ref[0] = pa[:, 0:256]
    qf_ref[0] = spread(qknorm(pa[:, 256:512], 0), feat_q)
    kf_ref[0] = spread(qknorm(pa[:, 512:768], 1), feat_k)
    vf_ref[0] = spread(pa[:, 768:1024], one)
    pb = _dot(h, wb_ref[...])
    conv_ref[0] = pb[:, 0:512]
    qs_ref[0] = spread(qknorm(pb[:, 512:768], 2), zero)
    ks_ref[0] = spread(qknorm(pb[:, 768:1024], 3), zero)
    vs_ref[0] = pb[:, 1024:1280].astype(BF16)


def _inproj(x, g, wa, wb, wf, bfg, gains, sel, const):
    B, L, D = x.shape
    tm = min(TM_PROJ, L)
    c2 = lambda b, l: (0, 0)
    row_spec = lambda n: pl.BlockSpec((1, tm, n), lambda b, l: (b, l, 0))
    return pl.pallas_call(
        functools.partial(_inproj_kernel, tm=tm),
        grid=(B, L // tm),
        in_specs=[
            row_spec(D),
            pl.BlockSpec((1, D), c2),
            pl.BlockSpec(wa.shape, c2),
            pl.BlockSpec(wb.shape, c2),
            pl.BlockSpec((D, LANES), c2),
            pl.BlockSpec((1, LANES), c2),
            pl.BlockSpec((4, D_BRANCH), c2),
            pl.BlockSpec(sel.shape, c2),
            pl.BlockSpec(const.shape, c2),
        ],
        out_specs=[
            row_spec(D_BRANCH),
            row_spec(D_SPREAD), row_spec(D_SPREAD), row_spec(D_SPREAD),
            row_spec(2 * D_BRANCH),
            row_spec(D_SPREAD), row_spec(D_SPREAD), row_spec(D_BRANCH),
        ],
        out_shape=[
            jax.ShapeDtypeStruct((B, L, D_BRANCH), F32),
            jax.ShapeDtypeStruct((B, L, D_SPREAD), BF16),
            jax.ShapeDtypeStruct((B, L, D_SPREAD), BF16),
            jax.ShapeDtypeStruct((B, L, D_SPREAD), BF16),
            jax.ShapeDtypeStruct((B, L, 2 * D_BRANCH), F32),
            jax.ShapeDtypeStruct((B, L, D_SPREAD), BF16),
            jax.ShapeDtypeStruct((B, L, D_SPREAD), BF16),
            jax.ShapeDtypeStruct((B, L, D_BRANCH), BF16),
        ],
        scratch_shapes=[pltpu.VMEM((1, LANES), F32)],
        compiler_params=_cparams(("parallel", "arbitrary")),
        name="inproj",
    )(x, g, wa, wb, wf, bfg, gains, sel, const)


def _s5_kernel(u_ref, bmat_ref, cmat_ref, a_ref, d_ref, wglu_ref, y_ref, utb_ref, bu_ref, st_ref, *, lc, nb):
    ci = pl.program_id(0)

    @pl.when(ci == 0)
    def _():
        st_ref[...] = jnp.zeros_like(st_ref)

    n_half = D_BRANCH // LANES
    for b in range(nb):
        for j in range(n_half):
            utb_ref.at[j][pl.ds(b, lc, stride=nb), :] = u_ref[b, :, j * LANES:(j + 1) * LANES]
    u = jnp.concatenate([utb_ref[j] for j in range(n_half)], axis=1)
    bu_ref[...] = _dot(u.astype(BF16), bmat_ref[...])

    a_re = jnp.broadcast_to(a_ref[0:1, :], (nb, N_STATE))
    a_im = jnp.broadcast_to(a_ref[1:2, :], (nb, N_STATE))

    def step(t, carry):
        s_re, s_im = carry
        r0 = pl.multiple_of(t * nb, nb)
        b_re = bu_ref[pl.ds(r0, nb), 0:N_STATE]
        b_im = bu_ref[pl.ds(r0, nb), N_STATE:2 * N_STATE]
        n_re = a_re * s_re - a_im * s_im + b_re
        n_im = a_re * s_im + a_im * s_re + b_im
        bu_ref[pl.ds(r0, nb), 0:N_STATE] = n_re
        bu_ref[pl.ds(r0, nb), N_STATE:2 * N_STATE] = n_im
        return n_re, n_im

    s_re, s_im = lax.fori_loop(0, lc, step, (st_ref[:, 0:N_STATE], st_ref[:, N_STATE:2 * N_STATE]),
                               unroll=2)
    st_ref[:, 0:N_STATE] = s_re
    st_ref[:, N_STATE:2 * N_STATE] = s_im

    y = _dot(bu_ref[...].astype(BF16), cmat_ref[...]) + d_ref[...] * u
    y = jax.nn.gelu(y, approximate=True)
    y = y * _sigmoid(_dot(y.astype(BF16), wglu_ref[...]))
    for j in range(n_half):
        utb_ref[j] = y[:, j * LANES:(j + 1) * LANES]
    for b in range(nb):
        for j in range(n_half):
            y_ref[b, :, j * LANES:(j + 1) * LANES] = utb_ref.at[j][pl.ds(b, lc, stride=nb), :].astype(y_ref.dtype)


def _s5(u, bmat, cmat, a, d, wglu):
    nb, L, _ = u.shape
    lc = min(LC_S5, L)
    const = lambda c: (0, 0)
    return pl.pallas_call(
        functools.partial(_s5_kernel, lc=lc, nb=nb),
        grid=(L // lc,),
        in_specs=[
            pl.BlockSpec((nb, lc, D_BRANCH), lambda c: (0, c, 0)),
            pl.BlockSpec((D_BRANCH, 2 * N_STATE), const),
            pl.BlockSpec((2 * N_STATE, D_BRANCH), const),
            pl.BlockSpec((2, N_STATE), const),
            pl.BlockSpec((1, D_BRANCH), const),
            pl.BlockSpec((D_BRANCH, D_BRANCH), const),
        ],
        out_specs=pl.BlockSpec((nb, lc, D_BRANCH), lambda c: (0, c, 0)),
        out_shape=jax.ShapeDtypeStruct((nb, L, D_BRANCH), BF16),
        scratch_shapes=[pltpu.VMEM((D_BRANCH // LANES, lc * nb, LANES), F32), pltpu.VMEM((lc * nb, 2 * N_STATE), F32),
                        pltpu.VMEM((nb, 2 * N_STATE), F32)],
        compiler_params=_cparams(("arbitrary",)),
        name="s5",
    )(u, bmat, cmat, a, d, wglu)


def _fox_kernel(q_ref, k_ref, v_ref, o_ref, sa_ref, sb_ref, m_ref, acc_ref, *, tq):
    qi = pl.program_id(2)
    lane = lax.broadcasted_iota(jnp.int32, (tq, LANES), 1)

    def logits(ki, s_ref):
        k0 = pl.multiple_of(ki * tq, tq)
        for h in range(2):
            hs = slice(h * LANES, (h + 1) * LANES)
            s_ref[h] = _dot_nt(q_ref[0, :, hs], k_ref[0, pl.ds(k0, tq), hs])

    def update_rows(ki, s_ref, r0, nr, nc, masked):
        k0 = pl.multiple_of(ki * tq, tq)
        rs = slice(r0, r0 + nr)
        for h in range(2):
            hs = slice(h * LANES, (h + 1) * LANES)
            s = s_ref[h, rs, 0:nc]
            if masked:
                row = lax.broadcasted_iota(jnp.int32, (nr, nc), 0) + r0
                col = lax.broadcasted_iota(jnp.int32, (nr, nc), 1)
                s = jnp.where(col <= row, s, NEG_INF)
            m = m_ref[h, rs]
            m_new = jnp.maximum(m, jnp.max(s, axis=1, keepdims=True))
            p = jnp.exp2(s - _lane_tile(m_new, nc // LANES))
            acc_ref[h, rs] = (jnp.exp2(m - m_new) * acc_ref[h, rs]
                              + _dot(p.astype(BF16), v_ref[0, pl.ds(k0, nc), hs]))
            m_ref[h, rs] = m_new

    def update(ki, s_ref, diag):
        if diag:
            half = tq // 2
            update_rows(ki, s_ref, 0, half, half, True)
            update_rows(ki, s_ref, half, half, tq, True)
        else:
            update_rows(ki, s_ref, 0, tq, tq, False)

    m_ref[...] = jnp.full(m_ref.shape, NEG_INF, F32)
    acc_ref[...] = jnp.zeros_like(acc_ref)
    logits(0, sa_ref)

    def pair(j, _):
        logits(2 * j + 1, sb_ref)
        update(2 * j, sa_ref, False)
        logits(2 * j + 2, sa_ref)
        update(2 * j + 1, sb_ref, False)
        return 0

    lax.fori_loop(0, qi // 2, pair, 0)

    @pl.when(qi % 2 == 0)
    def _():
        update(qi, sa_ref, True)

    @pl.when(qi % 2 == 1)
    def _():
        logits(qi, sb_ref)
        update(qi - 1, sa_ref, False)
        update(qi, sb_ref, True)

    res = [acc_ref[h] / pltpu.roll(acc_ref[h], HEAD_DIM, axis=1) for h in range(2)]
    o_ref[0] = jnp.where(lane < HEAD_DIM, res[0], pltpu.roll(res[1], HEAD_DIM, axis=1)).astype(o_ref.dtype)


def _fox(q, k, v):
    B, L, _ = q.shape
    tq = min(T_FOX, L)
    return pl.pallas_call(
        functools.partial(_fox_kernel, tq=tq),
        grid=(B, 2, L // tq),
        in_specs=[
            pl.BlockSpec((1, tq, 2 * LANES), lambda b, p, i: (b, i, p)),
            pl.BlockSpec((1, L, 2 * LANES), lambda b, p, i: (b, 0, p)),
            pl.BlockSpec((1, L, 2 * LANES), lambda b, p, i: (b, 0, p)),
        ],
        out_specs=pl.BlockSpec((1, tq, LANES), lambda b, p, i: (b, i, p)),
        out_shape=jax.ShapeDtypeStruct((B, L, D_BRANCH), BF16),
        scratch_shapes=[pltpu.VMEM((2, tq, tq), F32), pltpu.VMEM((2, tq, tq), F32),
                        pltpu.VMEM((2, tq, LANES), F32), pltpu.VMEM((2, tq, LANES), F32)],
        compiler_params=_cparams(("parallel", "parallel", "arbitrary")),
        name="fox",
    )(q, k, v)


def _sb_kernel(q_ref, k_ref, v_ref, o_ref, za_ref, zb_ref, run_ref, acc_ref, *, tq, blk):
    qi = pl.program_id(2)
    lane = lax.broadcasted_iota(jnp.int32, (tq, LANES), 1)
    ur = lax.broadcasted_iota(jnp.int32, (blk, blk), 0)
    uc = lax.broadcasted_iota(jnp.int32, (blk, blk), 1)
    upper = jnp.where(ur > uc, 1.0, 0.0).astype(BF16)

    def logits(ki, z_ref):
        k0 = pl.multiple_of(jnp.maximum(ki, 0) * tq, tq)
        for h in range(2):
            hs = slice(h * LANES, (h + 1) * LANES)
            z_ref[h] = _dot_nt(q_ref[0, :, hs], k_ref[0, pl.ds(k0, tq), hs])

    def update_rows(ki, z_ref, r0, nr, nc, masked):
        k0 = pl.multiple_of(ki * tq, tq)
        rs = slice(r0, r0 + nr)
        vt = v_ref[0, pl.ds(k0, nc), :]
        if masked:
            past = (lax.broadcasted_iota(jnp.int32, (nr, nc), 1)
                    < lax.broadcasted_iota(jnp.int32, (nr, nc), 0) + r0)
        for h in range(2):
            run = run_ref[h, rs]
            z = z_ref[h, rs, 0:nc]
            neg_abs = pltpu.bitcast(pltpu.bitcast(z, jnp.uint32) | jnp.uint32(0x80000000), F32)
            l1p = jnp.log(1.0 + jnp.exp2(neg_abs)) * LOG2E
            log_beta = jnp.minimum(z, 0.0) - l1p
            log_keep = log_beta - z
            if masked:
                log_keep = jnp.where(past, log_keep, 0.0)
            keep16 = log_keep.astype(BF16)
            n_c = nc // blk
            after = [None] * n_c
            for c in reversed(range(n_c)):
                cs = slice(c * blk, (c + 1) * blk)
                raw = _dot(keep16[:, cs], upper)
                after[c] = raw + _lane_tile(run, blk // LANES)
                run = run + (raw[:, 0:1] + log_keep[:, c * blk:c * blk + 1])
            a = jnp.exp2(log_beta + _lane_cat(after))
            if masked:
                a = jnp.where(past, a, 0.0)
            acc_ref[h, rs] += _dot(a.astype(BF16), vt)
            run_ref[h, rs] = run

    def update(ki, z_ref, diag):
        if diag:
            half = tq // 2
            update_rows(ki, z_ref, 0, half, half, True)
            update_rows(ki, z_ref, half, half, tq, True)
        else:
            update_rows(ki, z_ref, 0, tq, tq, False)

    run_ref[...] = jnp.zeros_like(run_ref)
    acc_ref[...] = jnp.zeros_like(acc_ref)
    logits(qi, za_ref)
    logits(qi - 1, zb_ref)
    update(qi, za_ref, True)

    def alive():
        return jnp.max(run_ref[...]) > SB_RUN_FLOOR

    def pair(carry):
        j, _ = carry
        t = qi - 1 - 2 * j
        logits(t - 1, za_ref)
        update(t, zb_ref, False)

        @pl.when(alive())
        def _():
            logits(t - 2, zb_ref)
            update(t - 1, za_ref, False)

        return j + 1, alive()

    n_pairs = qi // 2
    _, live = lax.while_loop(lambda c: (c[0] < n_pairs) & c[1], pair, (jnp.int32(0), alive()))

    @pl.when((qi % 2 == 1) & live)
    def _():
        update(0, zb_ref, False)

    o_ref[0] = jnp.where(lane < HEAD_DIM, acc_ref[0], acc_ref[1]).astype(o_ref.dtype)


def _sb(q, k, v):
    B, L, _ = q.shape
    tq = min(T_ATT, L)
    return pl.pallas_call(
        functools.partial(_sb_kernel, tq=tq, blk=min(SB_BLOCK, tq // 2)),
        grid=(B, 2, L // tq),
        in_specs=[
            pl.BlockSpec((1, tq, 2 * LANES), lambda b, p, i: (b, i, p)),
            pl.BlockSpec((1, L, 2 * LANES), lambda b, p, i: (b, 0, p)),
            pl.BlockSpec((1, L, LANES), lambda b, p, i: (b, 0, p)),
        ],
        out_specs=pl.BlockSpec((1, tq, LANES), lambda b, p, i: (b, i, p)),
        out_shape=jax.ShapeDtypeStruct((B, L, D_BRANCH), BF16),
        scratch_shapes=[pltpu.VMEM((2, tq, tq), F32), pltpu.VMEM((2, tq, tq), F32),
                        pltpu.VMEM((2, tq, LANES), F32), pltpu.VMEM((2, tq, LANES), F32)],
        compiler_params=_cparams(("parallel", "parallel", "arbitrary")),
        name="stickbreak",
    )(q, k, v)


def _conv_kernel(ab_ref, w_ref, b_ref, g_ref, beta_ref, o_ref, pad_ref, sh_ref, *, lc):
    li = pl.program_id(1)

    @pl.when(li == 0)
    def _():
        pad_ref[0:CONV_HALO, :] = jnp.zeros((CONV_HALO, D_BRANCH), F32)

    ab = ab_ref[0]
    pad_ref[CONV_HALO:CONV_HALO + lc, :] = ab[:, 0:D_BRANCH] * _sigmoid(ab[:, D_BRANCH:2 * D_BRANCH])
    span = lc + CONV_HALO - SUBLANES
    for ph in range(1, SUBLANES):
        sh_ref[ph - 1] = pad_ref[ph:ph + span, :]
    off = CONV_HALO - (CONV_WIDTH - 1)
    acc = jnp.zeros((lc, D_BRANCH), F32) + b_ref[...]
    for j in range(CONV_WIDTH):
        ph, base = (off + j) % SUBLANES, (off + j) // SUBLANES * SUBLANES
        tap = pad_ref[base:base + lc, :] if ph == 0 else sh_ref[ph - 1, base:base + lc, :]
        acc = acc + w_ref[j:j + 1, :] * tap
    pad_ref[0:CONV_HALO, :] = pad_ref[lc:lc + CONV_HALO, :]
    mu = jnp.mean(acc, axis=-1, keepdims=True)
    xc = acc - mu
    var = jnp.mean(xc * xc, axis=-1, keepdims=True)
    y = xc * lax.rsqrt(var + EPS) * g_ref[...] + beta_ref[...]
    o_ref[0] = (y * _sigmoid(y)).astype(o_ref.dtype)


def _conv(ab, w, b, g, beta):
    B, L, _ = ab.shape
    lc = min(LC_CONV, L)
    const = lambda b_, l: (0, 0)
    return pl.pallas_call(
        functools.partial(_conv_kernel, lc=lc),
        grid=(B, L // lc),
        in_specs=[
            pl.BlockSpec((1, lc, 2 * D_BRANCH), lambda b_, l: (b_, l, 0)),
            pl.BlockSpec((CONV_WIDTH, D_BRANCH), const),
            pl.BlockSpec((1, D_BRANCH), const),
            pl.BlockSpec((1, D_BRANCH), const),
            pl.BlockSpec((1, D_BRANCH), const),
        ],
        out_specs=pl.BlockSpec((1, lc, D_BRANCH), lambda b_, l: (b_, l, 0)),
        out_shape=jax.ShapeDtypeStruct((B, L, D_BRANCH), BF16),
        scratch_shapes=[pltpu.VMEM((CONV_HALO + lc, D_BRANCH), F32),
                        pltpu.VMEM((SUBLANES - 1, CONV_HALO + lc - SUBLANES, D_BRANCH), F32)],
        compiler_params=_cparams(("parallel", "arbitrary")),
        name="conv",
    )(ab, w, b, g, beta)


def _merge_kernel(x_ref, g_ref, ys5_ref, yfox_ref, yconv_ref, ysb_ref, wg_ref, wb_ref, wo_ref, o_ref):
    x = x_ref[0]
    D = x.shape[-1]
    h = _rmsnorm(x, g_ref[...]).astype(BF16)
    ys = (ys5_ref[0], yfox_ref[0], yconv_ref[0], ysb_ref[0])
    merged = None
    for n in range(4):
        gate = _sigmoid(_dot(h, wg_ref[:, n * D:(n + 1) * D]))
        term = gate * _dot(ys[n], wb_ref[n])
        merged = term if merged is None else merged + term
    o_ref[0] = x + _dot(merged.astype(BF16), wo_ref[...])


def _merge(x, g, ys5, yfox, yconv, ysb, wg, wb, wo):
    B, L, D = x.shape
    tm = min(TM_MERGE, L)
    c2 = lambda b, l: (0, 0)
    yspec = pl.BlockSpec((1, tm, D_BRANCH), lambda b, l: (b, l, 0))
    return pl.pallas_call(
        _merge_kernel,
        grid=(B, L // tm),
        in_specs=[
            pl.BlockSpec((1, tm, D), lambda b, l: (b, l, 0)),
            pl.BlockSpec((1, D), c2),
            yspec, yspec, yspec, yspec,
            pl.BlockSpec((D, 4 * D), c2),
            pl.BlockSpec((4, D_BRANCH, D), lambda b, l: (0, 0, 0)),
            pl.BlockSpec((D, D), c2),
        ],
        out_specs=pl.BlockSpec((1, tm, D), lambda b, l: (b, l, 0)),
        out_shape=jax.ShapeDtypeStruct((B, L, D), F32),
        compiler_params=_cparams(("parallel", "parallel")),
        name="merge",
    )(x, g, ys5, yfox, yconv, ysb, wg, wb, wo)


def _swiglu(h, w1_ref, w3_ref, w2_ref, lead, fc):
    acc = None
    for c in range(w1_ref.shape[-1] // fc):
        cs = slice(c * fc, (c + 1) * fc)
        a = _dot(h, w1_ref[lead + (slice(None), cs)])
        b = _dot(h, w3_ref[lead + (slice(None), cs)])
        part = _dot((a * _sigmoid(a) * b).astype(BF16), w2_ref[lead + (cs, slice(None))])
        acc = part if acc is None else acc + part
    return acc


def _ffn_kernel(x_ref, g_ref, w1_ref, w3_ref, w2_ref, o_ref, *, fc):
    x = x_ref[...]
    h = _rmsnorm(x, g_ref[...]).astype(BF16)
    o_ref[...] = x + _swiglu(h, w1_ref, w3_ref, w2_ref, (), fc)


def _ffn(x2, g, w1, w3, w2):
    T, D = x2.shape
    F = w1.shape[1]
    tm = min(TM_FFN, T)
    c2 = lambda i: (0, 0)
    return pl.pallas_call(
        functools.partial(_ffn_kernel, fc=FC_FFN),
        grid=(T // tm,),
        in_specs=[
            pl.BlockSpec((tm, D), lambda i: (i, 0)),
            pl.BlockSpec((1, D), c2),
            pl.BlockSpec((D, F), c2), pl.BlockSpec((D, F), c2), pl.BlockSpec((F, D), c2),
        ],
        out_specs=pl.BlockSpec((tm, D), lambda i: (i, 0)),
        out_shape=jax.ShapeDtypeStruct((T, D), F32),
        compiler_params=pltpu.CompilerParams(dimension_semantics=("parallel",),
                                             vmem_limit_bytes=VMEM_LIMIT_FFN),
        name="dense_ffn",
    )(x2, g, w1, w3, w2)


def _router_kernel(x_ref, g_ref, wr_ref, br_ref, meta_ref, cnt_ref, carry_ref, *, tm):
    i = pl.program_id(0)

    @pl.when(i == 0)
    def _():
        carry_ref[...] = jnp.zeros_like(carry_ref)

    h = _rmsnorm(x_ref[...], g_ref[...])
    logits = jnp.dot(h, wr_ref[...], preferred_element_type=F32, precision=lax.Precision.HIGHEST) + br_ref[...]
    lane = lax.broadcasted_iota(jnp.int32, logits.shape, 1).astype(F32)
    m1 = jnp.max(logits, axis=1, keepdims=True)
    i1 = jnp.min(jnp.where(logits == m1, lane, float(LANES)), axis=1, keepdims=True)
    rest = jnp.where(lane == i1, NEG_INF, logits)
    m2 = jnp.max(rest, axis=1, keepdims=True)
    i2 = jnp.min(jnp.where(rest == m2, lane, float(LANES)), axis=1, keepdims=True)
    e2 = jnp.exp(m2 - m1)
    p1 = 1.0 / (1.0 + e2)
    p2 = e2 * p1
    onehot = jnp.where(lane == i1, 1.0, 0.0) + jnp.where(lane == i2, 1.0, 0.0)
    rr = lax.broadcasted_iota(jnp.int32, (tm, tm), 0)
    cc = lax.broadcasted_iota(jnp.int32, (tm, tm), 1)
    before = jnp.where(cc < rr, 1.0, 0.0).astype(BF16)
    prefix = _dot(before, onehot.astype(BF16)) + carry_ref[...]
    r1 = jnp.sum(jnp.where(lane == i1, prefix, 0.0), axis=1, keepdims=True)
    r2 = jnp.sum(jnp.where(lane == i2, prefix, 0.0), axis=1, keepdims=True)
    meta = jnp.zeros_like(logits)
    for n, col in enumerate((i1, i2, p1, p2, r1, r2)):
        meta = jnp.where(lane == float(n), col, meta)
    meta_ref[...] = meta
    total = carry_ref[...] + jnp.sum(onehot, axis=0, keepdims=True)
    carry_ref[...] = total
    cnt_ref[...] = total


def _router(x2, g, wr, br):
    T, D = x2.shape
    tm = min(TM_FFN, T)
    c2 = lambda i: (0, 0)
    return pl.pallas_call(
        functools.partial(_router_kernel, tm=tm),
        grid=(T // tm,),
        in_specs=[
            pl.BlockSpec((tm, D), lambda i: (i, 0)),
            pl.BlockSpec((1, D), c2),
            pl.BlockSpec((D, LANES), c2),
            pl.BlockSpec((1, LANES), c2),
        ],
        out_specs=[pl.BlockSpec((tm, LANES), lambda i: (i, 0)), pl.BlockSpec((1, LANES), c2)],
        out_shape=[jax.ShapeDtypeStruct((T, LANES), F32), jax.ShapeDtypeStruct((1, LANES), F32)],
        scratch_shapes=[pltpu.VMEM((1, LANES), F32)],
        compiler_params=_cparams(("arbitrary",)),
        name="router",
    )(x2, g, wr, br)


def _to_slabs(x):
    return x.reshape(x.shape[0], SUBLANES, x.shape[1] // SUBLANES)


def _from_slabs(x):
    return x.reshape(x.shape[0], x.shape[1] * x.shape[2])


def _dispatch_kernel(dst_ref, ztile_ref, x_ref, xs_hbm, xt_ref, zero_ref, sem_ref, zsem_ref, *, tt, tm):
    i = pl.program_id(0)

    @pl.when(i == 0)
    def _():
        zero_ref[...] = jnp.zeros_like(zero_ref)
        for e in range(2 * N_EXPERTS):
            @pl.when(ztile_ref[e] >= 0)
            def _():
                start = pl.multiple_of(ztile_ref[e], tm)
                fill = pltpu.make_async_copy(zero_ref, xs_hbm.at[pl.ds(start, tm)], zsem_ref)
                fill.start()
                fill.wait()

    xt_ref[...] = _to_slabs(x_ref[...])

    def body(r, _):
        for k in range(2):
            pltpu.make_async_copy(xt_ref.at[r], xs_hbm.at[dst_ref[2 * (i * tt + r) + k]],
                                  sem_ref).start(priority=k)
        return 0

    lax.fori_loop(0, tt, body, 0, unroll=4)
    for _ in range(2):
        pltpu.make_async_copy(xt_ref, xs_hbm.at[pl.ds(0, tt)], sem_ref).wait()


def _dispatch(x2, dst, ztile, R):
    T, D = x2.shape
    tt = min(TT_MOE, T)
    slab = (SUBLANES, D // SUBLANES)
    grid_spec = pltpu.PrefetchScalarGridSpec(
        num_scalar_prefetch=2,
        grid=(T // tt,),
        in_specs=[pl.BlockSpec((tt, D), lambda i, d, z: (i, 0))],
        out_specs=pl.BlockSpec(memory_space=pl.ANY),
        scratch_shapes=[pltpu.VMEM((tt,) + slab, F32), pltpu.VMEM((TM_MOE,) + slab, F32),
                        pltpu.SemaphoreType.DMA(()), pltpu.SemaphoreType.DMA(())],
    )
    return pl.pallas_call(
        functools.partial(_dispatch_kernel, tt=tt, tm=TM_MOE),
        grid_spec=grid_spec,
        out_shape=jax.ShapeDtypeStruct((R,) + slab, F32),
        compiler_params=_cparams(("arbitrary",)),
        name="dispatch",
    )(dst, ztile, x2)


def _experts_kernel(texp_ref, nused_ref, xs_ref, g_ref, w1_ref, w3_ref, w2_ref, y_ref, *, fc):
    i = pl.program_id(0)

    @pl.when(i < nused_ref[0])
    def _():
        h = _rmsnorm(_from_slabs(xs_ref[...]), g_ref[...]).astype(BF16)
        y_ref[...] = _to_slabs(_swiglu(h, w1_ref, w3_ref, w2_ref, (0,), fc))

    @pl.when(i >= nused_ref[0])
    def _():
        y_ref[...] = jnp.zeros_like(y_ref)


def _experts(xs, g, tile_expert, n_used, w1, w3, w2):
    R = xs.shape[0]
    slab = xs.shape[1:]
    D, F = w1.shape[1:]
    tm = TM_MOE
    wspec = lambda shape: pl.BlockSpec(shape, lambda i, te, n: (te[i], 0, 0))
    grid_spec = pltpu.PrefetchScalarGridSpec(
        num_scalar_prefetch=2,
        grid=(R // tm,),
        in_specs=[
            pl.BlockSpec((tm,) + slab, lambda i, te, n: (jnp.minimum(i, n[0] - 1), 0, 0)),
            pl.BlockSpec((1, D), lambda i, te, n: (0, 0)),
            wspec((1, D, F)), wspec((1, D, F)), wspec((1, F, D)),
        ],
        out_specs=pl.BlockSpec((tm,) + slab, lambda i, te, n: (i, 0, 0)),
    )
    return pl.pallas_call(
        functools.partial(_experts_kernel, fc=FC_FFN),
        grid_spec=grid_spec,
        out_shape=jax.ShapeDtypeStruct((R,) + slab, F32),
        compiler_params=pltpu.CompilerParams(dimension_semantics=("arbitrary",),
                                             vmem_limit_bytes=VMEM_LIMIT_FFN),
        name="experts",
    )(tile_expert, n_used, xs, g, w1, w3, w2)


def _combine_kernel(dst_ref, x_ref, p_ref, y_hbm, o_ref, buf_ref, sem_ref, *, tc, n_tiles):
    i = pl.program_id(0)
    slot = lax.rem(i, 2)

    def issue(tile, s):
        base = tile * (2 * tc)

        def body(r, _):
            for k in range(2):
                pltpu.make_async_copy(y_hbm.at[dst_ref[base + 2 * r + k]], buf_ref.at[s, k, r],
                                      sem_ref.at[s]).start(priority=k)
            return 0

        lax.fori_loop(0, tc, body, 0, unroll=4)

    @pl.when(i == 0)
    def _():
        issue(0, 0)

    @pl.when(i + 1 < n_tiles)
    def _():
        issue(i + 1, 1 - slot)

    for k in range(2):
        pltpu.make_async_copy(y_hbm.at[pl.ds(0, tc)], buf_ref.at[slot, k], sem_ref.at[slot]).wait()
    o_ref[...] = (x_ref[...] + p_ref[:, 0:1] * _from_slabs(buf_ref[slot, 0])
                  + p_ref[:, 1:2] * _from_slabs(buf_ref[slot, 1]))


def _combine(x2, prob, y, dst):
    T, D = x2.shape
    tc = min(TC_MOE, T)
    n_tiles = T // tc
    grid_spec = pltpu.PrefetchScalarGridSpec(
        num_scalar_prefetch=1,
        grid=(n_tiles,),
        in_specs=[pl.BlockSpec((tc, D), lambda i, d: (i, 0)), pl.BlockSpec((tc, 2), lambda i, d: (i, 0)),
                  pl.BlockSpec(memory_space=pl.ANY)],
        out_specs=pl.BlockSpec((tc, D), lambda i, d: (i, 0)),
        scratch_shapes=[pltpu.VMEM((2, 2, tc) + y.shape[1:], F32), pltpu.SemaphoreType.DMA((2,))],
    )
    return pl.pallas_call(
        functools.partial(_combine_kernel, tc=tc, n_tiles=n_tiles),
        grid_spec=grid_spec,
        out_shape=jax.ShapeDtypeStruct((T, D), F32),
        compiler_params=_cparams(("arbitrary",)),
        name="combine",
    )(dst, x2, prob, y)


def _moe(x2, g, wr, br, w1, w3, w2):
    T, D = x2.shape
    tm = TM_MOE
    meta, cnt = _router(x2, g, wr, br)
    expert = meta[:, 0:2].astype(jnp.int32)
    prob = meta[:, 2:4]
    rank = meta[:, 4:6].astype(jnp.int32)
    counts = cnt[0, :N_EXPERTS].astype(jnp.int32)
    padded = (counts + (tm - 1)) // tm * tm
    ends = jnp.cumsum(padded)
    dst = ((ends - padded)[expert] + rank).reshape(-1)
    R = 2 * T + N_EXPERTS * tm
    tile_start = jnp.arange(R // tm, dtype=jnp.int32) * tm
    tile_expert = jnp.minimum(jnp.searchsorted(ends, tile_start, side="right"), N_EXPERTS - 1).astype(jnp.int32)
    n_used = (ends[-1] // tm).astype(jnp.int32).reshape(1)
    last_tile = jnp.where(padded > 0, ends - tm, -1)
    tail = ends[-1] + jnp.arange(N_EXPERTS, dtype=jnp.int32) * tm
    ztile = jnp.concatenate([last_tile, jnp.where(tail < R, tail, -1)]).astype(jnp.int32)
    xs = _dispatch(x2, dst, ztile, R)
    y = _experts(xs, g, tile_expert, n_used, w1, w3, w2)
    return _combine(x2, prob, y, dst)


def _s5_params(lam_re, lam_im, log_dt, b_re, b_im, c_re, c_im):
    G, P, H = S5_GROUPS, S5_STATE, S5_GROUP
    dt = jnp.exp(log_dt)[:, None]
    mag = jnp.exp(lam_re * dt)
    ab_re = mag * jnp.cos(lam_im * dt)
    ab_im = mag * jnp.sin(lam_im * dt)
    nr, ni = ab_re - 1.0, ab_im
    den = lam_re * lam_re + lam_im * lam_im
    k_re = (nr * lam_re + ni * lam_im) / den
    k_im = (ni * lam_re - nr * lam_im) / den
    bb_re = k_re[..., None] * b_re - k_im[..., None] * b_im
    bb_im = k_re[..., None] * b_im + k_im[..., None] * b_re
    eye = jnp.eye(G, dtype=F32)
    bm_re = jnp.einsum("gph,gk->ghkp", bb_re, eye).reshape(G * H, G * P)
    bm_im = jnp.einsum("gph,gk->ghkp", bb_im, eye).reshape(G * H, G * P)
    bmat = jnp.concatenate([bm_re, bm_im], axis=1).astype(BF16)
    cm_re = jnp.einsum("ghp,gk->kpgh", c_re, eye).reshape(G * P, G * H)
    cm_im = jnp.einsum("ghp,gk->kpgh", c_im, eye).reshape(G * P, G * H)
    cmat = jnp.concatenate([cm_re, -cm_im], axis=0).astype(BF16)
    a = jnp.stack([ab_re.reshape(-1), ab_im.reshape(-1)], axis=0)
    return bmat, cmat, a


def kernel(x, g_mix, w_in, b_forget, fox_q_gain, fox_k_gain, sb_q_gain, sb_k_gain, s5_lam_re, s5_lam_im, s5_log_dt, s5_b_re, s5_b_im, s5_c_re, s5_c_im, s5_d, s5_w_glu, conv_w, conv_b, conv_ln_g, conv_ln_b, w_branch, w_out, g_ffn, ffn_w1, ffn_w3, ffn_w2, router_w, router_b, moe_w1, moe_w3, moe_w2):
    B, L, D = x.shape
    depth = g_mix.shape[0]
    assert B == SUBLANES, "the S5 scan lays the batch along the sublanes"
    n_qkv = 3 * D_BRANCH
    o_f = D_BRANCH + n_qkv
    o_c = o_f + N_HEADS
    o_g = o_c + 2 * D_BRANCH + n_qkv
    scale = 1.0 / math.sqrt(HEAD_DIM)
    sel_np, const_np = _forget_feature_tables()
    sel = jnp.asarray(sel_np, BF16)
    const = jnp.asarray(const_np, F32)

    w_a = w_in[:, :, :o_f].astype(BF16)
    w_b = w_in[:, :, o_c:o_g].astype(BF16)
    w_f = jnp.pad(w_in[:, :, o_f:o_c], ((0, 0), (0, 0), (0, LANES - N_HEADS))).astype(BF16)
    w_g = w_in[:, :, o_g:].astype(BF16)

    for i in range(depth):
        bfg = jnp.pad(b_forget[i], (0, LANES - N_HEADS)).reshape(1, LANES)
        gains = jnp.stack([jnp.tile(fox_q_gain[i] * (scale * LOG2E), N_HEADS), jnp.tile(fox_k_gain[i], N_HEADS),
                           jnp.tile(sb_q_gain[i] * (scale * LOG2E), N_HEADS), jnp.tile(sb_k_gain[i], N_HEADS)],
                          axis=0)
        us5, qf, kf, vf, conv_ab, qs, ks, vs = _inproj(x, g_mix[i].reshape(1, D), w_a[i], w_b[i], w_f[i], bfg,
                                                       gains, sel, const)

        bmat, cmat, a = _s5_params(s5_lam_re[i], s5_lam_im[i], s5_log_dt[i], s5_b_re[i], s5_b_im[i],
                                   s5_c_re[i], s5_c_im[i])
        ys5 = _s5(us5, bmat, cmat, a, s5_d[i].reshape(1, D_BRANCH), s5_w_glu[i].astype(BF16))
        yfox = _fox(qf, kf, vf)
        yconv = _conv(conv_ab, conv_w[i], conv_b[i].reshape(1, -1), conv_ln_g[i].reshape(1, -1),
                      conv_ln_b[i].reshape(1, -1))
        ysb = _sb(qs, ks, vs)

        x = _merge(x, g_mix[i].reshape(1, D), ys5, yfox, yconv, ysb, w_g[i],
                   w_branch[i].astype(BF16), w_out[i].astype(BF16))

        j = i // 2
        gf = g_ffn[i].reshape(1, D)
        if i % 2 == 0:
            x2 = _ffn(x.reshape(B * L, D), gf, ffn_w1[j].astype(BF16), ffn_w3[j].astype(BF16),
                      ffn_w2[j].astype(BF16))
        else:
            wr = jnp.pad(router_w[j], ((0, 0), (0, LANES - N_EXPERTS)))
            br = jnp.pad(router_b[j], (0, LANES - N_EXPERTS), constant_values=-1e30).reshape(1, LANES)
            x2 = _moe(x.reshape(B * L, D), gf, wr, br, moe_w1[j].astype(BF16), moe_w3[j].astype(BF16),
                      moe_w2[j].astype(BF16))
        x = x2.reshape(B, L, D)
    return x
```

```python
import functools
import math

import numpy as np

import jax
import jax.numpy as jnp
from jax import lax
from jax.experimental import pallas as pl
from jax.experimental.pallas import tpu as pltpu

F32 = jnp.float32
BF16 = jnp.bfloat16
EPS = 1e-6
LOG2E = math.log2(math.e)

D_BRANCH = 256
HEAD_DIM = 64
N_HEADS = D_BRANCH // HEAD_DIM
S5_GROUP = 16
S5_GROUPS = D_BRANCH // S5_GROUP
S5_STATE = 64
N_STATE = S5_GROUPS * S5_STATE
CONV_WIDTH = 31
CONV_HALO = 32
N_EXPERTS = 8
LANES = 128
SUBLANES = 8
VMEM_LIMIT = 56 * 1024 * 1024
D_SPREAD = N_HEADS * LANES
N_SPLIT = 3

TM_PROJ = 512
TM_MERGE = 512
TM_FFN = 512
T_ATT = 512
T_FOX = 1024
SB_BLOCK = 256
SB_RUN_FLOOR = -160.0
LC_S5 = 128
LC_CONV = 512
TM_MOE = 512
FC_FFN = 256
TT_MOE = 1024
TC_MOE = 256
VMEM_LIMIT_FFN = 60 * 1024 * 1024

NEG_INF = float("-inf")


def _cparams(sem):
    return pltpu.CompilerParams(dimension_semantics=sem, vmem_limit_bytes=VMEM_LIMIT)


def _dot(a, b):
    return jnp.dot(a, b, preferred_element_type=F32)


def _dot_nt(a, b):
    return lax.dot_general(a, b, (((1,), (1,)), ((), ())), preferred_element_type=F32)


def _split2(x):
    hi = x.astype(BF16)
    lo = (x - hi.astype(F32)).astype(BF16)
    return hi, lo


def _rmsnorm(x, g):
    ms = jnp.mean(x * x, axis=-1, keepdims=True)
    return x * lax.rsqrt(ms + EPS) * g


def _sigmoid(x):
    return 1.0 / (1.0 + jnp.exp(-x))


def _log_sigmoid(x):
    return jnp.minimum(x, 0.0) - jnp.log(1.0 + jnp.exp(-jnp.abs(x)))


def _lane_cat(parts):
    return parts[0] if len(parts) == 1 else jnp.concatenate(parts, axis=1)


def _lane_tile(x, n):
    return _lane_cat([x] * n)


def _forget_feature_tables():
    sel = np.zeros((LANES, 2 * D_SPREAD), np.float32)
    const = np.zeros((1, 2 * D_SPREAD), np.float32)
    for h in range(N_HEADS):
        for j in range(N_SPLIT):
            sel[j * N_HEADS + h, h * LANES + HEAD_DIM + j] = 1.0
            sel[j * N_HEADS + h, D_SPREAD + h * LANES + HEAD_DIM + N_SPLIT + j] = -1.0
            const[0, h * LANES + HEAD_DIM + N_SPLIT + j] = 1.0
            const[0, D_SPREAD + h * LANES + HEAD_DIM + j] = 1.0
    return sel, const


def _inproj_kernel(x_ref, g_ref, wa_ref, wb_ref, wf_ref, bf_ref, gain_ref, sel_ref, const_ref,
                   us5_ref, qf_ref, kf_ref, vf_ref, conv_ref, qs_ref, ks_ref, vs_ref, carry_ref, *, tm):
    li = pl.program_id(1)
    x = x_ref[0]
    h = _rmsnorm(x, g_ref[...]).astype(BF16)
    lane = lax.broadcasted_iota(jnp.int32, (tm, LANES), 1)

    r = lax.broadcasted_iota(jnp.int32, (D_BRANCH, D_BRANCH), 0) // HEAD_DIM
    c = lax.broadcasted_iota(jnp.int32, (D_BRANCH, D_BRANCH), 1) // HEAD_DIM
    ones_bd = jnp.where(r == c, 1.0, 0.0).astype(BF16)

    def qknorm(t, gi):
        ss = _dot((t * t).astype(BF16), ones_bd)
        return t * lax.rsqrt(ss * (1.0 / HEAD_DIM) + EPS) * gain_ref[gi:gi + 1, :]

    def spread(t, fill):
        blocks = []
        for hp in range(2):
            pair = t[:, hp * LANES:(hp + 1) * LANES]
            blocks += [pair, pltpu.roll(pair, HEAD_DIM, axis=1)]
        out = [jnp.where(lane < HEAD_DIM, blocks[n], fill(n)) for n in range(N_HEADS)]
        return jnp.concatenate(out, axis=1).astype(BF16)

    @pl.when(li == 0)
    def _():
        carry_ref[...] = jnp.zeros_like(carry_ref)

    lf = _log_sigmoid(_dot(h, wf_ref[...]) + bf_ref[...]) * LOG2E
    rr = lax.broadcasted_iota(jnp.int32, (tm, tm), 0)
    cc = lax.broadcasted_iota(jnp.int32, (tm, tm), 1)
    tri = jnp.where(cc <= rr, 1.0, 0.0).astype(BF16)

    def split3(v):
        hi = v.astype(BF16).astype(F32)
        r1 = v - hi
        mid = r1.astype(BF16).astype(F32)
        lo = (r1 - mid).astype(BF16).astype(F32)
        return hi, mid, lo

    hi, mid, lo = split3(lf)
    cum = _dot(tri, hi.astype(BF16)) + _dot(tri, mid.astype(BF16)) + _dot(tri, lo.astype(BF16)) + carry_ref[...]
    carry_ref[...] = cum[tm - 1:tm, :]
    hi, mid, lo = split3(jnp.where(lane < N_HEADS, cum, 0.0))
    packed = hi + pltpu.roll(mid, N_HEADS, axis=1) + pltpu.roll(lo, 2 * N_HEADS, axis=1)
    feat = _dot(packed.astype(BF16), sel_ref[...]) + const_ref[...]

    def feat_q(n):
        return feat[:, n * LANES:(n + 1) * LANES]

    def feat_k(n):
        return feat[:, D_SPREAD + n * LANES:D_SPREAD + (n + 1) * LANES]

    zero = lambda n: 0.0
    one = lambda n: 1.0

    pa = _dot(h, wa_ref[...])
    us5_ref[0] = pa[:, 0:256]
    qf_ref[0] = spread(qknorm(pa[:, 256:512], 0), feat_q)
    kf_ref[0] = spread(qknorm(pa[:, 512:768], 1), feat_k)
    vf_ref[0] = spread(pa[:, 768:1024], one)
    pb = _dot(h, wb_ref[...])
    conv_ref[0] = pb[:, 0:512]
    qs_ref[0] = spread(qknorm(pb[:, 512:768], 2), zero)
    ks_ref[0] = spread(qknorm(pb[:, 768:1024], 3), zero)
    vs_ref[0] = pb[:, 1024:1280].astype(BF16)


def _inproj(x, g, wa, wb, wf, bfg, gains, sel, const):
    B, L, D = x.shape
    tm = min(TM_PROJ, L)
    c2 = lambda b, l: (0, 0)
    row_spec = lambda n: pl.BlockSpec((1, tm, n), lambda b, l: (b, l, 0))
    return pl.pallas_call(
        functools.partial(_inproj_kernel, tm=tm),
        grid=(B, L // tm),
        in_specs=[
            row_spec(D),
            pl.BlockSpec((1, D), c2),
            pl.BlockSpec(wa.shape, c2),
            pl.BlockSpec(wb.shape, c2),
            pl.BlockSpec((D, LANES), c2),
            pl.BlockSpec((1, LANES), c2),
            pl.BlockSpec((4, D_BRANCH), c2),
            pl.BlockSpec(sel.shape, c2),
            pl.BlockSpec(const.shape, c2),
        ],
        out_specs=[
            row_spec(D_BRANCH),
            row_spec(D_SPREAD), row_spec(D_SPREAD), row_spec(D_SPREAD),
            row_spec(2 * D_BRANCH),
            row_spec(D_SPREAD), row_spec(D_SPREAD), row_spec(D_BRANCH),
        ],
        out_shape=[
            jax.ShapeDtypeStruct((B, L, D_BRANCH), F32),
            jax.ShapeDtypeStruct((B, L, D_SPREAD), BF16),
            jax.ShapeDtypeStruct((B, L, D_SPREAD), BF16),
            jax.ShapeDtypeStruct((B, L, D_SPREAD), BF16),
            jax.ShapeDtypeStruct((B, L, 2 * D_BRANCH), F32),
            jax.ShapeDtypeStruct((B, L, D_SPREAD), BF16),
            jax.ShapeDtypeStruct((B, L, D_SPREAD), BF16),
            jax.ShapeDtypeStruct((B, L, D_BRANCH), BF16),
        ],
        scratch_shapes=[pltpu.VMEM((1, LANES), F32)],
        compiler_params=_cparams(("parallel", "arbitrary")),
        name="inproj",
    )(x, g, wa, wb, wf, bfg, gains, sel, const)


def _s5_kernel(u_ref, bmat_ref, cmat_ref, a_ref, d_ref, wglu_ref, y_ref, utb_ref, bu_ref, st_ref, *, lc, nb):
    ci = pl.program_id(0)

    @pl.when(ci == 0)
    def _():
        st_ref[...] = jnp.zeros_like(st_ref)

    n_half = D_BRANCH // LANES
    for b in range(nb):
        for j in range(n_half):
            utb_ref.at[j][pl.ds(b, lc, stride=nb), :] = u_ref[b, :, j * LANES:(j + 1) * LANES]
    u = jnp.concatenate([utb_ref[j] for j in range(n_half)], axis=1)
    bu_ref[...] = _dot(u.astype(BF16), bmat_ref[...])

    a_re = jnp.broadcast_to(a_ref[0:1, :], (nb, N_STATE))
    a_im = jnp.broadcast_to(a_ref[1:2, :], (nb, N_STATE))

    def step(t, carry):
        s_re, s_im = carry
        r0 = pl.multiple_of(t * nb, nb)
        b_re = bu_ref[pl.ds(r0, nb), 0:N_STATE]
        b_im = bu_ref[pl.ds(r0, nb), N_STATE:2 * N_STATE]
        n_re = a_re * s_re - a_im * s_im + b_re
        n_im = a_re * s_im + a_im * s_re + b_im
        bu_ref[pl.ds(r0, nb), 0:N_STATE] = n_re
        bu_ref[pl.ds(r0, nb), N_STATE:2 * N_STATE] = n_im
        return n_re, n_im

    s_re, s_im = lax.fori_loop(0, lc, step, (st_ref[:, 0:N_STATE], st_ref[:, N_STATE:2 * N_STATE]),
                               unroll=2)
    st_ref[:, 0:N_STATE] = s_re
    st_ref[:, N_STATE:2 * N_STATE] = s_im

    y = _dot(bu_ref[...].astype(BF16), cmat_ref[...]) + d_ref[...] * u
    y = jax.nn.gelu(y, approximate=True)
    y = y * _sigmoid(_dot(y.astype(BF16), wglu_ref[...]))
    for j in range(n_half):
        utb_ref[j] = y[:, j * LANES:(j + 1) * LANES]
    for b in range(nb):
        for j in range(n_half):
            y_ref[b, :, j * LANES:(j + 1) * LANES] = utb_ref.at[j][pl.ds(b, lc, stride=nb), :].astype(y_ref.dtype)


def _s5(u, bmat, cmat, a, d, wglu):
    nb, L, _ = u.shape
    lc = min(LC_S5, L)
    const = lambda c: (0, 0)
    return pl.pallas_call(
        functools.partial(_s5_kernel, lc=lc, nb=nb),
        grid=(L // lc,),
        in_specs=[
            pl.BlockSpec((nb, lc, D_BRANCH), lambda c: (0, c, 0)),
            pl.BlockSpec((D_BRANCH, 2 * N_STATE), const),
            pl.BlockSpec((2 * N_STATE, D_BRANCH), const),
            pl.BlockSpec((2, N_STATE), const),
            pl.BlockSpec((1, D_BRANCH), const),
            pl.BlockSpec((D_BRANCH, D_BRANCH), const),
        ],
        out_specs=pl.BlockSpec((nb, lc, D_BRANCH), lambda c: (0, c, 0)),
        out_shape=jax.ShapeDtypeStruct((nb, L, D_BRANCH), BF16),
        scratch_shapes=[pltpu.VMEM((D_BRANCH // LANES, lc * nb, LANES), F32), pltpu.VMEM((lc * nb, 2 * N_STATE), F32),
                        pltpu.VMEM((nb, 2 * N_STATE), F32)],
        compiler_params=_cparams(("arbitrary",)),
        name="s5",
    )(u, bmat, cmat, a, d, wglu)


def _fox_kernel(q_ref, k_ref, v_ref, o_ref, sa_ref, sb_ref, m_ref, acc_ref, *, tq):
    qi = pl.program_id(2)
    lane = lax.broadcasted_iota(jnp.int32, (tq, LANES), 1)

    def logits(ki, s_ref):
        k0 = pl.multiple_of(ki * tq, tq)
        for h in range(2):
            hs = slice(h * LANES, (h + 1) * LANES)
            s_ref[h] = _dot_nt(q_ref[0, :, hs], k_ref[0, pl.ds(k0, tq), hs])

    def update_rows(ki, s_ref, r0, nr, nc, masked):
        k0 = pl.multiple_of(ki * tq, tq)
        rs = slice(r0, r0 + nr)
        for h in range(2):
            hs = slice(h * LANES, (h + 1) * LANES)
            s = s_ref[h, rs, 0:nc]
            if masked:
                row = lax.broadcasted_iota(jnp.int32, (nr, nc), 0) + r0
                col = lax.broadcasted_iota(jnp.int32, (nr, nc), 1)
                s = jnp.where(col <= row, s, NEG_INF)
            m = m_ref[h, rs]
            m_new = jnp.maximum(m, jnp.max(s, axis=1, keepdims=True))
            p = jnp.exp2(s - _lane_tile(m_new, nc // LANES))
            acc_ref[h, rs] = (jnp.exp2(m - m_new) * acc_ref[h, rs]
                              + _dot(p.astype(BF16), v_ref[0, pl.ds(k0, nc), hs]))
            m_ref[h, rs] = m_new

    def update(ki, s_ref, diag):
        if diag:
            half = tq // 2
            update_rows(ki, s_ref, 0, half, half, True)
            update_rows(ki, s_ref, half, half, tq, True)
        else:
            update_rows(ki, s_ref, 0, tq, tq, False)

    m_ref[...] = jnp.full(m_ref.shape, NEG_INF, F32)
    acc_ref[...] = jnp.zeros_like(acc_ref)
    logits(0, sa_ref)

    def pair(j, _):
        logits(2 * j + 1, sb_ref)
        update(2 * j, sa_ref, False)
        logits(2 * j + 2, sa_ref)
        update(2 * j + 1, sb_ref, False)
        return 0

    lax.fori_loop(0, qi // 2, pair, 0)

    @pl.when(qi % 2 == 0)
    def _():
        update(qi, sa_ref, True)

    @pl.when(qi % 2 == 1)
    def _():
        logits(qi, sb_ref)
        update(qi - 1, sa_ref, False)
        update(qi, sb_ref, True)

    res = [acc_ref[h] / pltpu.roll(acc_ref[h], HEAD_DIM, axis=1) for h in range(2)]
    o_ref[0] = jnp.where(lane < HEAD_DIM, res[0], pltpu.roll(res[1], HEAD_DIM, axis=1)).astype(o_ref.dtype)


def _fox(q, k, v):
    B, L, _ = q.shape
    tq = min(T_FOX, L)
    return pl.pallas_call(
        functools.partial(_fox_kernel, tq=tq),
        grid=(B, 2, L // tq),
        in_specs=[
            pl.BlockSpec((1, tq, 2 * LANES), lambda b, p, i: (b, i, p)),
            pl.BlockSpec((1, L, 2 * LANES), lambda b, p, i: (b, 0, p)),
            pl.BlockSpec((1, L, 2 * LANES), lambda b, p, i: (b, 0, p)),
        ],
        out_specs=pl.BlockSpec((1, tq, LANES), lambda b, p, i: (b, i, p)),
        out_shape=jax.ShapeDtypeStruct((B, L, D_BRANCH), BF16),
        scratch_shapes=[pltpu.VMEM((2, tq, tq), F32), pltpu.VMEM((2, tq, tq), F32),
                        pltpu.VMEM((2, tq, LANES), F32), pltpu.VMEM((2, tq, LANES), F32)],
        compiler_params=_cparams(("parallel", "parallel", "arbitrary")),
        name="fox",
    )(q, k, v)


def _sb_kernel(q_ref, k_ref, v_ref, o_ref, za_ref, zb_ref, run_ref, acc_ref, *, tq, blk):
    qi = pl.program_id(2)
    lane = lax.broadcasted_iota(jnp.int32, (tq, LANES), 1)
    ur = lax.broadcasted_iota(jnp.int32, (blk, blk), 0)
    uc = lax.broadcasted_iota(jnp.int32, (blk, blk), 1)
    upper = jnp.where(ur > uc, 1.0, 0.0).astype(BF16)

    def logits(ki, z_ref):
        k0 = pl.multiple_of(jnp.maximum(ki, 0) * tq, tq)
        for h in range(2):
            hs = slice(h * LANES, (h + 1) * LANES)
            z_ref[h] = _dot_nt(q_ref[0, :, hs], k_ref[0, pl.ds(k0, tq), hs])

    def update_rows(ki, z_ref, r0, nr, c0, nc, masked):
        k0 = pl.multiple_of(ki * tq + c0, nc)
        rs = slice(r0, r0 + nr)
        vt = v_ref[0, pl.ds(k0, nc), :]
        if masked:
            past = (lax.broadcasted_iota(jnp.int32, (nr, nc), 1) + c0
                    < lax.broadcasted_iota(jnp.int32, (nr, nc), 0) + r0)
        for h in range(2):
            run = run_ref[h, rs]
            z = z_ref[h, rs, c0:c0 + nc]
            neg_abs = pltpu.bitcast(pltpu.bitcast(z, jnp.uint32) | jnp.uint32(0x80000000), F32)
            l1p = jnp.log(1.0 + jnp.exp2(neg_abs)) * LOG2E
            log_beta = jnp.minimum(z, 0.0) - l1p
            log_keep = log_beta - z
            if masked:
                log_keep = jnp.where(past, log_keep, 0.0)
            keep16 = log_keep.astype(BF16)
            n_c = nc // blk
            after = [None] * n_c
            for c in reversed(range(n_c)):
                cs = slice(c * blk, (c + 1) * blk)
                raw = _dot(keep16[:, cs], upper)
                after[c] = raw + _lane_tile(run, blk // LANES)
                run = run + (raw[:, 0:1] + log_keep[:, c * blk:c * blk + 1])
            a = jnp.exp2(log_beta + _lane_cat(after))
            if masked:
                a = jnp.where(past, a, 0.0)
            acc_ref[h, rs] += _dot(a.astype(BF16), vt)
            run_ref[h, rs] = run

    half = tq // 2

    def alive():
        return jnp.max(run_ref[...]) > SB_RUN_FLOOR

    def update_diag(ki, z_ref):
        update_rows(ki, z_ref, 0, half, 0, half, True)
        update_rows(ki, z_ref, half, half, 0, tq, True)

    def update_full(ki, z_ref):
        update_rows(ki, z_ref, 0, tq, half, half, False)

        @pl.when(alive())
        def _():
            update_rows(ki, z_ref, 0, tq, 0, half, False)

    run_ref[...] = jnp.zeros_like(run_ref)
    acc_ref[...] = jnp.zeros_like(acc_ref)
    logits(qi, za_ref)
    logits(qi - 1, zb_ref)
    update_diag(qi, za_ref)

    def pair(carry):
        j, _ = carry
        t = qi - 1 - 2 * j
        logits(t - 1, za_ref)
        update_full(t, zb_ref)

        @pl.when(alive())
        def _():
            logits(t - 2, zb_ref)
            update_full(t - 1, za_ref)

        return j + 1, alive()

    n_pairs = qi // 2
    _, live = lax.while_loop(lambda c: (c[0] < n_pairs) & c[1], pair, (jnp.int32(0), alive()))

    @pl.when((qi % 2 == 1) & live)
    def _():
        update_full(0, zb_ref)

    o_ref[0] = jnp.where(lane < HEAD_DIM, acc_ref[0], acc_ref[1]).astype(o_ref.dtype)


def _sb(q, k, v):
    B, L, _ = q.shape
    tq = min(T_ATT, L)
    return pl.pallas_call(
        functools.partial(_sb_kernel, tq=tq, blk=min(SB_BLOCK, tq // 2)),
        grid=(B, 2, L // tq),
        in_specs=[
            pl.BlockSpec((1, tq, 2 * LANES), lambda b, p, i: (b, i, p)),
            pl.BlockSpec((1, L, 2 * LANES), lambda b, p, i: (b, 0, p)),
            pl.BlockSpec((1, L, LANES), lambda b, p, i: (b, 0, p)),
        ],
        out_specs=pl.BlockSpec((1, tq, LANES), lambda b, p, i: (b, i, p)),
        out_shape=jax.ShapeDtypeStruct((B, L, D_BRANCH), BF16),
        scratch_shapes=[pltpu.VMEM((2, tq, tq), F32), pltpu.VMEM((2, tq, tq), F32),
                        pltpu.VMEM((2, tq, LANES), F32), pltpu.VMEM((2, tq, LANES), F32)],
        compiler_params=_cparams(("parallel", "parallel", "arbitrary")),
        name="stickbreak",
    )(q, k, v)


def _conv_kernel(ab_ref, w_ref, b_ref, g_ref, beta_ref, o_ref, pad_ref, sh_ref, *, lc):
    li = pl.program_id(1)

    @pl.when(li == 0)
    def _():
        pad_ref[0:CONV_HALO, :] = jnp.zeros((CONV_HALO, D_BRANCH), F32)

    ab = ab_ref[0]
    pad_ref[CONV_HALO:CONV_HALO + lc, :] = ab[:, 0:D_BRANCH] * _sigmoid(ab[:, D_BRANCH:2 * D_BRANCH])
    span = lc + CONV_HALO - SUBLANES
    for ph in range(1, SUBLANES):
        sh_ref[ph - 1] = pad_ref[ph:ph + span, :]
    off = CONV_HALO - (CONV_WIDTH - 1)
    acc = jnp.zeros((lc, D_BRANCH), F32) + b_ref[...]
    for j in range(CONV_WIDTH):
        ph, base = (off + j) % SUBLANES, (off + j) // SUBLANES * SUBLANES
        tap = pad_ref[base:base + lc, :] if ph == 0 else sh_ref[ph - 1, base:base + lc, :]
        acc = acc + w_ref[j:j + 1, :] * tap
    pad_ref[0:CONV_HALO, :] = pad_ref[lc:lc + CONV_HALO, :]
    mu = jnp.mean(acc, axis=-1, keepdims=True)
    xc = acc - mu
    var = jnp.mean(xc * xc, axis=-1, keepdims=True)
    y = xc * lax.rsqrt(var + EPS) * g_ref[...] + beta_ref[...]
    o_ref[0] = (y * _sigmoid(y)).astype(o_ref.dtype)


def _conv(ab, w, b, g, beta):
    B, L, _ = ab.shape
    lc = min(LC_CONV, L)
    const = lambda b_, l: (0, 0)
    return pl.pallas_call(
        functools.partial(_conv_kernel, lc=lc),
        grid=(B, L // lc),
        in_specs=[
            pl.BlockSpec((1, lc, 2 * D_BRANCH), lambda b_, l: (b_, l, 0)),
            pl.BlockSpec((CONV_WIDTH, D_BRANCH), const),
            pl.BlockSpec((1, D_BRANCH), const),
            pl.BlockSpec((1, D_BRANCH), const),
            pl.BlockSpec((1, D_BRANCH), const),
        ],
        out_specs=pl.BlockSpec((1, lc, D_BRANCH), lambda b_, l: (b_, l, 0)),
        out_shape=jax.ShapeDtypeStruct((B, L, D_BRANCH), BF16),
        scratch_shapes=[pltpu.VMEM((CONV_HALO + lc, D_BRANCH), F32),
                        pltpu.VMEM((SUBLANES - 1, CONV_HALO + lc - SUBLANES, D_BRANCH), F32)],
        compiler_params=_cparams(("parallel", "arbitrary")),
        name="conv",
    )(ab, w, b, g, beta)


def _merge_kernel(x_ref, g_ref, ys5_ref, yfox_ref, yconv_ref, ysb_ref, wg_ref, wb_ref, wo_ref, o_ref):
    x = x_ref[0]
    D = x.shape[-1]
    h = _rmsnorm(x, g_ref[...]).astype(BF16)
    ys = (ys5_ref[0], yfox_ref[0], yconv_ref[0], ysb_ref[0])
    merged = None
    for n in range(4):
        gate = _sigmoid(_dot(h, wg_ref[:, n * D:(n + 1) * D]))
        term = gate * _dot(ys[n], wb_ref[n])
        merged = term if merged is None else merged + term
    o_ref[0] = x + _dot(merged.astype(BF16), wo_ref[...])


def _merge(x, g, ys5, yfox, yconv, ysb, wg, wb, wo):
    B, L, D = x.shape
    tm = min(TM_MERGE, L)
    c2 = lambda b, l: (0, 0)
    yspec = pl.BlockSpec((1, tm, D_BRANCH), lambda b, l: (b, l, 0))
    return pl.pallas_call(
        _merge_kernel,
        grid=(B, L // tm),
        in_specs=[
            pl.BlockSpec((1, tm, D), lambda b, l: (b, l, 0)),
            pl.BlockSpec((1, D), c2),
            yspec, yspec, yspec, yspec,
            pl.BlockSpec((D, 4 * D), c2),
            pl.BlockSpec((4, D_BRANCH, D), lambda b, l: (0, 0, 0)),
            pl.BlockSpec((D, D), c2),
        ],
        out_specs=pl.BlockSpec((1, tm, D), lambda b, l: (b, l, 0)),
        out_shape=jax.ShapeDtypeStruct((B, L, D), F32),
        compiler_params=_cparams(("parallel", "parallel")),
        name="merge",
    )(x, g, ys5, yfox, yconv, ysb, wg, wb, wo)


def _swiglu(h, w1_ref, w3_ref, w2_ref, lead, fc):
    acc = None
    for c in range(w1_ref.shape[-1] // fc):
        cs = slice(c * fc, (c + 1) * fc)
        a = _dot(h, w1_ref[lead + (slice(None), cs)])
        b = _dot(h, w3_ref[lead + (slice(None), cs)])
        part = _dot((a * _sigmoid(a) * b).astype(BF16), w2_ref[lead + (cs, slice(None))])
        acc = part if acc is None else acc + part
    return acc


def _ffn_kernel(x_ref, g_ref, w1_ref, w3_ref, w2_ref, o_ref, *, fc):
    x = x_ref[...]
    h = _rmsnorm(x, g_ref[...]).astype(BF16)
    o_ref[...] = x + _swiglu(h, w1_ref, w3_ref, w2_ref, (), fc)


def _ffn(x2, g, w1, w3, w2):
    T, D = x2.shape
    F = w1.shape[1]
    tm = min(TM_FFN, T)
    c2 = lambda i: (0, 0)
    return pl.pallas_call(
        functools.partial(_ffn_kernel, fc=FC_FFN),
        grid=(T // tm,),
        in_specs=[
            pl.BlockSpec((tm, D), lambda i: (i, 0)),
            pl.BlockSpec((1, D), c2),
            pl.BlockSpec((D, F), c2), pl.BlockSpec((D, F), c2), pl.BlockSpec((F, D), c2),
        ],
        out_specs=pl.BlockSpec((tm, D), lambda i: (i, 0)),
        out_shape=jax.ShapeDtypeStruct((T, D), F32),
        compiler_params=pltpu.CompilerParams(dimension_semantics=("parallel",),
                                             vmem_limit_bytes=VMEM_LIMIT_FFN),
        name="dense_ffn",
    )(x2, g, w1, w3, w2)


def _router_kernel(x_ref, g_ref, wr_ref, br_ref, meta_ref, cnt_ref, carry_ref, *, tm):
    i = pl.program_id(0)

    @pl.when(i == 0)
    def _():
        carry_ref[...] = jnp.zeros_like(carry_ref)

    h = _rmsnorm(x_ref[...], g_ref[...])
    logits = jnp.dot(h, wr_ref[...], preferred_element_type=F32, precision=lax.Precision.HIGHEST) + br_ref[...]
    lane = lax.broadcasted_iota(jnp.int32, logits.shape, 1).astype(F32)
    m1 = jnp.max(logits, axis=1, keepdims=True)
    i1 = jnp.min(jnp.where(logits == m1, lane, float(LANES)), axis=1, keepdims=True)
    rest = jnp.where(lane == i1, NEG_INF, logits)
    m2 = jnp.max(rest, axis=1, keepdims=True)
    i2 = jnp.min(jnp.where(rest == m2, lane, float(LANES)), axis=1, keepdims=True)
    e2 = jnp.exp(m2 - m1)
    p1 = 1.0 / (1.0 + e2)
    p2 = e2 * p1
    onehot = jnp.where(lane == i1, 1.0, 0.0) + jnp.where(lane == i2, 1.0, 0.0)
    rr = lax.broadcasted_iota(jnp.int32, (tm, tm), 0)
    cc = lax.broadcasted_iota(jnp.int32, (tm, tm), 1)
    before = jnp.where(cc < rr, 1.0, 0.0).astype(BF16)
    prefix = _dot(before, onehot.astype(BF16)) + carry_ref[...]
    r1 = jnp.sum(jnp.where(lane == i1, prefix, 0.0), axis=1, keepdims=True)
    r2 = jnp.sum(jnp.where(lane == i2, prefix, 0.0), axis=1, keepdims=True)
    meta = jnp.zeros_like(logits)
    for n, col in enumerate((i1, i2, p1, p2, r1, r2)):
        meta = jnp.where(lane == float(n), col, meta)
    meta_ref[...] = meta
    total = carry_ref[...] + jnp.sum(onehot, axis=0, keepdims=True)
    carry_ref[...] = total
    cnt_ref[...] = total


def _router(x2, g, wr, br):
    T, D = x2.shape
    tm = min(TM_FFN, T)
    c2 = lambda i: (0, 0)
    return pl.pallas_call(
        functools.partial(_router_kernel, tm=tm),
        grid=(T // tm,),
        in_specs=[
            pl.BlockSpec((tm, D), lambda i: (i, 0)),
            pl.BlockSpec((1, D), c2),
            pl.BlockSpec((D, LANES), c2),
            pl.BlockSpec((1, LANES), c2),
        ],
        out_specs=[pl.BlockSpec((tm, LANES), lambda i: (i, 0)), pl.BlockSpec((1, LANES), c2)],
        out_shape=[jax.ShapeDtypeStruct((T, LANES), F32), jax.ShapeDtypeStruct((1, LANES), F32)],
        scratch_shapes=[pltpu.VMEM((1, LANES), F32)],
        compiler_params=_cparams(("arbitrary",)),
        name="router",
    )(x2, g, wr, br)


def _to_slabs(x):
    return x.reshape(x.shape[0], SUBLANES, x.shape[1] // SUBLANES)


def _from_slabs(x):
    return x.reshape(x.shape[0], x.shape[1] * x.shape[2])


def _dispatch_kernel(dst_ref, ztile_ref, x_ref, xs_hbm, xt_ref, zero_ref, sem_ref, zsem_ref, *, tt, tm):
    i = pl.program_id(0)

    @pl.when(i == 0)
    def _():
        zero_ref[...] = jnp.zeros_like(zero_ref)
        for e in range(2 * N_EXPERTS):
            @pl.when(ztile_ref[e] >= 0)
            def _():
                start = pl.multiple_of(ztile_ref[e], tm)
                fill = pltpu.make_async_copy(zero_ref, xs_hbm.at[pl.ds(start, tm)], zsem_ref)
                fill.start()
                fill.wait()

    xt_ref[...] = _to_slabs(x_ref[...])

    def body(r, _):
        for k in range(2):
            pltpu.make_async_copy(xt_ref.at[r], xs_hbm.at[dst_ref[2 * (i * tt + r) + k]],
                                  sem_ref).start(priority=k)
        return 0

    lax.fori_loop(0, tt, body, 0, unroll=4)
    for _ in range(2):
        pltpu.make_async_copy(xt_ref, xs_hbm.at[pl.ds(0, tt)], sem_ref).wait()


def _dispatch(x2, dst, ztile, R):
    T, D = x2.shape
    tt = min(TT_MOE, T)
    slab = (SUBLANES, D // SUBLANES)
    grid_spec = pltpu.PrefetchScalarGridSpec(
        num_scalar_prefetch=2,
        grid=(T // tt,),
        in_specs=[pl.BlockSpec((tt, D), lambda i, d, z: (i, 0))],
        out_specs=pl.BlockSpec(memory_space=pl.ANY),
        scratch_shapes=[pltpu.VMEM((tt,) + slab, F32), pltpu.VMEM((TM_MOE,) + slab, F32),
                        pltpu.SemaphoreType.DMA(()), pltpu.SemaphoreType.DMA(())],
    )
    return pl.pallas_call(
        functools.partial(_dispatch_kernel, tt=tt, tm=TM_MOE),
        grid_spec=grid_spec,
        out_shape=jax.ShapeDtypeStruct((R,) + slab, F32),
        compiler_params=_cparams(("arbitrary",)),
        name="dispatch",
    )(dst, ztile, x2)


def _experts_kernel(texp_ref, nused_ref, xs_ref, g_ref, w1_ref, w3_ref, w2_ref, y_ref, *, fc):
    i = pl.program_id(0)

    @pl.when(i < nused_ref[0])
    def _():
        h = _rmsnorm(_from_slabs(xs_ref[...]), g_ref[...]).astype(BF16)
        y_ref[...] = _to_slabs(_swiglu(h, w1_ref, w3_ref, w2_ref, (0,), fc))

    @pl.when(i >= nused_ref[0])
    def _():
        y_ref[...] = jnp.zeros_like(y_ref)


def _experts(xs, g, tile_expert, n_used, w1, w3, w2):
    R = xs.shape[0]
    slab = xs.shape[1:]
    D, F = w1.shape[1:]
    tm = TM_MOE
    wspec = lambda shape: pl.BlockSpec(shape, lambda i, te, n: (te[i], 0, 0))
    grid_spec = pltpu.PrefetchScalarGridSpec(
        num_scalar_prefetch=2,
        grid=(R // tm,),
        in_specs=[
            pl.BlockSpec((tm,) + slab, lambda i, te, n: (jnp.minimum(i, n[0] - 1), 0, 0)),
            pl.BlockSpec((1, D), lambda i, te, n: (0, 0)),
            wspec((1, D, F)), wspec((1, D, F)), wspec((1, F, D)),
        ],
        out_specs=pl.BlockSpec((tm,) + slab, lambda i, te, n: (i, 0, 0)),
    )
    return pl.pallas_call(
        functools.partial(_experts_kernel, fc=FC_FFN),
        grid_spec=grid_spec,
        out_shape=jax.ShapeDtypeStruct((R,) + slab, F32),
        compiler_params=pltpu.CompilerParams(dimension_semantics=("arbitrary",),
                                             vmem_limit_bytes=VMEM_LIMIT_FFN),
        name="experts",
    )(tile_expert, n_used, xs, g, w1, w3, w2)


def _combine_kernel(dst_ref, x_ref, p_ref, y_hbm, o_ref, buf_ref, sem_ref, *, tc, n_tiles):
    i = pl.program_id(0)
    slot = lax.rem(i, 2)

    def issue(tile, s):
        base = tile * (2 * tc)

        def body(r, _):
            for k in range(2):
                pltpu.make_async_copy(y_hbm.at[dst_ref[base + 2 * r + k]], buf_ref.at[s, k, r],
                                      sem_ref.at[s]).start(priority=k)
            return 0

        lax.fori_loop(0, tc, body, 0, unroll=4)

    @pl.when(i == 0)
    def _():
        issue(0, 0)

    @pl.when(i + 1 < n_tiles)
    def _():
        issue(i + 1, 1 - slot)

    for k in range(2):
        pltpu.make_async_copy(y_hbm.at[pl.ds(0, tc)], buf_ref.at[slot, k], sem_ref.at[slot]).wait()
    o_ref[...] = (x_ref[...] + p_ref[:, 0:1] * _from_slabs(buf_ref[slot, 0])
                  + p_ref[:, 1:2] * _from_slabs(buf_ref[slot, 1]))


def _combine(x2, prob, y, dst):
    T, D = x2.shape
    tc = min(TC_MOE, T)
    n_tiles = T // tc
    grid_spec = pltpu.PrefetchScalarGridSpec(
        num_scalar_prefetch=1,
        grid=(n_tiles,),
        in_specs=[pl.BlockSpec((tc, D), lambda i, d: (i, 0)), pl.BlockSpec((tc, 2), lambda i, d: (i, 0)),
                  pl.BlockSpec(memory_space=pl.ANY)],
        out_specs=pl.BlockSpec((tc, D), lambda i, d: (i, 0)),
        scratch_shapes=[pltpu.VMEM((2, 2, tc) + y.shape[1:], F32), pltpu.SemaphoreType.DMA((2,))],
    )
    return pl.pallas_call(
        functools.partial(_combine_kernel, tc=tc, n_tiles=n_tiles),
        grid_spec=grid_spec,
        out_shape=jax.ShapeDtypeStruct((T, D), F32),
        compiler_params=_cparams(("arbitrary",)),
        name="combine",
    )(dst, x2, prob, y)


def _moe(x2, g, wr, br, w1, w3, w2):
    T, D = x2.shape
    tm = TM_MOE
    meta, cnt = _router(x2, g, wr, br)
    expert = meta[:, 0:2].astype(jnp.int32)
    prob = meta[:, 2:4]
    rank = meta[:, 4:6].astype(jnp.int32)
    counts = cnt[0, :N_EXPERTS].astype(jnp.int32)
    padded = (counts + (tm - 1)) // tm * tm
    ends = jnp.cumsum(padded)
    dst = ((ends - padded)[expert] + rank).reshape(-1)
    R = 2 * T + N_EXPERTS * tm
    tile_start = jnp.arange(R // tm, dtype=jnp.int32) * tm
    tile_expert = jnp.minimum(jnp.searchsorted(ends, tile_start, side="right"), N_EXPERTS - 1).astype(jnp.int32)
    n_used = (ends[-1] // tm).astype(jnp.int32).reshape(1)
    last_tile = jnp.where(padded > 0, ends - tm, -1)
    tail = ends[-1] + jnp.arange(N_EXPERTS, dtype=jnp.int32) * tm
    ztile = jnp.concatenate([last_tile, jnp.where(tail < R, tail, -1)]).astype(jnp.int32)
    xs = _dispatch(x2, dst, ztile, R)
    y = _experts(xs, g, tile_expert, n_used, w1, w3, w2)
    return _combine(x2, prob, y, dst)


def _s5_params(lam_re, lam_im, log_dt, b_re, b_im, c_re, c_im):
    G, P, H = S5_GROUPS, S5_STATE, S5_GROUP
    dt = jnp.exp(log_dt)[:, None]
    mag = jnp.exp(lam_re * dt)
    ab_re = mag * jnp.cos(lam_im * dt)
    ab_im = mag * jnp.sin(lam_im * dt)
    nr, ni = ab_re - 1.0, ab_im
    den = lam_re * lam_re + lam_im * lam_im
    k_re = (nr * lam_re + ni * lam_im) / den
    k_im = (ni * lam_re - nr * lam_im) / den
    bb_re = k_re[..., None] * b_re - k_im[..., None] * b_im
    bb_im = k_re[..., None] * b_im + k_im[..., None] * b_re
    eye = jnp.eye(G, dtype=F32)
    bm_re = jnp.einsum("gph,gk->ghkp", bb_re, eye).reshape(G * H, G * P)
    bm_im = jnp.einsum("gph,gk->ghkp", bb_im, eye).reshape(G * H, G * P)
    bmat = jnp.concatenate([bm_re, bm_im], axis=1).astype(BF16)
    cm_re = jnp.einsum("ghp,gk->kpgh", c_re, eye).reshape(G * P, G * H)
    cm_im = jnp.einsum("ghp,gk->kpgh", c_im, eye).reshape(G * P, G * H)
    cmat = jnp.concatenate([cm_re, -cm_im], axis=0).astype(BF16)
    a = jnp.stack([ab_re.reshape(-1), ab_im.reshape(-1)], axis=0)
    return bmat, cmat, a


def kernel(x, g_mix, w_in, b_forget, fox_q_gain, fox_k_gain, sb_q_gain, sb_k_gain, s5_lam_re, s5_lam_im, s5_log_dt, s5_b_re, s5_b_im, s5_c_re, s5_c_im, s5_d, s5_w_glu, conv_w, conv_b, conv_ln_g, conv_ln_b, w_branch, w_out, g_ffn, ffn_w1, ffn_w3, ffn_w2, router_w, router_b, moe_w1, moe_w3, moe_w2):
    B, L, D = x.shape
    depth = g_mix.shape[0]
    assert B == SUBLANES, "the S5 scan lays the batch along the sublanes"
    n_qkv = 3 * D_BRANCH
    o_f = D_BRANCH + n_qkv
    o_c = o_f + N_HEADS
    o_g = o_c + 2 * D_BRANCH + n_qkv
    scale = 1.0 / math.sqrt(HEAD_DIM)
    sel_np, const_np = _forget_feature_tables()
    sel = jnp.asarray(sel_np, BF16)
    const = jnp.asarray(const_np, F32)

    w_a = w_in[:, :, :o_f].astype(BF16)
    w_b = w_in[:, :, o_c:o_g].astype(BF16)
    w_f = jnp.pad(w_in[:, :, o_f:o_c], ((0, 0), (0, 0), (0, LANES - N_HEADS))).astype(BF16)
    w_g = w_in[:, :, o_g:].astype(BF16)

    for i in range(depth):
        bfg = jnp.pad(b_forget[i], (0, LANES - N_HEADS)).reshape(1, LANES)
        gains = jnp.stack([jnp.tile(fox_q_gain[i] * (scale * LOG2E), N_HEADS), jnp.tile(fox_k_gain[i], N_HEADS),
                           jnp.tile(sb_q_gain[i] * (scale * LOG2E), N_HEADS), jnp.tile(sb_k_gain[i], N_HEADS)],
                          axis=0)
        us5, qf, kf, vf, conv_ab, qs, ks, vs = _inproj(x, g_mix[i].reshape(1, D), w_a[i], w_b[i], w_f[i], bfg,
                                                       gains, sel, const)

        bmat, cmat, a = _s5_params(s5_lam_re[i], s5_lam_im[i], s5_log_dt[i], s5_b_re[i], s5_b_im[i],
                                   s5_c_re[i], s5_c_im[i])
        ys5 = _s5(us5, bmat, cmat, a, s5_d[i].reshape(1, D_BRANCH), s5_w_glu[i].astype(BF16))
        yfox = _fox(qf, kf, vf)
        yconv = _conv(conv_ab, conv_w[i], conv_b[i].reshape(1, -1), conv_ln_g[i].reshape(1, -1),
                      conv_ln_b[i].reshape(1, -1))
        ysb = _sb(qs, ks, vs)

        x = _merge(x, g_mix[i].reshape(1, D), ys5, yfox, yconv, ysb, w_g[i],
                   w_branch[i].astype(BF16), w_out[i].astype(BF16))

        j = i // 2
        gf = g_ffn[i].reshape(1, D)
        if i % 2 == 0:
            x2 = _ffn(x.reshape(B * L, D), gf, ffn_w1[j].astype(BF16), ffn_w3[j].astype(BF16),
                      ffn_w2[j].astype(BF16))
        else:
            wr = jnp.pad(router_w[j], ((0, 0), (0, LANES - N_EXPERTS)))
            br = jnp.pad(router_b[j], (0, LANES - N_EXPERTS), constant_values=-1e30).reshape(1, LANES)
            x2 = _moe(x.reshape(B * L, D), gf, wr, br, moe_w1[j].astype(BF16), moe_w3[j].astype(BF16),
                      moe_w2[j].astype(BF16))
        x = x2.reshape(B, L, D)
    return x
```

```python
import functools
import math

import numpy as np

import jax
import jax.numpy as jnp
from jax import lax
from jax.experimental import pallas as pl
from jax.experimental.pallas import tpu as pltpu

F32 = jnp.float32
BF16 = jnp.bfloat16
EPS = 1e-6
LOG2E = math.log2(math.e)

D_BRANCH = 256
HEAD_DIM = 64
N_HEADS = D_BRANCH // HEAD_DIM
S5_GROUP = 16
S5_GROUPS = D_BRANCH // S5_GROUP
S5_STATE = 64
N_STATE = S5_GROUPS * S5_STATE
CONV_WIDTH = 31
CONV_HALO = 32
N_EXPERTS = 8
LANES = 128
SUBLANES = 8
VMEM_LIMIT = 56 * 1024 * 1024
D_SPREAD = N_HEADS * LANES
N_SPLIT = 3

TM_PROJ = 512
TM_MERGE = 512
TM_FFN = 512
T_ATT = 512
T_FOX = 1024
SB_BLOCK = 256
SB_RUN_FLOOR = -160.0
LC_S5 = 128
LC_CONV = 512
TM_MOE = 512
FC_FFN = 256
TT_MOE = 1024
TC_MOE = 256
VMEM_LIMIT_FFN = 60 * 1024 * 1024

NEG_INF = float("-inf")


def _cparams(sem):
    return pltpu.CompilerParams(dimension_semantics=sem, vmem_limit_bytes=VMEM_LIMIT)


def _dot(a, b):
    return jnp.dot(a, b, preferred_element_type=F32)


def _dot_nt(a, b):
    return lax.dot_general(a, b, (((1,), (1,)), ((), ())), preferred_element_type=F32)


def _split2(x):
    hi = x.astype(BF16)
    lo = (x - hi.astype(F32)).astype(BF16)
    return hi, lo


def _rmsnorm(x, g):
    ms = jnp.mean(x * x, axis=-1, keepdims=True)
    return x * lax.rsqrt(ms + EPS) * g


def _sigmoid(x):
    return 1.0 / (1.0 + jnp.exp(-x))


def _log_sigmoid(x):
    return jnp.minimum(x, 0.0) - jnp.log(1.0 + jnp.exp(-jnp.abs(x)))


def _lane_cat(parts):
    return parts[0] if len(parts) == 1 else jnp.concatenate(parts, axis=1)


def _lane_tile(x, n):
    return _lane_cat([x] * n)


def _forget_feature_tables():
    sel = np.zeros((LANES, 2 * D_SPREAD), np.float32)
    const = np.zeros((1, 2 * D_SPREAD), np.float32)
    for h in range(N_HEADS):
        for j in range(N_SPLIT):
            sel[j * N_HEADS + h, h * LANES + HEAD_DIM + j] = 1.0
            sel[j * N_HEADS + h, D_SPREAD + h * LANES + HEAD_DIM + N_SPLIT + j] = -1.0
            const[0, h * LANES + HEAD_DIM + N_SPLIT + j] = 1.0
            const[0, D_SPREAD + h * LANES + HEAD_DIM + j] = 1.0
    return sel, const


def _inproj_kernel(x_ref, g_ref, wa_ref, wb_ref, wf_ref, bf_ref, gain_ref, sel_ref, const_ref,
                   us5_ref, qf_ref, kf_ref, vf_ref, conv_ref, qs_ref, ks_ref, vs_ref, carry_ref, *, tm):
    li = pl.program_id(1)
    x = x_ref[0]
    h = _rmsnorm(x, g_ref[...]).astype(BF16)
    lane = lax.broadcasted_iota(jnp.int32, (tm, LANES), 1)

    r = lax.broadcasted_iota(jnp.int32, (D_BRANCH, D_BRANCH), 0) // HEAD_DIM
    c = lax.broadcasted_iota(jnp.int32, (D_BRANCH, D_BRANCH), 1) // HEAD_DIM
    ones_bd = jnp.where(r == c, 1.0, 0.0).astype(BF16)

    def qknorm(t, gi):
        ss = _dot((t * t).astype(BF16), ones_bd)
        return t * lax.rsqrt(ss * (1.0 / HEAD_DIM) + EPS) * gain_ref[gi:gi + 1, :]

    def spread(t, fill):
        blocks = []
        for hp in range(2):
            pair = t[:, hp * LANES:(hp + 1) * LANES]
            blocks += [pair, pltpu.roll(pair, HEAD_DIM, axis=1)]
        out = [jnp.where(lane < HEAD_DIM, blocks[n], fill(n)) for n in range(N_HEADS)]
        return jnp.concatenate(out, axis=1).astype(BF16)

    @pl.when(li == 0)
    def _():
        carry_ref[...] = jnp.zeros_like(carry_ref)

    lf = _log_sigmoid(_dot(h, wf_ref[...]) + bf_ref[...]) * LOG2E
    rr = lax.broadcasted_iota(jnp.int32, (tm, tm), 0)
    cc = lax.broadcasted_iota(jnp.int32, (tm, tm), 1)
    tri = jnp.where(cc <= rr, 1.0, 0.0).astype(BF16)

    def split3(v):
        hi = v.astype(BF16).astype(F32)
        r1 = v - hi
        mid = r1.astype(BF16).astype(F32)
        lo = (r1 - mid).astype(BF16).astype(F32)
        return hi, mid, lo

    hi, mid, lo = split3(lf)
    cum = _dot(tri, hi.astype(BF16)) + _dot(tri, mid.astype(BF16)) + _dot(tri, lo.astype(BF16)) + carry_ref[...]
    carry_ref[...] = cum[tm - 1:tm, :]
    hi, mid, lo = split3(jnp.where(lane < N_HEADS, cum, 0.0))
    packed = hi + pltpu.roll(mid, N_HEADS, axis=1) + pltpu.roll(lo, 2 * N_HEADS, axis=1)
    feat = _dot(packed.astype(BF16), sel_ref[...]) + const_ref[...]

    def feat_q(n):
        return feat[:, n * LANES:(n + 1) * LANES]

    def feat_k(n):
        return feat[:, D_SPREAD + n * LANES:D_SPREAD + (n + 1) * LANES]

    zero = lambda n: 0.0
    one = lambda n: 1.0

    pa = _dot(h, wa_ref[...])
    us5_ref[0] = pa[:, 0:256]
    qf_ref[0] = spread(qknorm(pa[:, 256:512], 0), feat_q)
    kf_ref[0] = spread(qknorm(pa[:, 512:768], 1), feat_k)
    vf_ref[0] = spread(pa[:, 768:1024], one)
    pb = _dot(h, wb_ref[...])
    conv_ref[0] = pb[:, 0:512]
    qs_ref[0] = spread(qknorm(pb[:, 512:768], 2), zero)
    ks_ref[0] = spread(qknorm(pb[:, 768:1024], 3), zero)
    vs_ref[0] = pb[:, 1024:1280].astype(BF16)


def _inproj(x, g, wa, wb, wf, bfg, gains, sel, const):
    B, L, D = x.shape
    tm = min(TM_PROJ, L)
    c2 = lambda b, l: (0, 0)
    row_spec = lambda n: pl.BlockSpec((1, tm, n), lambda b, l: (b, l, 0))
    return pl.pallas_call(
        functools.partial(_inproj_kernel, tm=tm),
        grid=(B, L // tm),
        in_specs=[
            row_spec(D),
            pl.BlockSpec((1, D), c2),
            pl.BlockSpec(wa.shape, c2),
            pl.BlockSpec(wb.shape, c2),
            pl.BlockSpec((D, LANES), c2),
            pl.BlockSpec((1, LANES), c2),
            pl.BlockSpec((4, D_BRANCH), c2),
            pl.BlockSpec(sel.shape, c2),
            pl.BlockSpec(const.shape, c2),
        ],
        out_specs=[
            row_spec(D_BRANCH),
            row_spec(D_SPREAD), row_spec(D_SPREAD), row_spec(D_SPREAD),
            row_spec(2 * D_BRANCH),
            row_spec(D_SPREAD), row_spec(D_SPREAD), row_spec(D_BRANCH),
        ],
        out_shape=[
            jax.ShapeDtypeStruct((B, L, D_BRANCH), F32),
            jax.ShapeDtypeStruct((B, L, D_SPREAD), BF16),
            jax.ShapeDtypeStruct((B, L, D_SPREAD), BF16),
            jax.ShapeDtypeStruct((B, L, D_SPREAD), BF16),
            jax.ShapeDtypeStruct((B, L, 2 * D_BRANCH), F32),
            jax.ShapeDtypeStruct((B, L, D_SPREAD), BF16),
            jax.ShapeDtypeStruct((B, L, D_SPREAD), BF16),
            jax.ShapeDtypeStruct((B, L, D_BRANCH), BF16),
        ],
        scratch_shapes=[pltpu.VMEM((1, LANES), F32)],
        compiler_params=_cparams(("parallel", "arbitrary")),
        name="inproj",
    )(x, g, wa, wb, wf, bfg, gains, sel, const)


def _s5_kernel(u_ref, bmat_ref, cmat_ref, a_ref, d_ref, wglu_ref, y_ref, utb_ref, bu_ref, st_ref, *, lc, nb):
    ci = pl.program_id(0)

    @pl.when(ci == 0)
    def _():
        st_ref[...] = jnp.zeros_like(st_ref)

    n_half = D_BRANCH // LANES
    for b in range(nb):
        for j in range(n_half):
            utb_ref.at[j][pl.ds(b, lc, stride=nb), :] = u_ref[b, :, j * LANES:(j + 1) * LANES]
    u = jnp.concatenate([utb_ref[j] for j in range(n_half)], axis=1)
    bu_ref[...] = _dot(u.astype(BF16), bmat_ref[...])

    a_re = jnp.broadcast_to(a_ref[0:1, :], (nb, N_STATE))
    a_im = jnp.broadcast_to(a_ref[1:2, :], (nb, N_STATE))

    def step(t, carry):
        s_re, s_im = carry
        r0 = pl.multiple_of(t * nb, nb)
        b_re = bu_ref[pl.ds(r0, nb), 0:N_STATE]
        b_im = bu_ref[pl.ds(r0, nb), N_STATE:2 * N_STATE]
        n_re = a_re * s_re - a_im * s_im + b_re
        n_im = a_re * s_im + a_im * s_re + b_im
        bu_ref[pl.ds(r0, nb), 0:N_STATE] = n_re
        bu_ref[pl.ds(r0, nb), N_STATE:2 * N_STATE] = n_im
        return n_re, n_im

    s_re, s_im = lax.fori_loop(0, lc, step, (st_ref[:, 0:N_STATE], st_ref[:, N_STATE:2 * N_STATE]),
                               unroll=2)
    st_ref[:, 0:N_STATE] = s_re
    st_ref[:, N_STATE:2 * N_STATE] = s_im

    y = _dot(bu_ref[...].astype(BF16), cmat_ref[...]) + d_ref[...] * u
    y = jax.nn.gelu(y, approximate=True)
    y = y * _sigmoid(_dot(y.astype(BF16), wglu_ref[...]))
    for j in range(n_half):
        utb_ref[j] = y[:, j * LANES:(j + 1) * LANES]
    for b in range(nb):
        for j in range(n_half):
            y_ref[b, :, j * LANES:(j + 1) * LANES] = utb_ref.at[j][pl.ds(b, lc, stride=nb), :].astype(y_ref.dtype)


def _s5(u, bmat, cmat, a, d, wglu):
    nb, L, _ = u.shape
    lc = min(LC_S5, L)
    const = lambda c: (0, 0)
    return pl.pallas_call(
        functools.partial(_s5_kernel, lc=lc, nb=nb),
        grid=(L // lc,),
        in_specs=[
            pl.BlockSpec((nb, lc, D_BRANCH), lambda c: (0, c, 0)),
            pl.BlockSpec((D_BRANCH, 2 * N_STATE), const),
            pl.BlockSpec((2 * N_STATE, D_BRANCH), const),
            pl.BlockSpec((2, N_STATE), const),
            pl.BlockSpec((1, D_BRANCH), const),
            pl.BlockSpec((D_BRANCH, D_BRANCH), const),
        ],
        out_specs=pl.BlockSpec((nb, lc, D_BRANCH), lambda c: (0, c, 0)),
        out_shape=jax.ShapeDtypeStruct((nb, L, D_BRANCH), BF16),
        scratch_shapes=[pltpu.VMEM((D_BRANCH // LANES, lc * nb, LANES), F32), pltpu.VMEM((lc * nb, 2 * N_STATE), F32),
                        pltpu.VMEM((nb, 2 * N_STATE), F32)],
        compiler_params=_cparams(("arbitrary",)),
        name="s5",
    )(u, bmat, cmat, a, d, wglu)


def _fox_kernel(q_ref, k_ref, v_ref, o_ref, sa_ref, sb_ref, m_ref, acc_ref, *, tq):
    qi = pl.program_id(2)
    lane = lax.broadcasted_iota(jnp.int32, (tq, LANES), 1)

    def logits(ki, s_ref):
        k0 = pl.multiple_of(ki * tq, tq)
        for h in range(2):
            hs = slice(h * LANES, (h + 1) * LANES)
            s_ref[h] = _dot_nt(q_ref[0, :, hs], k_ref[0, pl.ds(k0, tq), hs])

    def update_rows(ki, s_ref, r0, nr, nc, masked):
        k0 = pl.multiple_of(ki * tq, tq)
        rs = slice(r0, r0 + nr)
        for h in range(2):
            hs = slice(h * LANES, (h + 1) * LANES)
            s = s_ref[h, rs, 0:nc]
            if masked:
                row = lax.broadcasted_iota(jnp.int32, (nr, nc), 0) + r0
                col = lax.broadcasted_iota(jnp.int32, (nr, nc), 1)
                s = jnp.where(col <= row, s, NEG_INF)
            m = m_ref[h, rs]
            m_new = jnp.maximum(m, jnp.max(s, axis=1, keepdims=True))
            p = jnp.exp2(s - _lane_tile(m_new, nc // LANES))
            acc_ref[h, rs] = (jnp.exp2(m - m_new) * acc_ref[h, rs]
                              + _dot(p.astype(BF16), v_ref[0, pl.ds(k0, nc), hs]))
            m_ref[h, rs] = m_new

    def update(ki, s_ref, diag):
        if diag:
            half = tq // 2
            update_rows(ki, s_ref, 0, half, half, True)
            update_rows(ki, s_ref, half, half, tq, True)
        else:
            update_rows(ki, s_ref, 0, tq, tq, False)

    m_ref[...] = jnp.full(m_ref.shape, NEG_INF, F32)
    acc_ref[...] = jnp.zeros_like(acc_ref)
    logits(0, sa_ref)

    def pair(j, _):
        logits(2 * j + 1, sb_ref)
        update(2 * j, sa_ref, False)
        logits(2 * j + 2, sa_ref)
        update(2 * j + 1, sb_ref, False)
        return 0

    lax.fori_loop(0, qi // 2, pair, 0)

    @pl.when(qi % 2 == 0)
    def _():
        update(qi, sa_ref, True)

    @pl.when(qi % 2 == 1)
    def _():
        logits(qi, sb_ref)
        update(qi - 1, sa_ref, False)
        update(qi, sb_ref, True)

    res = [acc_ref[h] / pltpu.roll(acc_ref[h], HEAD_DIM, axis=1) for h in range(2)]
    o_ref[0] = jnp.where(lane < HEAD_DIM, res[0], pltpu.roll(res[1], HEAD_DIM, axis=1)).astype(o_ref.dtype)


def _fox(q, k, v):
    B, L, _ = q.shape
    tq = min(T_FOX, L)
    return pl.pallas_call(
        functools.partial(_fox_kernel, tq=tq),
        grid=(B, 2, L // tq),
        in_specs=[
            pl.BlockSpec((1, tq, 2 * LANES), lambda b, p, i: (b, i, p)),
            pl.BlockSpec((1, L, 2 * LANES), lambda b, p, i: (b, 0, p)),
            pl.BlockSpec((1, L, 2 * LANES), lambda b, p, i: (b, 0, p)),
        ],
        out_specs=pl.BlockSpec((1, tq, LANES), lambda b, p, i: (b, i, p)),
        out_shape=jax.ShapeDtypeStruct((B, L, D_BRANCH), BF16),
        scratch_shapes=[pltpu.VMEM((2, tq, tq), F32), pltpu.VMEM((2, tq, tq), F32),
                        pltpu.VMEM((2, tq, LANES), F32), pltpu.VMEM((2, tq, LANES), F32)],
        compiler_params=_cparams(("parallel", "parallel", "arbitrary")),
        name="fox",
    )(q, k, v)


def _sb_kernel(q_ref, k_ref, v_ref, o_ref, za_ref, zb_ref, run_ref, acc_ref, *, tq, blk):
    qi = pl.program_id(2)
    half = tq // 2
    lane = lax.broadcasted_iota(jnp.int32, (tq, LANES), 1)
    ur = lax.broadcasted_iota(jnp.int32, (blk, blk), 0)
    uc = lax.broadcasted_iota(jnp.int32, (blk, blk), 1)
    upper = jnp.where(ur > uc, 1.0, 0.0).astype(BF16)

    def logits(ki, z_ref, r0, nr, c0, nc):
        k0 = pl.multiple_of(jnp.maximum(ki, 0) * tq + c0, nc)
        for h in range(2):
            hs = slice(h * LANES, (h + 1) * LANES)
            z_ref[h, r0:r0 + nr, c0:c0 + nc] = _dot_nt(q_ref[0, r0:r0 + nr, hs], k_ref[0, pl.ds(k0, nc), hs])

    def update_rows(ki, z_ref, r0, nr, c0, nc, masked):
        k0 = pl.multiple_of(ki * tq + c0, nc)
        rs = slice(r0, r0 + nr)
        vt = v_ref[0, pl.ds(k0, nc), :]
        if masked:
            past = (lax.broadcasted_iota(jnp.int32, (nr, nc), 1) + c0
                    < lax.broadcasted_iota(jnp.int32, (nr, nc), 0) + r0)
        for h in range(2):
            run = run_ref[h, rs]
            z = z_ref[h, rs, c0:c0 + nc]
            neg_abs = pltpu.bitcast(pltpu.bitcast(z, jnp.uint32) | jnp.uint32(0x80000000), F32)
            l1p = jnp.log(1.0 + jnp.exp2(neg_abs)) * LOG2E
            log_beta = jnp.minimum(z, 0.0) - l1p
            log_keep = log_beta - z
            if masked:
                log_keep = jnp.where(past, log_keep, 0.0)
            keep16 = log_keep.astype(BF16)
            n_c = nc // blk
            after = [None] * n_c
            for c in reversed(range(n_c)):
                cs = slice(c * blk, (c + 1) * blk)
                raw = _dot(keep16[:, cs], upper)
                after[c] = raw + _lane_tile(run, blk // LANES)
                run = run + (raw[:, 0:1] + log_keep[:, c * blk:c * blk + 1])
            a = jnp.exp2(log_beta + _lane_cat(after))
            if masked:
                a = jnp.where(past, a, 0.0)
            acc_ref[h, rs] += _dot(a.astype(BF16), vt)
            run_ref[h, rs] = run

    def alive():
        return jnp.max(run_ref[...]) > SB_RUN_FLOOR

    run_ref[...] = jnp.zeros_like(run_ref)
    acc_ref[...] = jnp.zeros_like(acc_ref)
    logits(qi, za_ref, 0, tq, 0, half)
    logits(qi, za_ref, half, half, half, half)
    logits(qi - 1, zb_ref, 0, tq, half, half)
    update_rows(qi, za_ref, 0, half, 0, half, True)
    update_rows(qi, za_ref, half, half, 0, tq, True)

    def step(carry):
        j, _ = carry
        t = qi - 1 - j
        logits(t, za_ref, 0, tq, 0, half)
        update_rows(t, zb_ref, 0, tq, half, half, False)

        @pl.when(alive())
        def _():
            logits(t - 1, zb_ref, 0, tq, half, half)
            update_rows(t, za_ref, 0, tq, 0, half, False)

        return j + 1, alive()

    lax.while_loop(lambda c: (c[0] < qi) & c[1], step, (jnp.int32(0), alive()))

    o_ref[0] = jnp.where(lane < HEAD_DIM, acc_ref[0], acc_ref[1]).astype(o_ref.dtype)


def _sb(q, k, v):
    B, L, _ = q.shape
    tq = min(T_ATT, L)
    return pl.pallas_call(
        functools.partial(_sb_kernel, tq=tq, blk=min(SB_BLOCK, tq // 2)),
        grid=(B, 2, L // tq),
        in_specs=[
            pl.BlockSpec((1, tq, 2 * LANES), lambda b, p, i: (b, i, p)),
            pl.BlockSpec((1, L, 2 * LANES), lambda b, p, i: (b, 0, p)),
            pl.BlockSpec((1, L, LANES), lambda b, p, i: (b, 0, p)),
        ],
        out_specs=pl.BlockSpec((1, tq, LANES), lambda b, p, i: (b, i, p)),
        out_shape=jax.ShapeDtypeStruct((B, L, D_BRANCH), BF16),
        scratch_shapes=[pltpu.VMEM((2, tq, tq), F32), pltpu.VMEM((2, tq, tq), F32),
                        pltpu.VMEM((2, tq, LANES), F32), pltpu.VMEM((2, tq, LANES), F32)],
        compiler_params=_cparams(("parallel", "parallel", "arbitrary")),
        name="stickbreak",
    )(q, k, v)


def _conv_kernel(ab_ref, w_ref, b_ref, g_ref, beta_ref, o_ref, pad_ref, sh_ref, *, lc):
    li = pl.program_id(1)

    @pl.when(li == 0)
    def _():
        pad_ref[0:CONV_HALO, :] = jnp.zeros((CONV_HALO, D_BRANCH), F32)

    ab = ab_ref[0]
    pad_ref[CONV_HALO:CONV_HALO + lc, :] = ab[:, 0:D_BRANCH] * _sigmoid(ab[:, D_BRANCH:2 * D_BRANCH])
    span = lc + CONV_HALO - SUBLANES
    for ph in range(1, SUBLANES):
        sh_ref[ph - 1] = pad_ref[ph:ph + span, :]
    off = CONV_HALO - (CONV_WIDTH - 1)
    acc = jnp.zeros((lc, D_BRANCH), F32) + b_ref[...]
    for j in range(CONV_WIDTH):
        ph, base = (off + j) % SUBLANES, (off + j) // SUBLANES * SUBLANES
        tap = pad_ref[base:base + lc, :] if ph == 0 else sh_ref[ph - 1, base:base + lc, :]
        acc = acc + w_ref[j:j + 1, :] * tap
    pad_ref[0:CONV_HALO, :] = pad_ref[lc:lc + CONV_HALO, :]
    mu = jnp.mean(acc, axis=-1, keepdims=True)
    xc = acc - mu
    var = jnp.mean(xc * xc, axis=-1, keepdims=True)
    y = xc * lax.rsqrt(var + EPS) * g_ref[...] + beta_ref[...]
    o_ref[0] = (y * _sigmoid(y)).astype(o_ref.dtype)


def _conv(ab, w, b, g, beta):
    B, L, _ = ab.shape
    lc = min(LC_CONV, L)
    const = lambda b_, l: (0, 0)
    return pl.pallas_call(
        functools.partial(_conv_kernel, lc=lc),
        grid=(B, L // lc),
        in_specs=[
            pl.BlockSpec((1, lc, 2 * D_BRANCH), lambda b_, l: (b_, l, 0)),
            pl.BlockSpec((CONV_WIDTH, D_BRANCH), const),
            pl.BlockSpec((1, D_BRANCH), const),
            pl.BlockSpec((1, D_BRANCH), const),
            pl.BlockSpec((1, D_BRANCH), const),
        ],
        out_specs=pl.BlockSpec((1, lc, D_BRANCH), lambda b_, l: (b_, l, 0)),
        out_shape=jax.ShapeDtypeStruct((B, L, D_BRANCH), BF16),
        scratch_shapes=[pltpu.VMEM((CONV_HALO + lc, D_BRANCH), F32),
                        pltpu.VMEM((SUBLANES - 1, CONV_HALO + lc - SUBLANES, D_BRANCH), F32)],
        compiler_params=_cparams(("parallel", "arbitrary")),
        name="conv",
    )(ab, w, b, g, beta)


def _merge_kernel(x_ref, g_ref, ys5_ref, yfox_ref, yconv_ref, ysb_ref, wg_ref, wb_ref, wo_ref, o_ref):
    x = x_ref[0]
    D = x.shape[-1]
    h = _rmsnorm(x, g_ref[...]).astype(BF16)
    ys = (ys5_ref[0], yfox_ref[0], yconv_ref[0], ysb_ref[0])
    merged = None
    for n in range(4):
        gate = _sigmoid(_dot(h, wg_ref[:, n * D:(n + 1) * D]))
        term = gate * _dot(ys[n], wb_ref[n])
        merged = term if merged is None else merged + term
    o_ref[0] = x + _dot(merged.astype(BF16), wo_ref[...])


def _merge(x, g, ys5, yfox, yconv, ysb, wg, wb, wo):
    B, L, D = x.shape
    tm = min(TM_MERGE, L)
    c2 = lambda b, l: (0, 0)
    yspec = pl.BlockSpec((1, tm, D_BRANCH), lambda b, l: (b, l, 0))
    return pl.pallas_call(
        _merge_kernel,
        grid=(B, L // tm),
        in_specs=[
            pl.BlockSpec((1, tm, D), lambda b, l: (b, l, 0)),
            pl.BlockSpec((1, D), c2),
            yspec, yspec, yspec, yspec,
            pl.BlockSpec((D, 4 * D), c2),
            pl.BlockSpec((4, D_BRANCH, D), lambda b, l: (0, 0, 0)),
            pl.BlockSpec((D, D), c2),
        ],
        out_specs=pl.BlockSpec((1, tm, D), lambda b, l: (b, l, 0)),
        out_shape=jax.ShapeDtypeStruct((B, L, D), F32),
        compiler_params=_cparams(("parallel", "parallel")),
        name="merge",
    )(x, g, ys5, yfox, yconv, ysb, wg, wb, wo)


def _swiglu(h, w1_ref, w3_ref, w2_ref, lead, fc):
    acc = None
    for c in range(w1_ref.shape[-1] // fc):
        cs = slice(c * fc, (c + 1) * fc)
        a = _dot(h, w1_ref[lead + (slice(None), cs)])
        b = _dot(h, w3_ref[lead + (slice(None), cs)])
        part = _dot((a * _sigmoid(a) * b).astype(BF16), w2_ref[lead + (cs, slice(None))])
        acc = part if acc is None else acc + part
    return acc


def _ffn_kernel(x_ref, g_ref, w1_ref, w3_ref, w2_ref, o_ref, *, fc):
    x = x_ref[...]
    h = _rmsnorm(x, g_ref[...]).astype(BF16)
    o_ref[...] = x + _swiglu(h, w1_ref, w3_ref, w2_ref, (), fc)


def _ffn(x2, g, w1, w3, w2):
    T, D = x2.shape
    F = w1.shape[1]
    tm = min(TM_FFN, T)
    c2 = lambda i: (0, 0)
    return pl.pallas_call(
        functools.partial(_ffn_kernel, fc=FC_FFN),
        grid=(T // tm,),
        in_specs=[
            pl.BlockSpec((tm, D), lambda i: (i, 0)),
            pl.BlockSpec((1, D), c2),
            pl.BlockSpec((D, F), c2), pl.BlockSpec((D, F), c2), pl.BlockSpec((F, D), c2),
        ],
        out_specs=pl.BlockSpec((tm, D), lambda i: (i, 0)),
        out_shape=jax.ShapeDtypeStruct((T, D), F32),
        compiler_params=pltpu.CompilerParams(dimension_semantics=("parallel",),
                                             vmem_limit_bytes=VMEM_LIMIT_FFN),
        name="dense_ffn",
    )(x2, g, w1, w3, w2)


def _router_kernel(x_ref, g_ref, wr_ref, br_ref, meta_ref, cnt_ref, carry_ref, *, tm):
    i = pl.program_id(0)

    @pl.when(i == 0)
    def _():
        carry_ref[...] = jnp.zeros_like(carry_ref)

    h = _rmsnorm(x_ref[...], g_ref[...])
    logits = jnp.dot(h, wr_ref[...], preferred_element_type=F32, precision=lax.Precision.HIGHEST) + br_ref[...]
    lane = lax.broadcasted_iota(jnp.int32, logits.shape, 1).astype(F32)
    m1 = jnp.max(logits, axis=1, keepdims=True)
    i1 = jnp.min(jnp.where(logits == m1, lane, float(LANES)), axis=1, keepdims=True)
    rest = jnp.where(lane == i1, NEG_INF, logits)
    m2 = jnp.max(rest, axis=1, keepdims=True)
    i2 = jnp.min(jnp.where(rest == m2, lane, float(LANES)), axis=1, keepdims=True)
    e2 = jnp.exp(m2 - m1)
    p1 = 1.0 / (1.0 + e2)
    p2 = e2 * p1
    onehot = jnp.where(lane == i1, 1.0, 0.0) + jnp.where(lane == i2, 1.0, 0.0)
    rr = lax.broadcasted_iota(jnp.int32, (tm, tm), 0)
    cc = lax.broadcasted_iota(jnp.int32, (tm, tm), 1)
    before = jnp.where(cc < rr, 1.0, 0.0).astype(BF16)
    prefix = _dot(before, onehot.astype(BF16)) + carry_ref[...]
    r1 = jnp.sum(jnp.where(lane == i1, prefix, 0.0), axis=1, keepdims=True)
    r2 = jnp.sum(jnp.where(lane == i2, prefix, 0.0), axis=1, keepdims=True)
    meta = jnp.zeros_like(logits)
    for n, col in enumerate((i1, i2, p1, p2, r1, r2)):
        meta = jnp.where(lane == float(n), col, meta)
    meta_ref[...] = meta
    total = carry_ref[...] + jnp.sum(onehot, axis=0, keepdims=True)
    carry_ref[...] = total
    cnt_ref[...] = total


def _router(x2, g, wr, br):
    T, D = x2.shape
    tm = min(TM_FFN, T)
    c2 = lambda i: (0, 0)
    return pl.pallas_call(
        functools.partial(_router_kernel, tm=tm),
        grid=(T // tm,),
        in_specs=[
            pl.BlockSpec((tm, D), lambda i: (i, 0)),
            pl.BlockSpec((1, D), c2),
            pl.BlockSpec((D, LANES), c2),
            pl.BlockSpec((1, LANES), c2),
        ],
        out_specs=[pl.BlockSpec((tm, LANES), lambda i: (i, 0)), pl.BlockSpec((1, LANES), c2)],
        out_shape=[jax.ShapeDtypeStruct((T, LANES), F32), jax.ShapeDtypeStruct((1, LANES), F32)],
        scratch_shapes=[pltpu.VMEM((1, LANES), F32)],
        compiler_params=_cparams(("arbitrary",)),
        name="router",
    )(x2, g, wr, br)


def _to_slabs(x):
    return x.reshape(x.shape[0], SUBLANES, x.shape[1] // SUBLANES)


def _from_slabs(x):
    return x.reshape(x.shape[0], x.shape[1] * x.shape[2])


def _dispatch_kernel(dst_ref, ztile_ref, x_ref, xs_hbm, xt_ref, zero_ref, sem_ref, zsem_ref, *, tt, tm):
    i = pl.program_id(0)

    @pl.when(i == 0)
    def _():
        zero_ref[...] = jnp.zeros_like(zero_ref)
        for e in range(2 * N_EXPERTS):
            @pl.when(ztile_ref[e] >= 0)
            def _():
                start = pl.multiple_of(ztile_ref[e], tm)
                fill = pltpu.make_async_copy(zero_ref, xs_hbm.at[pl.ds(start, tm)], zsem_ref)
                fill.start()
                fill.wait()

    xt_ref[...] = _to_slabs(x_ref[...])

    def body(r, _):
        for k in range(2):
            pltpu.make_async_copy(xt_ref.at[r], xs_hbm.at[dst_ref[2 * (i * tt + r) + k]],
                                  sem_ref).start(priority=k)
        return 0

    lax.fori_loop(0, tt, body, 0, unroll=4)
    for _ in range(2):
        pltpu.make_async_copy(xt_ref, xs_hbm.at[pl.ds(0, tt)], sem_ref).wait()


def _dispatch(x2, dst, ztile, R):
    T, D = x2.shape
    tt = min(TT_MOE, T)
    slab = (SUBLANES, D // SUBLANES)
    grid_spec = pltpu.PrefetchScalarGridSpec(
        num_scalar_prefetch=2,
        grid=(T // tt,),
        in_specs=[pl.BlockSpec((tt, D), lambda i, d, z: (i, 0))],
        out_specs=pl.BlockSpec(memory_space=pl.ANY),
        scratch_shapes=[pltpu.VMEM((tt,) + slab, F32), pltpu.VMEM((TM_MOE,) + slab, F32),
                        pltpu.SemaphoreType.DMA(()), pltpu.SemaphoreType.DMA(())],
    )
    return pl.pallas_call(
        functools.partial(_dispatch_kernel, tt=tt, tm=TM_MOE),
        grid_spec=grid_spec,
        out_shape=jax.ShapeDtypeStruct((R,) + slab, F32),
        compiler_params=_cparams(("arbitrary",)),
        name="dispatch",
    )(dst, ztile, x2)


def _experts_kernel(texp_ref, nused_ref, xs_ref, g_ref, w1_ref, w3_ref, w2_ref, y_ref, *, fc):
    i = pl.program_id(0)

    @pl.when(i < nused_ref[0])
    def _():
        h = _rmsnorm(_from_slabs(xs_ref[...]), g_ref[...]).astype(BF16)
        y_ref[...] = _to_slabs(_swiglu(h, w1_ref, w3_ref, w2_ref, (0,), fc))

    @pl.when(i >= nused_ref[0])
    def _():
        y_ref[...] = jnp.zeros_like(y_ref)


def _experts(xs, g, tile_expert, n_used, w1, w3, w2):
    R = xs.shape[0]
    slab = xs.shape[1:]
    D, F = w1.shape[1:]
    tm = TM_MOE
    wspec = lambda shape: pl.BlockSpec(shape, lambda i, te, n: (te[i], 0, 0))
    grid_spec = pltpu.PrefetchScalarGridSpec(
        num_scalar_prefetch=2,
        grid=(R // tm,),
        in_specs=[
            pl.BlockSpec((tm,) + slab, lambda i, te, n: (jnp.minimum(i, n[0] - 1), 0, 0)),
            pl.BlockSpec((1, D), lambda i, te, n: (0, 0)),
            wspec((1, D, F)), wspec((1, D, F)), wspec((1, F, D)),
        ],
        out_specs=pl.BlockSpec((tm,) + slab, lambda i, te, n: (i, 0, 0)),
    )
    return pl.pallas_call(
        functools.partial(_experts_kernel, fc=FC_FFN),
        grid_spec=grid_spec,
        out_shape=jax.ShapeDtypeStruct((R,) + slab, F32),
        compiler_params=pltpu.CompilerParams(dimension_semantics=("arbitrary",),
                                             vmem_limit_bytes=VMEM_LIMIT_FFN),
        name="experts",
    )(tile_expert, n_used, xs, g, w1, w3, w2)


def _combine_kernel(dst_ref, x_ref, p_ref, y_hbm, o_ref, buf_ref, sem_ref, *, tc, n_tiles):
    i = pl.program_id(0)
    slot = lax.rem(i, 2)

    def issue(tile, s):
        base = tile * (2 * tc)

        def body(r, _):
            for k in range(2):
                pltpu.make_async_copy(y_hbm.at[dst_ref[base + 2 * r + k]], buf_ref.at[s, k, r],
                                      sem_ref.at[s]).start(priority=k)
            return 0

        lax.fori_loop(0, tc, body, 0, unroll=4)

    @pl.when(i == 0)
    def _():
        issue(0, 0)

    @pl.when(i + 1 < n_tiles)
    def _():
        issue(i + 1, 1 - slot)

    for k in range(2):
        pltpu.make_async_copy(y_hbm.at[pl.ds(0, tc)], buf_ref.at[slot, k], sem_ref.at[slot]).wait()
    o_ref[...] = (x_ref[...] + p_ref[:, 0:1] * _from_slabs(buf_ref[slot, 0])
                  + p_ref[:, 1:2] * _from_slabs(buf_ref[slot, 1]))


def _combine(x2, prob, y, dst):
    T, D = x2.shape
    tc = min(TC_MOE, T)
    n_tiles = T // tc
    grid_spec = pltpu.PrefetchScalarGridSpec(
        num_scalar_prefetch=1,
        grid=(n_tiles,),
        in_specs=[pl.BlockSpec((tc, D), lambda i, d: (i, 0)), pl.BlockSpec((tc, 2), lambda i, d: (i, 0)),
                  pl.BlockSpec(memory_space=pl.ANY)],
        out_specs=pl.BlockSpec((tc, D), lambda i, d: (i, 0)),
        scratch_shapes=[pltpu.VMEM((2, 2, tc) + y.shape[1:], F32), pltpu.SemaphoreType.DMA((2,))],
    )
    return pl.pallas_call(
        functools.partial(_combine_kernel, tc=tc, n_tiles=n_tiles),
        grid_spec=grid_spec,
        out_shape=jax.ShapeDtypeStruct((T, D), F32),
        compiler_params=_cparams(("arbitrary",)),
        name="combine",
    )(dst, x2, prob, y)


def _moe(x2, g, wr, br, w1, w3, w2):
    T, D = x2.shape
    tm = TM_MOE
    meta, cnt = _router(x2, g, wr, br)
    expert = meta[:, 0:2].astype(jnp.int32)
    prob = meta[:, 2:4]
    rank = meta[:, 4:6].astype(jnp.int32)
    counts = cnt[0, :N_EXPERTS].astype(jnp.int32)
    padded = (counts + (tm - 1)) // tm * tm
    ends = jnp.cumsum(padded)
    dst = ((ends - padded)[expert] + rank).reshape(-1)
    R = 2 * T + N_EXPERTS * tm
    tile_start = jnp.arange(R // tm, dtype=jnp.int32) * tm
    tile_expert = jnp.minimum(jnp.searchsorted(ends, tile_start, side="right"), N_EXPERTS - 1).astype(jnp.int32)
    n_used = (ends[-1] // tm).astype(jnp.int32).reshape(1)
    last_tile = jnp.where(padded > 0, ends - tm, -1)
    tail = ends[-1] + jnp.arange(N_EXPERTS, dtype=jnp.int32) * tm
    ztile = jnp.concatenate([last_tile, jnp.where(tail < R, tail, -1)]).astype(jnp.int32)
    xs = _dispatch(x2, dst, ztile, R)
    y = _experts(xs, g, tile_expert, n_used, w1, w3, w2)
    return _combine(x2, prob, y, dst)


def _s5_params(lam_re, lam_im, log_dt, b_re, b_im, c_re, c_im):
    G, P, H = S5_GROUPS, S5_STATE, S5_GROUP
    dt = jnp.exp(log_dt)[:, None]
    mag = jnp.exp(lam_re * dt)
    ab_re = mag * jnp.cos(lam_im * dt)
    ab_im = mag * jnp.sin(lam_im * dt)
    nr, ni = ab_re - 1.0, ab_im
    den = lam_re * lam_re + lam_im * lam_im
    k_re = (nr * lam_re + ni * lam_im) / den
    k_im = (ni * lam_re - nr * lam_im) / den
    bb_re = k_re[..., None] * b_re - k_im[..., None] * b_im
    bb_im = k_re[..., None] * b_im + k_im[..., None] * b_re
    eye = jnp.eye(G, dtype=F32)
    bm_re = jnp.einsum("gph,gk->ghkp", bb_re, eye).reshape(G * H, G * P)
    bm_im = jnp.einsum("gph,gk->ghkp", bb_im, eye).reshape(G * H, G * P)
    bmat = jnp.concatenate([bm_re, bm_im], axis=1).astype(BF16)
    cm_re = jnp.einsum("ghp,gk->kpgh", c_re, eye).reshape(G * P, G * H)
    cm_im = jnp.einsum("ghp,gk->kpgh", c_im, eye).reshape(G * P, G * H)
    cmat = jnp.concatenate([cm_re, -cm_im], axis=0).astype(BF16)
    a = jnp.stack([ab_re.reshape(-1), ab_im.reshape(-1)], axis=0)
    return bmat, cmat, a


def kernel(x, g_mix, w_in, b_forget, fox_q_gain, fox_k_gain, sb_q_gain, sb_k_gain, s5_lam_re, s5_lam_im, s5_log_dt, s5_b_re, s5_b_im, s5_c_re, s5_c_im, s5_d, s5_w_glu, conv_w, conv_b, conv_ln_g, conv_ln_b, w_branch, w_out, g_ffn, ffn_w1, ffn_w3, ffn_w2, router_w, router_b, moe_w1, moe_w3, moe_w2):
    B, L, D = x.shape
    depth = g_mix.shape[0]
    assert B == SUBLANES, "the S5 scan lays the batch along the sublanes"
    n_qkv = 3 * D_BRANCH
    o_f = D_BRANCH + n_qkv
    o_c = o_f + N_HEADS
    o_g = o_c + 2 * D_BRANCH + n_qkv
    scale = 1.0 / math.sqrt(HEAD_DIM)
    sel_np, const_np = _forget_feature_tables()
    sel = jnp.asarray(sel_np, BF16)
    const = jnp.asarray(const_np, F32)

    w_a = w_in[:, :, :o_f].astype(BF16)
    w_b = w_in[:, :, o_c:o_g].astype(BF16)
    w_f = jnp.pad(w_in[:, :, o_f:o_c], ((0, 0), (0, 0), (0, LANES - N_HEADS))).astype(BF16)
    w_g = w_in[:, :, o_g:].astype(BF16)

    for i in range(depth):
        bfg = jnp.pad(b_forget[i], (0, LANES - N_HEADS)).reshape(1, LANES)
        gains = jnp.stack([jnp.tile(fox_q_gain[i] * (scale * LOG2E), N_HEADS), jnp.tile(fox_k_gain[i], N_HEADS),
                           jnp.tile(sb_q_gain[i] * (scale * LOG2E), N_HEADS), jnp.tile(sb_k_gain[i], N_HEADS)],
                          axis=0)
        us5, qf, kf, vf, conv_ab, qs, ks, vs = _inproj(x, g_mix[i].reshape(1, D), w_a[i], w_b[i], w_f[i], bfg,
                                                       gains, sel, const)

        bmat, cmat, a = _s5_params(s5_lam_re[i], s5_lam_im[i], s5_log_dt[i], s5_b_re[i], s5_b_im[i],
                                   s5_c_re[i], s5_c_im[i])
        ys5 = _s5(us5, bmat, cmat, a, s5_d[i].reshape(1, D_BRANCH), s5_w_glu[i].astype(BF16))
        yfox = _fox(qf, kf, vf)
        yconv = _conv(conv_ab, conv_w[i], conv_b[i].reshape(1, -1), conv_ln_g[i].reshape(1, -1),
                      conv_ln_b[i].reshape(1, -1))
        ysb = _sb(qs, ks, vs)

        x = _merge(x, g_mix[i].reshape(1, D), ys5, yfox, yconv, ysb, w_g[i],
                   w_branch[i].astype(BF16), w_out[i].astype(BF16))

        j = i // 2
        gf = g_ffn[i].reshape(1, D)
        if i % 2 == 0:
            x2 = _ffn(x.reshape(B * L, D), gf, ffn_w1[j].astype(BF16), ffn_w3[j].astype(BF16),
                      ffn_w2[j].astype(BF16))
        else:
            wr = jnp.pad(router_w[j], ((0, 0), (0, LANES - N_EXPERTS)))
            br = jnp.pad(router_b[j], (0, LANES - N_EXPERTS), constant_values=-1e30).reshape(1, LANES)
            x2 = _moe(x.reshape(B * L, D), gf, wr, br, moe_w1[j].astype(BF16), moe_w3[j].astype(BF16),
                      moe_w2[j].astype(BF16))
        x = x2.reshape(B, L, D)
    return x
```

```python
import functools
import math

import numpy as np

import jax
import jax.numpy as jnp
from jax import lax
from jax.experimental import pallas as pl
from jax.experimental.pallas import tpu as pltpu

F32 = jnp.float32
BF16 = jnp.bfloat16
EPS = 1e-6
LOG2E = math.log2(math.e)

D_BRANCH = 256
HEAD_DIM = 64
N_HEADS = D_BRANCH // HEAD_DIM
S5_GROUP = 16
S5_GROUPS = D_BRANCH // S5_GROUP
S5_STATE = 64
N_STATE = S5_GROUPS * S5_STATE
CONV_WIDTH = 31
CONV_HALO = 32
N_EXPERTS = 8
LANES = 128
SUBLANES = 8
VMEM_LIMIT = 56 * 1024 * 1024
D_SPREAD = N_HEADS * LANES
N_SPLIT = 3

TM_PROJ = 512
TM_MERGE = 512
TM_FFN = 512
T_ATT = 512
T_FOX = 1024
SB_BLOCK = 256
SB_RUN_FLOOR = -160.0
LC_S5 = 128
LC_CONV = 512
TM_MOE = 512
FC_FFN = 256
TT_MOE = 1024
TC_MOE = 512
VMEM_LIMIT_FFN = 60 * 1024 * 1024

NEG_INF = float("-inf")
F32_SIGN_BIT = 0x80000000
NO_EXPERT_LOGIT = -1e30


def _cparams(sem):
    return pltpu.CompilerParams(dimension_semantics=sem, vmem_limit_bytes=VMEM_LIMIT)


def _dot(a, b):
    return jnp.dot(a, b, preferred_element_type=F32)


def _dot_nt(a, b):
    return lax.dot_general(a, b, (((1,), (1,)), ((), ())), preferred_element_type=F32)


def _rmsnorm(x, g):
    ms = jnp.mean(x * x, axis=-1, keepdims=True)
    return x * lax.rsqrt(ms + EPS) * g


def _sigmoid(x):
    return 1.0 / (1.0 + jnp.exp(-x))


def _log_sigmoid(x):
    return jnp.minimum(x, 0.0) - jnp.log(1.0 + jnp.exp(-jnp.abs(x)))


def _lane_cat(parts):
    return parts[0] if len(parts) == 1 else jnp.concatenate(parts, axis=1)


def _lane_tile(x, n):
    return _lane_cat([x] * n)


def _forget_feature_tables():
    sel = np.zeros((LANES, 2 * D_SPREAD), np.float32)
    const = np.zeros((1, 2 * D_SPREAD), np.float32)
    for h in range(N_HEADS):
        for j in range(N_SPLIT):
            sel[j * N_HEADS + h, h * LANES + HEAD_DIM + j] = 1.0
            sel[j * N_HEADS + h, D_SPREAD + h * LANES + HEAD_DIM + N_SPLIT + j] = -1.0
            const[0, h * LANES + HEAD_DIM + N_SPLIT + j] = 1.0
            const[0, D_SPREAD + h * LANES + HEAD_DIM + j] = 1.0
    return sel, const


def _inproj_kernel(x_ref, g_ref, wa_ref, wb_ref, wf_ref, bf_ref, gain_ref, sel_ref, const_ref,
                   us5_ref, qf_ref, kf_ref, vf_ref, conv_ref, qs_ref, ks_ref, vs_ref, carry_ref, *, tm):
    li = pl.program_id(1)
    x = x_ref[0]
    h = _rmsnorm(x, g_ref[...]).astype(BF16)
    lane = lax.broadcasted_iota(jnp.int32, (tm, LANES), 1)

    r = lax.broadcasted_iota(jnp.int32, (D_BRANCH, D_BRANCH), 0) // HEAD_DIM
    c = lax.broadcasted_iota(jnp.int32, (D_BRANCH, D_BRANCH), 1) // HEAD_DIM
    ones_bd = jnp.where(r == c, 1.0, 0.0).astype(BF16)

    def qknorm(t, gi):
        ss = _dot((t * t).astype(BF16), ones_bd)
        return t * lax.rsqrt(ss * (1.0 / HEAD_DIM) + EPS) * gain_ref[gi:gi + 1, :]

    def spread(t, fill):
        blocks = []
        for hp in range(2):
            pair = t[:, hp * LANES:(hp + 1) * LANES]
            blocks += [pair, pltpu.roll(pair, HEAD_DIM, axis=1)]
        out = [jnp.where(lane < HEAD_DIM, blocks[n], fill(n)) for n in range(N_HEADS)]
        return jnp.concatenate(out, axis=1).astype(BF16)

    @pl.when(li == 0)
    def _():
        carry_ref[...] = jnp.zeros_like(carry_ref)

    lf = _log_sigmoid(_dot(h, wf_ref[...]) + bf_ref[...]) * LOG2E
    rr = lax.broadcasted_iota(jnp.int32, (tm, tm), 0)
    cc = lax.broadcasted_iota(jnp.int32, (tm, tm), 1)
    tri = jnp.where(cc <= rr, 1.0, 0.0).astype(BF16)

    def split3(v):
        hi = v.astype(BF16).astype(F32)
        r1 = v - hi
        mid = r1.astype(BF16).astype(F32)
        lo = (r1 - mid).astype(BF16).astype(F32)
        return hi, mid, lo

    hi, mid, lo = split3(lf)
    cum = _dot(tri, hi.astype(BF16)) + _dot(tri, mid.astype(BF16)) + _dot(tri, lo.astype(BF16)) + carry_ref[...]
    carry_ref[...] = cum[tm - 1:tm, :]
    hi, mid, lo = split3(jnp.where(lane < N_HEADS, cum, 0.0))
    packed = hi + pltpu.roll(mid, N_HEADS, axis=1) + pltpu.roll(lo, 2 * N_HEADS, axis=1)
    feat = _dot(packed.astype(BF16), sel_ref[...]) + const_ref[...]

    def feat_q(n):
        return feat[:, n * LANES:(n + 1) * LANES]

    def feat_k(n):
        return feat[:, D_SPREAD + n * LANES:D_SPREAD + (n + 1) * LANES]

    zero = lambda n: 0.0
    one = lambda n: 1.0

    pa = _dot(h, wa_ref[...])
    us5_ref[0] = pa[:, 0:256]
    qf_ref[0] = spread(qknorm(pa[:, 256:512], 0), feat_q)
    kf_ref[0] = spread(qknorm(pa[:, 512:768], 1), feat_k)
    vf_ref[0] = spread(pa[:, 768:1024], one)
    pb = _dot(h, wb_ref[...])
    conv_ref[0] = pb[:, 0:512]
    qs_ref[0] = spread(qknorm(pb[:, 512:768], 2), zero)
    ks_ref[0] = spread(qknorm(pb[:, 768:1024], 3), zero)
    vs_ref[0] = pb[:, 1024:1280].astype(BF16)


def _inproj(x, g, wa, wb, wf, bfg, gains, sel, const):
    B, L, D = x.shape
    tm = min(TM_PROJ, L)
    c2 = lambda b, l: (0, 0)
    row_spec = lambda n: pl.BlockSpec((1, tm, n), lambda b, l: (b, l, 0))
    return pl.pallas_call(
        functools.partial(_inproj_kernel, tm=tm),
        grid=(B, L // tm),
        in_specs=[
            row_spec(D),
            pl.BlockSpec((1, D), c2),
            pl.BlockSpec(wa.shape, c2),
            pl.BlockSpec(wb.shape, c2),
            pl.BlockSpec((D, LANES), c2),
            pl.BlockSpec((1, LANES), c2),
            pl.BlockSpec((4, D_BRANCH), c2),
            pl.BlockSpec(sel.shape, c2),
            pl.BlockSpec(const.shape, c2),
        ],
        out_specs=[
            row_spec(D_BRANCH),
            row_spec(D_SPREAD), row_spec(D_SPREAD), row_spec(D_SPREAD),
            row_spec(2 * D_BRANCH),
            row_spec(D_SPREAD), row_spec(D_SPREAD), row_spec(D_BRANCH),
        ],
        out_shape=[
            jax.ShapeDtypeStruct((B, L, D_BRANCH), F32),
            jax.ShapeDtypeStruct((B, L, D_SPREAD), BF16),
            jax.ShapeDtypeStruct((B, L, D_SPREAD), BF16),
            jax.ShapeDtypeStruct((B, L, D_SPREAD), BF16),
            jax.ShapeDtypeStruct((B, L, 2 * D_BRANCH), F32),
            jax.ShapeDtypeStruct((B, L, D_SPREAD), BF16),
            jax.ShapeDtypeStruct((B, L, D_SPREAD), BF16),
            jax.ShapeDtypeStruct((B, L, D_BRANCH), BF16),
        ],
        scratch_shapes=[pltpu.VMEM((1, LANES), F32)],
        compiler_params=_cparams(("parallel", "arbitrary")),
        name="inproj",
    )(x, g, wa, wb, wf, bfg, gains, sel, const)


def _s5_kernel(u_ref, bmat_ref, cmat_ref, a_ref, d_ref, wglu_ref, y_ref, utb_ref, bu_ref, st_ref, *, lc, nb):
    ci = pl.program_id(0)

    @pl.when(ci == 0)
    def _():
        st_ref[...] = jnp.zeros_like(st_ref)

    n_half = D_BRANCH // LANES
    for b in range(nb):
        for j in range(n_half):
            utb_ref.at[j][pl.ds(b, lc, stride=nb), :] = u_ref[b, :, j * LANES:(j + 1) * LANES]
    u = jnp.concatenate([utb_ref[j] for j in range(n_half)], axis=1)
    bu_ref[...] = _dot(u.astype(BF16), bmat_ref[...])

    a_re = jnp.broadcast_to(a_ref[0:1, :], (nb, N_STATE))
    a_im = jnp.broadcast_to(a_ref[1:2, :], (nb, N_STATE))

    def step(t, carry):
        s_re, s_im = carry
        r0 = pl.multiple_of(t * nb, nb)
        b_re = bu_ref[pl.ds(r0, nb), 0:N_STATE]
        b_im = bu_ref[pl.ds(r0, nb), N_STATE:2 * N_STATE]
        n_re = a_re * s_re - a_im * s_im + b_re
        n_im = a_re * s_im + a_im * s_re + b_im
        bu_ref[pl.ds(r0, nb), 0:N_STATE] = n_re
        bu_ref[pl.ds(r0, nb), N_STATE:2 * N_STATE] = n_im
        return n_re, n_im

    s_re, s_im = lax.fori_loop(0, lc, step, (st_ref[:, 0:N_STATE], st_ref[:, N_STATE:2 * N_STATE]),
                               unroll=2)
    st_ref[:, 0:N_STATE] = s_re
    st_ref[:, N_STATE:2 * N_STATE] = s_im

    y = _dot(bu_ref[...].astype(BF16), cmat_ref[...]) + d_ref[...] * u
    y = jax.nn.gelu(y, approximate=True)
    y = y * _sigmoid(_dot(y.astype(BF16), wglu_ref[...]))
    for j in range(n_half):
        utb_ref[j] = y[:, j * LANES:(j + 1) * LANES]
    for b in range(nb):
        for j in range(n_half):
            y_ref[b, :, j * LANES:(j + 1) * LANES] = utb_ref.at[j][pl.ds(b, lc, stride=nb), :].astype(y_ref.dtype)


def _s5(u, bmat, cmat, a, d, wglu):
    nb, L, _ = u.shape
    lc = min(LC_S5, L)
    const = lambda c: (0, 0)
    return pl.pallas_call(
        functools.partial(_s5_kernel, lc=lc, nb=nb),
        grid=(L // lc,),
        in_specs=[
            pl.BlockSpec((nb, lc, D_BRANCH), lambda c: (0, c, 0)),
            pl.BlockSpec((D_BRANCH, 2 * N_STATE), const),
            pl.BlockSpec((2 * N_STATE, D_BRANCH), const),
            pl.BlockSpec((2, N_STATE), const),
            pl.BlockSpec((1, D_BRANCH), const),
            pl.BlockSpec((D_BRANCH, D_BRANCH), const),
        ],
        out_specs=pl.BlockSpec((nb, lc, D_BRANCH), lambda c: (0, c, 0)),
        out_shape=jax.ShapeDtypeStruct((nb, L, D_BRANCH), BF16),
        scratch_shapes=[pltpu.VMEM((D_BRANCH // LANES, lc * nb, LANES), F32), pltpu.VMEM((lc * nb, 2 * N_STATE), F32),
                        pltpu.VMEM((nb, 2 * N_STATE), F32)],
        compiler_params=_cparams(("arbitrary",)),
        name="s5",
    )(u, bmat, cmat, a, d, wglu)


def _fox_kernel(q_ref, k_ref, v_ref, o_ref, sa_ref, sb_ref, m_ref, acc_ref, *, tq):
    qi = pl.program_id(2)
    lane = lax.broadcasted_iota(jnp.int32, (tq, LANES), 1)

    def logits(ki, s_ref):
        k0 = pl.multiple_of(ki * tq, tq)
        for h in range(2):
            hs = slice(h * LANES, (h + 1) * LANES)
            s_ref[h] = _dot_nt(q_ref[0, :, hs], k_ref[0, pl.ds(k0, tq), hs])

    def update_rows(ki, s_ref, r0, nr, nc, masked):
        k0 = pl.multiple_of(ki * tq, tq)
        rs = slice(r0, r0 + nr)
        for h in range(2):
            hs = slice(h * LANES, (h + 1) * LANES)
            s = s_ref[h, rs, 0:nc]
            if masked:
                row = lax.broadcasted_iota(jnp.int32, (nr, nc), 0) + r0
                col = lax.broadcasted_iota(jnp.int32, (nr, nc), 1)
                s = jnp.where(col <= row, s, NEG_INF)
            m = m_ref[h, rs]
            m_new = jnp.maximum(m, jnp.max(s, axis=1, keepdims=True))
            p = jnp.exp2(s - _lane_tile(m_new, nc // LANES))
            acc_ref[h, rs] = (jnp.exp2(m - m_new) * acc_ref[h, rs]
                              + _dot(p.astype(BF16), v_ref[0, pl.ds(k0, nc), hs]))
            m_ref[h, rs] = m_new

    def update(ki, s_ref, diag):
        if diag:
            half = tq // 2
            update_rows(ki, s_ref, 0, half, half, True)
            update_rows(ki, s_ref, half, half, tq, True)
        else:
            update_rows(ki, s_ref, 0, tq, tq, False)

    m_ref[...] = jnp.full(m_ref.shape, NEG_INF, F32)
    acc_ref[...] = jnp.zeros_like(acc_ref)
    logits(0, sa_ref)

    def pair(j, _):
        logits(2 * j + 1, sb_ref)
        update(2 * j, sa_ref, False)
        logits(2 * j + 2, sa_ref)
        update(2 * j + 1, sb_ref, False)
        return 0

    lax.fori_loop(0, qi // 2, pair, 0)

    @pl.when(qi % 2 == 0)
    def _():
        update(qi, sa_ref, True)

    @pl.when(qi % 2 == 1)
    def _():
        logits(qi, sb_ref)
        update(qi - 1, sa_ref, False)
        update(qi, sb_ref, True)

    res = [acc_ref[h] / pltpu.roll(acc_ref[h], HEAD_DIM, axis=1) for h in range(2)]
    o_ref[0] = jnp.where(lane < HEAD_DIM, res[0], pltpu.roll(res[1], HEAD_DIM, axis=1)).astype(o_ref.dtype)


def _fox(q, k, v):
    B, L, _ = q.shape
    tq = min(T_FOX, L)
    return pl.pallas_call(
        functools.partial(_fox_kernel, tq=tq),
        grid=(B, 2, L // tq),
        in_specs=[
            pl.BlockSpec((1, tq, 2 * LANES), lambda b, p, i: (b, i, p)),
            pl.BlockSpec((1, L, 2 * LANES), lambda b, p, i: (b, 0, p)),
            pl.BlockSpec((1, L, 2 * LANES), lambda b, p, i: (b, 0, p)),
        ],
        out_specs=pl.BlockSpec((1, tq, LANES), lambda b, p, i: (b, i, p)),
        out_shape=jax.ShapeDtypeStruct((B, L, D_BRANCH), BF16),
        scratch_shapes=[pltpu.VMEM((2, tq, tq), F32), pltpu.VMEM((2, tq, tq), F32),
                        pltpu.VMEM((2, tq, LANES), F32), pltpu.VMEM((2, tq, LANES), F32)],
        compiler_params=_cparams(("parallel", "parallel", "arbitrary")),
        name="fox",
    )(q, k, v)


def _sb_kernel(q_ref, k_ref, v_ref, o_ref, za_ref, zb_ref, run_ref, acc_ref, *, tq, blk):
    qi = pl.program_id(2)
    half = tq // 2
    lane = lax.broadcasted_iota(jnp.int32, (tq, LANES), 1)
    ur = lax.broadcasted_iota(jnp.int32, (blk, blk), 0)
    uc = lax.broadcasted_iota(jnp.int32, (blk, blk), 1)
    upper = jnp.where(ur > uc, 1.0, 0.0).astype(BF16)

    def logits(ki, z_ref, r0, nr, c0, nc):
        k0 = pl.multiple_of(jnp.maximum(ki, 0) * tq + c0, nc)
        for h in range(2):
            hs = slice(h * LANES, (h + 1) * LANES)
            z_ref[h, r0:r0 + nr, c0:c0 + nc] = _dot_nt(q_ref[0, r0:r0 + nr, hs], k_ref[0, pl.ds(k0, nc), hs])

    def update_rows(ki, z_ref, r0, nr, c0, nc, masked):
        k0 = pl.multiple_of(ki * tq + c0, nc)
        rs = slice(r0, r0 + nr)
        vt = v_ref[0, pl.ds(k0, nc), :]
        if masked:
            past = (lax.broadcasted_iota(jnp.int32, (nr, nc), 1) + c0
                    < lax.broadcasted_iota(jnp.int32, (nr, nc), 0) + r0)
        for h in range(2):
            run = run_ref[h, rs]
            z = z_ref[h, rs, c0:c0 + nc]
            neg_abs = pltpu.bitcast(pltpu.bitcast(z, jnp.uint32) | jnp.uint32(F32_SIGN_BIT), F32)
            l1p = jnp.log(1.0 + jnp.exp2(neg_abs)) * LOG2E
            log_beta = jnp.minimum(z, 0.0) - l1p
            log_keep = log_beta - z
            if masked:
                log_keep = jnp.where(past, log_keep, 0.0)
            keep16 = log_keep.astype(BF16)
            n_c = nc // blk
            after = [None] * n_c
            for c in reversed(range(n_c)):
                cs = slice(c * blk, (c + 1) * blk)
                raw = _dot(keep16[:, cs], upper)
                after[c] = raw + _lane_tile(run, blk // LANES)
                run = run + (raw[:, 0:1] + log_keep[:, c * blk:c * blk + 1])
            a = jnp.exp2(log_beta + _lane_cat(after))
            if masked:
                a = jnp.where(past, a, 0.0)
            acc_ref[h, rs] += _dot(a.astype(BF16), vt)
            run_ref[h, rs] = run

    def alive():
        return jnp.max(run_ref[...]) > SB_RUN_FLOOR

    run_ref[...] = jnp.zeros_like(run_ref)
    acc_ref[...] = jnp.zeros_like(acc_ref)
    logits(qi, za_ref, 0, tq, 0, half)
    logits(qi, za_ref, half, half, half, half)
    logits(qi - 1, zb_ref, 0, tq, half, half)
    update_rows(qi, za_ref, 0, half, 0, half, True)
    update_rows(qi, za_ref, half, half, 0, tq, True)

    def step(carry):
        j, _ = carry
        t = qi - 1 - j
        logits(t, za_ref, 0, tq, 0, half)
        update_rows(t, zb_ref, 0, tq, half, half, False)

        @pl.when(alive())
        def _():
            logits(t - 1, zb_ref, 0, tq, half, half)
            update_rows(t, za_ref, 0, tq, 0, half, False)

        return j + 1, alive()

    lax.while_loop(lambda c: (c[0] < qi) & c[1], step, (jnp.int32(0), alive()))

    o_ref[0] = jnp.where(lane < HEAD_DIM, acc_ref[0], acc_ref[1]).astype(o_ref.dtype)


def _sb(q, k, v):
    B, L, _ = q.shape
    tq = min(T_ATT, L)
    return pl.pallas_call(
        functools.partial(_sb_kernel, tq=tq, blk=min(SB_BLOCK, tq // 2)),
        grid=(B, 2, L // tq),
        in_specs=[
            pl.BlockSpec((1, tq, 2 * LANES), lambda b, p, i: (b, i, p)),
            pl.BlockSpec((1, L, 2 * LANES), lambda b, p, i: (b, 0, p)),
            pl.BlockSpec((1, L, LANES), lambda b, p, i: (b, 0, p)),
        ],
        out_specs=pl.BlockSpec((1, tq, LANES), lambda b, p, i: (b, i, p)),
        out_shape=jax.ShapeDtypeStruct((B, L, D_BRANCH), BF16),
        scratch_shapes=[pltpu.VMEM((2, tq, tq), F32), pltpu.VMEM((2, tq, tq), F32),
                        pltpu.VMEM((2, tq, LANES), F32), pltpu.VMEM((2, tq, LANES), F32)],
        compiler_params=_cparams(("parallel", "parallel", "arbitrary")),
        name="stickbreak",
    )(q, k, v)


def _conv_kernel(ab_ref, w_ref, b_ref, g_ref, beta_ref, o_ref, pad_ref, sh_ref, *, lc):
    li = pl.program_id(1)

    @pl.when(li == 0)
    def _():
        pad_ref[0:CONV_HALO, :] = jnp.zeros((CONV_HALO, D_BRANCH), F32)

    ab = ab_ref[0]
    pad_ref[CONV_HALO:CONV_HALO + lc, :] = ab[:, 0:D_BRANCH] * _sigmoid(ab[:, D_BRANCH:2 * D_BRANCH])
    span = lc + CONV_HALO - SUBLANES
    for ph in range(1, SUBLANES):
        sh_ref[ph - 1] = pad_ref[ph:ph + span, :]
    off = CONV_HALO - (CONV_WIDTH - 1)
    acc = jnp.zeros((lc, D_BRANCH), F32) + b_ref[...]
    for j in range(CONV_WIDTH):
        ph, base = (off + j) % SUBLANES, (off + j) // SUBLANES * SUBLANES
        tap = pad_ref[base:base + lc, :] if ph == 0 else sh_ref[ph - 1, base:base + lc, :]
        acc = acc + w_ref[j:j + 1, :] * tap
    pad_ref[0:CONV_HALO, :] = pad_ref[lc:lc + CONV_HALO, :]
    mu = jnp.mean(acc, axis=-1, keepdims=True)
    xc = acc - mu
    var = jnp.mean(xc * xc, axis=-1, keepdims=True)
    y = xc * lax.rsqrt(var + EPS) * g_ref[...] + beta_ref[...]
    o_ref[0] = (y * _sigmoid(y)).astype(o_ref.dtype)


def _conv(ab, w, b, g, beta):
    B, L, _ = ab.shape
    lc = min(LC_CONV, L)
    const = lambda b_, l: (0, 0)
    return pl.pallas_call(
        functools.partial(_conv_kernel, lc=lc),
        grid=(B, L // lc),
        in_specs=[
            pl.BlockSpec((1, lc, 2 * D_BRANCH), lambda b_, l: (b_, l, 0)),
            pl.BlockSpec((CONV_WIDTH, D_BRANCH), const),
            pl.BlockSpec((1, D_BRANCH), const),
            pl.BlockSpec((1, D_BRANCH), const),
            pl.BlockSpec((1, D_BRANCH), const),
        ],
        out_specs=pl.BlockSpec((1, lc, D_BRANCH), lambda b_, l: (b_, l, 0)),
        out_shape=jax.ShapeDtypeStruct((B, L, D_BRANCH), BF16),
        scratch_shapes=[pltpu.VMEM((CONV_HALO + lc, D_BRANCH), F32),
                        pltpu.VMEM((SUBLANES - 1, CONV_HALO + lc - SUBLANES, D_BRANCH), F32)],
        compiler_params=_cparams(("parallel", "arbitrary")),
        name="conv",
    )(ab, w, b, g, beta)


def _merge_kernel(x_ref, g_ref, ys5_ref, yfox_ref, yconv_ref, ysb_ref, wg_ref, wb_ref, wo_ref, o_ref):
    x = x_ref[0]
    D = x.shape[-1]
    h = _rmsnorm(x, g_ref[...]).astype(BF16)
    ys = (ys5_ref[0], yfox_ref[0], yconv_ref[0], ysb_ref[0])
    merged = None
    for n in range(4):
        gate = _sigmoid(_dot(h, wg_ref[:, n * D:(n + 1) * D]))
        term = gate * _dot(ys[n], wb_ref[n])
        merged = term if merged is None else merged + term
    o_ref[0] = x + _dot(merged.astype(BF16), wo_ref[...])


def _merge(x, g, ys5, yfox, yconv, ysb, wg, wb, wo):
    B, L, D = x.shape
    tm = min(TM_MERGE, L)
    c2 = lambda b, l: (0, 0)
    yspec = pl.BlockSpec((1, tm, D_BRANCH), lambda b, l: (b, l, 0))
    return pl.pallas_call(
        _merge_kernel,
        grid=(B, L // tm),
        in_specs=[
            pl.BlockSpec((1, tm, D), lambda b, l: (b, l, 0)),
            pl.BlockSpec((1, D), c2),
            yspec, yspec, yspec, yspec,
            pl.BlockSpec((D, 4 * D), c2),
            pl.BlockSpec((4, D_BRANCH, D), lambda b, l: (0, 0, 0)),
            pl.BlockSpec((D, D), c2),
        ],
        out_specs=pl.BlockSpec((1, tm, D), lambda b, l: (b, l, 0)),
        out_shape=jax.ShapeDtypeStruct((B, L, D), F32),
        compiler_params=_cparams(("parallel", "parallel")),
        name="merge",
    )(x, g, ys5, yfox, yconv, ysb, wg, wb, wo)


def _swiglu(h, w1_ref, w3_ref, w2_ref, lead, fc):
    acc = None
    for c in range(w1_ref.shape[-1] // fc):
        cs = slice(c * fc, (c + 1) * fc)
        a = _dot(h, w1_ref[lead + (slice(None), cs)])
        b = _dot(h, w3_ref[lead + (slice(None), cs)])
        part = _dot((a * _sigmoid(a) * b).astype(BF16), w2_ref[lead + (cs, slice(None))])
        acc = part if acc is None else acc + part
    return acc


def _ffn_kernel(x_ref, g_ref, w1_ref, w3_ref, w2_ref, o_ref, *, fc):
    x = x_ref[...]
    h = _rmsnorm(x, g_ref[...]).astype(BF16)
    o_ref[...] = x + _swiglu(h, w1_ref, w3_ref, w2_ref, (), fc)


def _ffn(x2, g, w1, w3, w2):
    T, D = x2.shape
    F = w1.shape[1]
    tm = min(TM_FFN, T)
    c2 = lambda i: (0, 0)
    return pl.pallas_call(
        functools.partial(_ffn_kernel, fc=FC_FFN),
        grid=(T // tm,),
        in_specs=[
            pl.BlockSpec((tm, D), lambda i: (i, 0)),
            pl.BlockSpec((1, D), c2),
            pl.BlockSpec((D, F), c2), pl.BlockSpec((D, F), c2), pl.BlockSpec((F, D), c2),
        ],
        out_specs=pl.BlockSpec((tm, D), lambda i: (i, 0)),
        out_shape=jax.ShapeDtypeStruct((T, D), F32),
        compiler_params=pltpu.CompilerParams(dimension_semantics=("parallel",),
                                             vmem_limit_bytes=VMEM_LIMIT_FFN),
        name="dense_ffn",
    )(x2, g, w1, w3, w2)


def _router_kernel(x_ref, g_ref, wr_ref, br_ref, meta_ref, cnt_ref, carry_ref, *, tm):
    i = pl.program_id(0)

    @pl.when(i == 0)
    def _():
        carry_ref[...] = jnp.zeros_like(carry_ref)

    h = _rmsnorm(x_ref[...], g_ref[...])
    logits = jnp.dot(h, wr_ref[...], preferred_element_type=F32, precision=lax.Precision.HIGHEST) + br_ref[...]
    lane = lax.broadcasted_iota(jnp.int32, logits.shape, 1).astype(F32)
    m1 = jnp.max(logits, axis=1, keepdims=True)
    i1 = jnp.min(jnp.where(logits == m1, lane, float(LANES)), axis=1, keepdims=True)
    rest = jnp.where(lane == i1, NEG_INF, logits)
    m2 = jnp.max(rest, axis=1, keepdims=True)
    i2 = jnp.min(jnp.where(rest == m2, lane, float(LANES)), axis=1, keepdims=True)
    e2 = jnp.exp(m2 - m1)
    p1 = 1.0 / (1.0 + e2)
    p2 = e2 * p1
    onehot = jnp.where(lane == i1, 1.0, 0.0) + jnp.where(lane == i2, 1.0, 0.0)
    rr = lax.broadcasted_iota(jnp.int32, (tm, tm), 0)
    cc = lax.broadcasted_iota(jnp.int32, (tm, tm), 1)
    before = jnp.where(cc < rr, 1.0, 0.0).astype(BF16)
    prefix = _dot(before, onehot.astype(BF16)) + carry_ref[...]
    r1 = jnp.sum(jnp.where(lane == i1, prefix, 0.0), axis=1, keepdims=True)
    r2 = jnp.sum(jnp.where(lane == i2, prefix, 0.0), axis=1, keepdims=True)
    meta = jnp.zeros_like(logits)
    for n, col in enumerate((i1, i2, p1, p2, r1, r2)):
        meta = jnp.where(lane == float(n), col, meta)
    meta_ref[...] = meta
    total = carry_ref[...] + jnp.sum(onehot, axis=0, keepdims=True)
    carry_ref[...] = total
    cnt_ref[...] = total


def _router(x2, g, wr, br):
    T, D = x2.shape
    tm = min(TM_FFN, T)
    c2 = lambda i: (0, 0)
    return pl.pallas_call(
        functools.partial(_router_kernel, tm=tm),
        grid=(T // tm,),
        in_specs=[
            pl.BlockSpec((tm, D), lambda i: (i, 0)),
            pl.BlockSpec((1, D), c2),
            pl.BlockSpec((D, LANES), c2),
            pl.BlockSpec((1, LANES), c2),
        ],
        out_specs=[pl.BlockSpec((tm, LANES), lambda i: (i, 0)), pl.BlockSpec((1, LANES), c2)],
        out_shape=[jax.ShapeDtypeStruct((T, LANES), F32), jax.ShapeDtypeStruct((1, LANES), F32)],
        scratch_shapes=[pltpu.VMEM((1, LANES), F32)],
        compiler_params=_cparams(("arbitrary",)),
        name="router",
    )(x2, g, wr, br)


def _to_slabs(x):
    return x.reshape(x.shape[0], SUBLANES, x.shape[1] // SUBLANES)


def _from_slabs(x):
    return x.reshape(x.shape[0], x.shape[1] * x.shape[2])


def _dispatch_kernel(dst_ref, ztile_ref, x_ref, xs_hbm, xt_ref, zero_ref, sem_ref, zsem_ref, *, tt, tm):
    i = pl.program_id(0)

    @pl.when(i == 0)
    def _():
        zero_ref[...] = jnp.zeros_like(zero_ref)
        for e in range(2 * N_EXPERTS):
            @pl.when(ztile_ref[e] >= 0)
            def _():
                start = pl.multiple_of(ztile_ref[e], tm)
                fill = pltpu.make_async_copy(zero_ref, xs_hbm.at[pl.ds(start, tm)], zsem_ref)
                fill.start()
                fill.wait()

    xt_ref[...] = _to_slabs(x_ref[...])

    def body(r, _):
        for k in range(2):
            pltpu.make_async_copy(xt_ref.at[r], xs_hbm.at[dst_ref[2 * (i * tt + r) + k]],
                                  sem_ref).start(priority=k)
        return 0

    lax.fori_loop(0, tt, body, 0, unroll=4)
    for _ in range(2):
        pltpu.make_async_copy(xt_ref, xs_hbm.at[pl.ds(0, tt)], sem_ref).wait()


def _dispatch(x2, dst, ztile, R):
    T, D = x2.shape
    tt = min(TT_MOE, T)
    slab = (SUBLANES, D // SUBLANES)
    grid_spec = pltpu.PrefetchScalarGridSpec(
        num_scalar_prefetch=2,
        grid=(T // tt,),
        in_specs=[pl.BlockSpec((tt, D), lambda i, d, z: (i, 0))],
        out_specs=pl.BlockSpec(memory_space=pl.ANY),
        scratch_shapes=[pltpu.VMEM((tt,) + slab, F32), pltpu.VMEM((TM_MOE,) + slab, F32),
                        pltpu.SemaphoreType.DMA(()), pltpu.SemaphoreType.DMA(())],
    )
    return pl.pallas_call(
        functools.partial(_dispatch_kernel, tt=tt, tm=TM_MOE),
        grid_spec=grid_spec,
        out_shape=jax.ShapeDtypeStruct((R,) + slab, F32),
        compiler_params=_cparams(("arbitrary",)),
        name="dispatch",
    )(dst, ztile, x2)


def _experts_kernel(texp_ref, nused_ref, xs_ref, g_ref, w1_ref, w3_ref, w2_ref, y_ref, *, fc):
    i = pl.program_id(0)

    @pl.when(i < nused_ref[0])
    def _():
        h = _rmsnorm(_from_slabs(xs_ref[...]), g_ref[...]).astype(BF16)
        y_ref[...] = _to_slabs(_swiglu(h, w1_ref, w3_ref, w2_ref, (0,), fc))

    @pl.when(i >= nused_ref[0])
    def _():
        y_ref[...] = jnp.zeros_like(y_ref)


def _experts(xs, g, tile_expert, n_used, w1, w3, w2):
    R = xs.shape[0]
    slab = xs.shape[1:]
    D, F = w1.shape[1:]
    tm = TM_MOE
    wspec = lambda shape: pl.BlockSpec(shape, lambda i, te, n: (te[i], 0, 0))
    grid_spec = pltpu.PrefetchScalarGridSpec(
        num_scalar_prefetch=2,
        grid=(R // tm,),
        in_specs=[
            pl.BlockSpec((tm,) + slab, lambda i, te, n: (jnp.minimum(i, n[0] - 1), 0, 0)),
            pl.BlockSpec((1, D), lambda i, te, n: (0, 0)),
            wspec((1, D, F)), wspec((1, D, F)), wspec((1, F, D)),
        ],
        out_specs=pl.BlockSpec((tm,) + slab, lambda i, te, n: (i, 0, 0)),
    )
    return pl.pallas_call(
        functools.partial(_experts_kernel, fc=FC_FFN),
        grid_spec=grid_spec,
        out_shape=jax.ShapeDtypeStruct((R,) + slab, F32),
        compiler_params=pltpu.CompilerParams(dimension_semantics=("arbitrary",),
                                             vmem_limit_bytes=VMEM_LIMIT_FFN),
        name="experts",
    )(tile_expert, n_used, xs, g, w1, w3, w2)


def _combine_kernel(dst_ref, x_ref, p_ref, y_hbm, o_ref, buf_ref, sem_ref, *, tc, n_tiles):
    i = pl.program_id(0)
    slot = lax.rem(i, 2)

    def issue(tile, s):
        base = tile * (2 * tc)

        def body(r, _):
            for k in range(2):
                pltpu.make_async_copy(y_hbm.at[dst_ref[base + 2 * r + k]], buf_ref.at[s, k, r],
                                      sem_ref.at[s]).start(priority=k)
            return 0

        lax.fori_loop(0, tc, body, 0, unroll=4)

    @pl.when(i == 0)
    def _():
        issue(0, 0)

    @pl.when(i + 1 < n_tiles)
    def _():
        issue(i + 1, 1 - slot)

    for k in range(2):
        pltpu.make_async_copy(y_hbm.at[pl.ds(0, tc)], buf_ref.at[slot, k], sem_ref.at[slot]).wait()
    o_ref[...] = (x_ref[...] + p_ref[:, 0:1] * _from_slabs(buf_ref[slot, 0])
                  + p_ref[:, 1:2] * _from_slabs(buf_ref[slot, 1]))


def _combine(x2, prob, y, dst):
    T, D = x2.shape
    tc = min(TC_MOE, T)
    n_tiles = T // tc
    grid_spec = pltpu.PrefetchScalarGridSpec(
        num_scalar_prefetch=1,
        grid=(n_tiles,),
        in_specs=[pl.BlockSpec((tc, D), lambda i, d: (i, 0)), pl.BlockSpec((tc, 2), lambda i, d: (i, 0)),
                  pl.BlockSpec(memory_space=pl.ANY)],
        out_specs=pl.BlockSpec((tc, D), lambda i, d: (i, 0)),
        scratch_shapes=[pltpu.VMEM((2, 2, tc) + y.shape[1:], F32), pltpu.SemaphoreType.DMA((2,))],
    )
    return pl.pallas_call(
        functools.partial(_combine_kernel, tc=tc, n_tiles=n_tiles),
        grid_spec=grid_spec,
        out_shape=jax.ShapeDtypeStruct((T, D), F32),
        compiler_params=_cparams(("arbitrary",)),
        name="combine",
    )(dst, x2, prob, y)


def _moe(x2, g, wr, br, w1, w3, w2):
    T, D = x2.shape
    tm = TM_MOE
    meta, cnt = _router(x2, g, wr, br)
    expert = meta[:, 0:2].astype(jnp.int32)
    prob = meta[:, 2:4]
    rank = meta[:, 4:6].astype(jnp.int32)
    counts = cnt[0, :N_EXPERTS].astype(jnp.int32)
    padded = (counts + (tm - 1)) // tm * tm
    ends = jnp.cumsum(padded)
    dst = ((ends - padded)[expert] + rank).reshape(-1)
    R = 2 * T + N_EXPERTS * tm
    tile_start = jnp.arange(R // tm, dtype=jnp.int32) * tm
    tile_expert = jnp.minimum(jnp.searchsorted(ends, tile_start, side="right"), N_EXPERTS - 1).astype(jnp.int32)
    n_used = (ends[-1] // tm).astype(jnp.int32).reshape(1)
    last_tile = jnp.where(padded > 0, ends - tm, -1)
    tail = ends[-1] + jnp.arange(N_EXPERTS, dtype=jnp.int32) * tm
    ztile = jnp.concatenate([last_tile, jnp.where(tail < R, tail, -1)]).astype(jnp.int32)
    xs = _dispatch(x2, dst, ztile, R)
    y = _experts(xs, g, tile_expert, n_used, w1, w3, w2)
    return _combine(x2, prob, y, dst)


def _s5_params(lam_re, lam_im, log_dt, b_re, b_im, c_re, c_im):
    G, P, H = S5_GROUPS, S5_STATE, S5_GROUP
    dt = jnp.exp(log_dt)[:, None]
    mag = jnp.exp(lam_re * dt)
    ab_re = mag * jnp.cos(lam_im * dt)
    ab_im = mag * jnp.sin(lam_im * dt)
    nr, ni = ab_re - 1.0, ab_im
    den = lam_re * lam_re + lam_im * lam_im
    k_re = (nr * lam_re + ni * lam_im) / den
    k_im = (ni * lam_re - nr * lam_im) / den
    bb_re = k_re[..., None] * b_re - k_im[..., None] * b_im
    bb_im = k_re[..., None] * b_im + k_im[..., None] * b_re
    eye = jnp.eye(G, dtype=F32)
    bm_re = jnp.einsum("gph,gk->ghkp", bb_re, eye).reshape(G * H, G * P)
    bm_im = jnp.einsum("gph,gk->ghkp", bb_im, eye).reshape(G * H, G * P)
    bmat = jnp.concatenate([bm_re, bm_im], axis=1).astype(BF16)
    cm_re = jnp.einsum("ghp,gk->kpgh", c_re, eye).reshape(G * P, G * H)
    cm_im = jnp.einsum("ghp,gk->kpgh", c_im, eye).reshape(G * P, G * H)
    cmat = jnp.concatenate([cm_re, -cm_im], axis=0).astype(BF16)
    a = jnp.stack([ab_re.reshape(-1), ab_im.reshape(-1)], axis=0)
    return bmat, cmat, a


def kernel(x, g_mix, w_in, b_forget, fox_q_gain, fox_k_gain, sb_q_gain, sb_k_gain, s5_lam_re, s5_lam_im, s5_log_dt, s5_b_re, s5_b_im, s5_c_re, s5_c_im, s5_d, s5_w_glu, conv_w, conv_b, conv_ln_g, conv_ln_b, w_branch, w_out, g_ffn, ffn_w1, ffn_w3, ffn_w2, router_w, router_b, moe_w1, moe_w3, moe_w2):
    B, L, D = x.shape
    depth = g_mix.shape[0]
    assert B == SUBLANES, "the S5 scan lays the batch along the sublanes"
    n_qkv = 3 * D_BRANCH
    o_f = D_BRANCH + n_qkv
    o_c = o_f + N_HEADS
    o_g = o_c + 2 * D_BRANCH + n_qkv
    scale = 1.0 / math.sqrt(HEAD_DIM)
    sel_np, const_np = _forget_feature_tables()
    sel = jnp.asarray(sel_np, BF16)
    const = jnp.asarray(const_np, F32)

    w_a = w_in[:, :, :o_f].astype(BF16)
    w_b = w_in[:, :, o_c:o_g].astype(BF16)
    w_f = jnp.pad(w_in[:, :, o_f:o_c], ((0, 0), (0, 0), (0, LANES - N_HEADS))).astype(BF16)
    w_g = w_in[:, :, o_g:].astype(BF16)

    for i in range(depth):
        bfg = jnp.pad(b_forget[i], (0, LANES - N_HEADS)).reshape(1, LANES)
        gains = jnp.stack([jnp.tile(fox_q_gain[i] * (scale * LOG2E), N_HEADS), jnp.tile(fox_k_gain[i], N_HEADS),
                           jnp.tile(sb_q_gain[i] * (scale * LOG2E), N_HEADS), jnp.tile(sb_k_gain[i], N_HEADS)],
                          axis=0)
        us5, qf, kf, vf, conv_ab, qs, ks, vs = _inproj(x, g_mix[i].reshape(1, D), w_a[i], w_b[i], w_f[i], bfg,
                                                       gains, sel, const)

        bmat, cmat, a = _s5_params(s5_lam_re[i], s5_lam_im[i], s5_log_dt[i], s5_b_re[i], s5_b_im[i],
                                   s5_c_re[i], s5_c_im[i])
        ys5 = _s5(us5, bmat, cmat, a, s5_d[i].reshape(1, D_BRANCH), s5_w_glu[i].astype(BF16))
        yfox = _fox(qf, kf, vf)
        yconv = _conv(conv_ab, conv_w[i], conv_b[i].reshape(1, -1), conv_ln_g[i].reshape(1, -1),
                      conv_ln_b[i].reshape(1, -1))
        ysb = _sb(qs, ks, vs)

        x = _merge(x, g_mix[i].reshape(1, D), ys5, yfox, yconv, ysb, w_g[i],
                   w_branch[i].astype(BF16), w_out[i].astype(BF16))

        j = i // 2
        gf = g_ffn[i].reshape(1, D)
        if i % 2 == 0:
            x2 = _ffn(x.reshape(B * L, D), gf, ffn_w1[j].astype(BF16), ffn_w3[j].astype(BF16),
                      ffn_w2[j].astype(BF16))
        else:
            wr = jnp.pad(router_w[j], ((0, 0), (0, LANES - N_EXPERTS)))
            br = jnp.pad(router_b[j], (0, LANES - N_EXPERTS), constant_values=NO_EXPERT_LOGIT).reshape(1, LANES)
            x2 = _moe(x.reshape(B * L, D), gf, wr, br, moe_w1[j].astype(BF16), moe_w3[j].astype(BF16),
                      moe_w2[j].astype(BF16))
        x = x2.reshape(B, L, D)
    return x
```

```python
import functools
import math

import numpy as np

import jax
import jax.numpy as jnp
from jax import lax
from jax.experimental import pallas as pl
from jax.experimental.pallas import tpu as pltpu

F32 = jnp.float32
BF16 = jnp.bfloat16
EPS = 1e-6
LOG2E = math.log2(math.e)

D_BRANCH = 256
HEAD_DIM = 64
N_HEADS = D_BRANCH // HEAD_DIM
S5_GROUP = 16
S5_GROUPS = D_BRANCH // S5_GROUP
S5_STATE = 64
N_STATE = S5_GROUPS * S5_STATE
CONV_WIDTH = 31
CONV_HALO = 32
N_EXPERTS = 8
LANES = 128
SUBLANES = 8
VMEM_LIMIT = 56 * 1024 * 1024
D_SPREAD = N_HEADS * LANES
N_SPLIT = 3

TM_PROJ = 512
TM_MERGE = 512
MERGE_COL_PARTS = 2
TM_FFN = 512
T_ATT = 512
T_FOX = 1024
SB_BLOCK = 256
SB_RUN_FLOOR = -160.0
LC_S5 = 128
LC_CONV = 512
TM_MOE = 512
FC_FFN = 256
TT_MOE = 1024
TC_MOE = 512
VMEM_LIMIT_FFN = 60 * 1024 * 1024

NEG_INF = float("-inf")
F32_SIGN_BIT = 0x80000000
NO_EXPERT_LOGIT = -1e30


def _cparams(sem):
    return pltpu.CompilerParams(dimension_semantics=sem, vmem_limit_bytes=VMEM_LIMIT)


def _dot(a, b):
    return jnp.dot(a, b, preferred_element_type=F32)


def _dot_nt(a, b):
    return lax.dot_general(a, b, (((1,), (1,)), ((), ())), preferred_element_type=F32)


def _rmsnorm(x, g):
    ms = jnp.mean(x * x, axis=-1, keepdims=True)
    return x * lax.rsqrt(ms + EPS) * g


def _sigmoid(x):
    return 1.0 / (1.0 + jnp.exp(-x))


def _log_sigmoid(x):
    return jnp.minimum(x, 0.0) - jnp.log(1.0 + jnp.exp(-jnp.abs(x)))


def _lane_cat(parts):
    return parts[0] if len(parts) == 1 else jnp.concatenate(parts, axis=1)


def _lane_tile(x, n):
    return _lane_cat([x] * n)


def _forget_feature_tables():
    sel = np.zeros((LANES, 2 * D_SPREAD), np.float32)
    const = np.zeros((1, 2 * D_SPREAD), np.float32)
    for h in range(N_HEADS):
        for j in range(N_SPLIT):
            sel[j * N_HEADS + h, h * LANES + HEAD_DIM + j] = 1.0
            sel[j * N_HEADS + h, D_SPREAD + h * LANES + HEAD_DIM + N_SPLIT + j] = -1.0
            const[0, h * LANES + HEAD_DIM + N_SPLIT + j] = 1.0
            const[0, D_SPREAD + h * LANES + HEAD_DIM + j] = 1.0
    return sel, const


def _inproj_kernel(x_ref, g_ref, wa_ref, wb_ref, wf_ref, bf_ref, gain_ref, sel_ref, const_ref,
                   us5_ref, qf_ref, kf_ref, vf_ref, conv_ref, qs_ref, ks_ref, vs_ref, carry_ref, *, tm):
    li = pl.program_id(1)
    x = x_ref[0]
    h = _rmsnorm(x, g_ref[...]).astype(BF16)
    lane = lax.broadcasted_iota(jnp.int32, (tm, LANES), 1)

    r = lax.broadcasted_iota(jnp.int32, (D_BRANCH, D_BRANCH), 0) // HEAD_DIM
    c = lax.broadcasted_iota(jnp.int32, (D_BRANCH, D_BRANCH), 1) // HEAD_DIM
    ones_bd = jnp.where(r == c, 1.0, 0.0).astype(BF16)

    def qknorm(t, gi):
        ss = _dot((t * t).astype(BF16), ones_bd)
        return t * lax.rsqrt(ss * (1.0 / HEAD_DIM) + EPS) * gain_ref[gi:gi + 1, :]

    def spread(t, fill):
        blocks = []
        for hp in range(2):
            pair = t[:, hp * LANES:(hp + 1) * LANES]
            blocks += [pair, pltpu.roll(pair, HEAD_DIM, axis=1)]
        out = [jnp.where(lane < HEAD_DIM, blocks[n], fill(n)) for n in range(N_HEADS)]
        return jnp.concatenate(out, axis=1).astype(BF16)

    @pl.when(li == 0)
    def _():
        carry_ref[...] = jnp.zeros_like(carry_ref)

    lf = _log_sigmoid(_dot(h, wf_ref[...]) + bf_ref[...]) * LOG2E
    rr = lax.broadcasted_iota(jnp.int32, (tm, tm), 0)
    cc = lax.broadcasted_iota(jnp.int32, (tm, tm), 1)
    tri = jnp.where(cc <= rr, 1.0, 0.0).astype(BF16)

    def split3(v):
        hi = v.astype(BF16).astype(F32)
        r1 = v - hi
        mid = r1.astype(BF16).astype(F32)
        lo = (r1 - mid).astype(BF16).astype(F32)
        return hi, mid, lo

    hi, mid, lo = split3(lf)
    cum = _dot(tri, hi.astype(BF16)) + _dot(tri, mid.astype(BF16)) + _dot(tri, lo.astype(BF16)) + carry_ref[...]
    carry_ref[...] = cum[tm - 1:tm, :]
    hi, mid, lo = split3(jnp.where(lane < N_HEADS, cum, 0.0))
    packed = hi + pltpu.roll(mid, N_HEADS, axis=1) + pltpu.roll(lo, 2 * N_HEADS, axis=1)
    feat = _dot(packed.astype(BF16), sel_ref[...]) + const_ref[...]

    def feat_q(n):
        return feat[:, n * LANES:(n + 1) * LANES]

    def feat_k(n):
        return feat[:, D_SPREAD + n * LANES:D_SPREAD + (n + 1) * LANES]

    zero = lambda n: 0.0
    one = lambda n: 1.0

    pa = _dot(h, wa_ref[...])
    us5_ref[0] = pa[:, 0:256]
    qf_ref[0] = spread(qknorm(pa[:, 256:512], 0), feat_q)
    kf_ref[0] = spread(qknorm(pa[:, 512:768], 1), feat_k)
    vf_ref[0] = spread(pa[:, 768:1024], one)
    pb = _dot(h, wb_ref[...])
    conv_ref[0] = pb[:, 0:512]
    qs_ref[0] = spread(qknorm(pb[:, 512:768], 2), zero)
    ks_ref[0] = spread(qknorm(pb[:, 768:1024], 3), zero)
    vs_ref[0] = pb[:, 1024:1280].astype(BF16)


def _inproj(x, g, wa, wb, wf, bfg, gains, sel, const):
    B, L, D = x.shape
    tm = min(TM_PROJ, L)
    c2 = lambda b, l: (0, 0)
    row_spec = lambda n: pl.BlockSpec((1, tm, n), lambda b, l: (b, l, 0))
    return pl.pallas_call(
        functools.partial(_inproj_kernel, tm=tm),
        grid=(B, L // tm),
        in_specs=[
            row_spec(D),
            pl.BlockSpec((1, D), c2),
            pl.BlockSpec(wa.shape, c2),
            pl.BlockSpec(wb.shape, c2),
            pl.BlockSpec((D, LANES), c2),
            pl.BlockSpec((1, LANES), c2),
            pl.BlockSpec((4, D_BRANCH), c2),
            pl.BlockSpec(sel.shape, c2),
            pl.BlockSpec(const.shape, c2),
        ],
        out_specs=[
            row_spec(D_BRANCH),
            row_spec(D_SPREAD), row_spec(D_SPREAD), row_spec(D_SPREAD),
            row_spec(2 * D_BRANCH),
            row_spec(D_SPREAD), row_spec(D_SPREAD), row_spec(D_BRANCH),
        ],
        out_shape=[
            jax.ShapeDtypeStruct((B, L, D_BRANCH), F32),
            jax.ShapeDtypeStruct((B, L, D_SPREAD), BF16),
            jax.ShapeDtypeStruct((B, L, D_SPREAD), BF16),
            jax.ShapeDtypeStruct((B, L, D_SPREAD), BF16),
            jax.ShapeDtypeStruct((B, L, 2 * D_BRANCH), F32),
            jax.ShapeDtypeStruct((B, L, D_SPREAD), BF16),
            jax.ShapeDtypeStruct((B, L, D_SPREAD), BF16),
            jax.ShapeDtypeStruct((B, L, D_BRANCH), BF16),
        ],
        scratch_shapes=[pltpu.VMEM((1, LANES), F32)],
        compiler_params=_cparams(("parallel", "arbitrary")),
        name="inproj",
    )(x, g, wa, wb, wf, bfg, gains, sel, const)


def _s5_kernel(u_ref, bmat_ref, cmat_ref, a_ref, d_ref, wglu_ref, y_ref, utb_ref, bu_ref, st_ref, *, lc, nb):
    ci = pl.program_id(0)

    @pl.when(ci == 0)
    def _():
        st_ref[...] = jnp.zeros_like(st_ref)

    n_half = D_BRANCH // LANES
    for b in range(nb):
        for j in range(n_half):
            utb_ref.at[j][pl.ds(b, lc, stride=nb), :] = u_ref[b, :, j * LANES:(j + 1) * LANES]
    u = jnp.concatenate([utb_ref[j] for j in range(n_half)], axis=1)
    bu_ref[...] = _dot(u.astype(BF16), bmat_ref[...])

    a_re = jnp.broadcast_to(a_ref[0:1, :], (nb, N_STATE))
    a_im = jnp.broadcast_to(a_ref[1:2, :], (nb, N_STATE))

    def step(t, carry):
        s_re, s_im = carry
        r0 = pl.multiple_of(t * nb, nb)
        b_re = bu_ref[pl.ds(r0, nb), 0:N_STATE]
        b_im = bu_ref[pl.ds(r0, nb), N_STATE:2 * N_STATE]
        n_re = a_re * s_re - a_im * s_im + b_re
        n_im = a_re * s_im + a_im * s_re + b_im
        bu_ref[pl.ds(r0, nb), 0:N_STATE] = n_re
        bu_ref[pl.ds(r0, nb), N_STATE:2 * N_STATE] = n_im
        return n_re, n_im

    s_re, s_im = lax.fori_loop(0, lc, step, (st_ref[:, 0:N_STATE], st_ref[:, N_STATE:2 * N_STATE]),
                               unroll=2)
    st_ref[:, 0:N_STATE] = s_re
    st_ref[:, N_STATE:2 * N_STATE] = s_im

    y = _dot(bu_ref[...].astype(BF16), cmat_ref[...]) + d_ref[...] * u
    y = jax.nn.gelu(y, approximate=True)
    y = y * _sigmoid(_dot(y.astype(BF16), wglu_ref[...]))
    for j in range(n_half):
        utb_ref[j] = y[:, j * LANES:(j + 1) * LANES]
    for b in range(nb):
        for j in range(n_half):
            y_ref[b, :, j * LANES:(j + 1) * LANES] = utb_ref.at[j][pl.ds(b, lc, stride=nb), :].astype(y_ref.dtype)


def _s5(u, bmat, cmat, a, d, wglu):
    nb, L, _ = u.shape
    lc = min(LC_S5, L)
    const = lambda c: (0, 0)
    return pl.pallas_call(
        functools.partial(_s5_kernel, lc=lc, nb=nb),
        grid=(L // lc,),
        in_specs=[
            pl.BlockSpec((nb, lc, D_BRANCH), lambda c: (0, c, 0)),
            pl.BlockSpec((D_BRANCH, 2 * N_STATE), const),
            pl.BlockSpec((2 * N_STATE, D_BRANCH), const),
            pl.BlockSpec((2, N_STATE), const),
            pl.BlockSpec((1, D_BRANCH), const),
            pl.BlockSpec((D_BRANCH, D_BRANCH), const),
        ],
        out_specs=pl.BlockSpec((nb, lc, D_BRANCH), lambda c: (0, c, 0)),
        out_shape=jax.ShapeDtypeStruct((nb, L, D_BRANCH), BF16),
        scratch_shapes=[pltpu.VMEM((D_BRANCH // LANES, lc * nb, LANES), F32), pltpu.VMEM((lc * nb, 2 * N_STATE), F32),
                        pltpu.VMEM((nb, 2 * N_STATE), F32)],
        compiler_params=_cparams(("arbitrary",)),
        name="s5",
    )(u, bmat, cmat, a, d, wglu)


def _fox_kernel(q_ref, k_ref, v_ref, o_ref, sa_ref, sb_ref, m_ref, acc_ref, *, tq):
    qi = pl.program_id(2)
    lane = lax.broadcasted_iota(jnp.int32, (tq, LANES), 1)

    def logits(ki, s_ref):
        k0 = pl.multiple_of(ki * tq, tq)
        for h in range(2):
            hs = slice(h * LANES, (h + 1) * LANES)
            s_ref[h] = _dot_nt(q_ref[0, :, hs], k_ref[0, pl.ds(k0, tq), hs])

    def update_rows(ki, s_ref, r0, nr, nc, masked):
        k0 = pl.multiple_of(ki * tq, tq)
        rs = slice(r0, r0 + nr)
        for h in range(2):
            hs = slice(h * LANES, (h + 1) * LANES)
            s = s_ref[h, rs, 0:nc]
            if masked:
                row = lax.broadcasted_iota(jnp.int32, (nr, nc), 0) + r0
                col = lax.broadcasted_iota(jnp.int32, (nr, nc), 1)
                s = jnp.where(col <= row, s, NEG_INF)
            m = m_ref[h, rs]
            m_new = jnp.maximum(m, jnp.max(s, axis=1, keepdims=True))
            p = jnp.exp2(s - _lane_tile(m_new, nc // LANES))
            acc_ref[h, rs] = (jnp.exp2(m - m_new) * acc_ref[h, rs]
                              + _dot(p.astype(BF16), v_ref[0, pl.ds(k0, nc), hs]))
            m_ref[h, rs] = m_new

    def update(ki, s_ref, diag):
        if diag:
            half = tq // 2
            update_rows(ki, s_ref, 0, half, half, True)
            update_rows(ki, s_ref, half, half, tq, True)
        else:
            update_rows(ki, s_ref, 0, tq, tq, False)

    m_ref[...] = jnp.full(m_ref.shape, NEG_INF, F32)
    acc_ref[...] = jnp.zeros_like(acc_ref)
    logits(0, sa_ref)

    def pair(j, _):
        logits(2 * j + 1, sb_ref)
        update(2 * j, sa_ref, False)
        logits(2 * j + 2, sa_ref)
        update(2 * j + 1, sb_ref, False)
        return 0

    lax.fori_loop(0, qi // 2, pair, 0)

    @pl.when(qi % 2 == 0)
    def _():
        update(qi, sa_ref, True)

    @pl.when(qi % 2 == 1)
    def _():
        logits(qi, sb_ref)
        update(qi - 1, sa_ref, False)
        update(qi, sb_ref, True)

    res = [acc_ref[h] / pltpu.roll(acc_ref[h], HEAD_DIM, axis=1) for h in range(2)]
    o_ref[0] = jnp.where(lane < HEAD_DIM, res[0], pltpu.roll(res[1], HEAD_DIM, axis=1)).astype(o_ref.dtype)


def _fox(q, k, v):
    B, L, _ = q.shape
    tq = min(T_FOX, L)
    return pl.pallas_call(
        functools.partial(_fox_kernel, tq=tq),
        grid=(B, 2, L // tq),
        in_specs=[
            pl.BlockSpec((1, tq, 2 * LANES), lambda b, p, i: (b, i, p)),
            pl.BlockSpec((1, L, 2 * LANES), lambda b, p, i: (b, 0, p)),
            pl.BlockSpec((1, L, 2 * LANES), lambda b, p, i: (b, 0, p)),
        ],
        out_specs=pl.BlockSpec((1, tq, LANES), lambda b, p, i: (b, i, p)),
        out_shape=jax.ShapeDtypeStruct((B, L, D_BRANCH), BF16),
        scratch_shapes=[pltpu.VMEM((2, tq, tq), F32), pltpu.VMEM((2, tq, tq), F32),
                        pltpu.VMEM((2, tq, LANES), F32), pltpu.VMEM((2, tq, LANES), F32)],
        compiler_params=_cparams(("parallel", "parallel", "arbitrary")),
        name="fox",
    )(q, k, v)


def _sb_kernel(q_ref, k_ref, v_ref, o_ref, za_ref, zb_ref, run_ref, acc_ref, *, tq, blk):
    qi = pl.program_id(2)
    half = tq // 2
    lane = lax.broadcasted_iota(jnp.int32, (tq, LANES), 1)
    ur = lax.broadcasted_iota(jnp.int32, (blk, blk), 0)
    uc = lax.broadcasted_iota(jnp.int32, (blk, blk), 1)
    upper = jnp.where(ur > uc, 1.0, 0.0).astype(BF16)

    def logits(ki, z_ref, r0, nr, c0, nc):
        k0 = pl.multiple_of(jnp.maximum(ki, 0) * tq + c0, nc)
        for h in range(2):
            hs = slice(h * LANES, (h + 1) * LANES)
            z_ref[h, r0:r0 + nr, c0:c0 + nc] = _dot_nt(q_ref[0, r0:r0 + nr, hs], k_ref[0, pl.ds(k0, nc), hs])

    def update_rows(ki, z_ref, r0, nr, c0, nc, masked):
        k0 = pl.multiple_of(ki * tq + c0, nc)
        rs = slice(r0, r0 + nr)
        vt = v_ref[0, pl.ds(k0, nc), :]
        if masked:
            past = (lax.broadcasted_iota(jnp.int32, (nr, nc), 1) + c0
                    < lax.broadcasted_iota(jnp.int32, (nr, nc), 0) + r0)
        for h in range(2):
            run = run_ref[h, rs]
            z = z_ref[h, rs, c0:c0 + nc]
            neg_abs = pltpu.bitcast(pltpu.bitcast(z, jnp.uint32) | jnp.uint32(F32_SIGN_BIT), F32)
            l1p = jnp.log(1.0 + jnp.exp2(neg_abs)) * LOG2E
            log_beta = jnp.minimum(z, 0.0) - l1p
            log_keep = log_beta - z
            if masked:
                log_keep = jnp.where(past, log_keep, 0.0)
            keep16 = log_keep.astype(BF16)
            n_c = nc // blk
            after = [None] * n_c
            for c in reversed(range(n_c)):
                cs = slice(c * blk, (c + 1) * blk)
                raw = _dot(keep16[:, cs], upper)
                after[c] = raw + _lane_tile(run, blk // LANES)
                run = run + (raw[:, 0:1] + log_keep[:, c * blk:c * blk + 1])
            a = jnp.exp2(log_beta + _lane_cat(after))
            if masked:
                a = jnp.where(past, a, 0.0)
            acc_ref[h, rs] += _dot(a.astype(BF16), vt)
            run_ref[h, rs] = run

    def alive():
        return jnp.max(run_ref[...]) > SB_RUN_FLOOR

    run_ref[...] = jnp.zeros_like(run_ref)
    acc_ref[...] = jnp.zeros_like(acc_ref)
    logits(qi, za_ref, 0, tq, 0, half)
    logits(qi, za_ref, half, half, half, half)
    logits(qi - 1, zb_ref, 0, tq, half, half)
    update_rows(qi, za_ref, 0, half, 0, half, True)
    update_rows(qi, za_ref, half, half, 0, tq, True)

    def step(carry):
        j, _ = carry
        t = qi - 1 - j
        logits(t, za_ref, 0, tq, 0, half)
        update_rows(t, zb_ref, 0, tq, half, half, False)

        @pl.when(alive())
        def _():
            logits(t - 1, zb_ref, 0, tq, half, half)
            update_rows(t, za_ref, 0, tq, 0, half, False)

        return j + 1, alive()

    lax.while_loop(lambda c: (c[0] < qi) & c[1], step, (jnp.int32(0), alive()))

    o_ref[0] = jnp.where(lane < HEAD_DIM, acc_ref[0], acc_ref[1]).astype(o_ref.dtype)


def _sb(q, k, v):
    B, L, _ = q.shape
    tq = min(T_ATT, L)
    return pl.pallas_call(
        functools.partial(_sb_kernel, tq=tq, blk=min(SB_BLOCK, tq // 2)),
        grid=(B, 2, L // tq),
        in_specs=[
            pl.BlockSpec((1, tq, 2 * LANES), lambda b, p, i: (b, i, p)),
            pl.BlockSpec((1, L, 2 * LANES), lambda b, p, i: (b, 0, p)),
            pl.BlockSpec((1, L, LANES), lambda b, p, i: (b, 0, p)),
        ],
        out_specs=pl.BlockSpec((1, tq, LANES), lambda b, p, i: (b, i, p)),
        out_shape=jax.ShapeDtypeStruct((B, L, D_BRANCH), BF16),
        scratch_shapes=[pltpu.VMEM((2, tq, tq), F32), pltpu.VMEM((2, tq, tq), F32),
                        pltpu.VMEM((2, tq, LANES), F32), pltpu.VMEM((2, tq, LANES), F32)],
        compiler_params=_cparams(("parallel", "parallel", "arbitrary")),
        name="stickbreak",
    )(q, k, v)


def _conv_kernel(ab_ref, w_ref, b_ref, g_ref, beta_ref, o_ref, pad_ref, sh_ref, *, lc):
    li = pl.program_id(1)

    @pl.when(li == 0)
    def _():
        pad_ref[0:CONV_HALO, :] = jnp.zeros((CONV_HALO, D_BRANCH), F32)

    ab = ab_ref[0]
    pad_ref[CONV_HALO:CONV_HALO + lc, :] = ab[:, 0:D_BRANCH] * _sigmoid(ab[:, D_BRANCH:2 * D_BRANCH])
    span = lc + CONV_HALO - SUBLANES
    for ph in range(1, SUBLANES):
        sh_ref[ph - 1] = pad_ref[ph:ph + span, :]
    off = CONV_HALO - (CONV_WIDTH - 1)
    acc = jnp.zeros((lc, D_BRANCH), F32) + b_ref[...]
    for j in range(CONV_WIDTH):
        ph, base = (off + j) % SUBLANES, (off + j) // SUBLANES * SUBLANES
        tap = pad_ref[base:base + lc, :] if ph == 0 else sh_ref[ph - 1, base:base + lc, :]
        acc = acc + w_ref[j:j + 1, :] * tap
    pad_ref[0:CONV_HALO, :] = pad_ref[lc:lc + CONV_HALO, :]
    mu = jnp.mean(acc, axis=-1, keepdims=True)
    xc = acc - mu
    var = jnp.mean(xc * xc, axis=-1, keepdims=True)
    y = xc * lax.rsqrt(var + EPS) * g_ref[...] + beta_ref[...]
    o_ref[0] = (y * _sigmoid(y)).astype(o_ref.dtype)


def _conv(ab, w, b, g, beta):
    B, L, _ = ab.shape
    lc = min(LC_CONV, L)
    const = lambda b_, l: (0, 0)
    return pl.pallas_call(
        functools.partial(_conv_kernel, lc=lc),
        grid=(B, L // lc),
        in_specs=[
            pl.BlockSpec((1, lc, 2 * D_BRANCH), lambda b_, l: (b_, l, 0)),
            pl.BlockSpec((CONV_WIDTH, D_BRANCH), const),
            pl.BlockSpec((1, D_BRANCH), const),
            pl.BlockSpec((1, D_BRANCH), const),
            pl.BlockSpec((1, D_BRANCH), const),
        ],
        out_specs=pl.BlockSpec((1, lc, D_BRANCH), lambda b_, l: (b_, l, 0)),
        out_shape=jax.ShapeDtypeStruct((B, L, D_BRANCH), BF16),
        scratch_shapes=[pltpu.VMEM((CONV_HALO + lc, D_BRANCH), F32),
                        pltpu.VMEM((SUBLANES - 1, CONV_HALO + lc - SUBLANES, D_BRANCH), F32)],
        compiler_params=_cparams(("parallel", "arbitrary")),
        name="conv",
    )(ab, w, b, g, beta)


def _merge_kernel(x_ref, g_ref, ys5_ref, yfox_ref, yconv_ref, ysb_ref, wg_ref, wb_ref, wo_ref, o_ref):
    x = x_ref[0]
    D = x.shape[-1]
    h = _rmsnorm(x, g_ref[...]).astype(BF16)
    ys = (ys5_ref[0], yfox_ref[0], yconv_ref[0], ysb_ref[0])
    halves = []
    for c in range(MERGE_COL_PARTS):
        w = D // MERGE_COL_PARTS
        merged = None
        for n in range(4):
            gate = _sigmoid(_dot(h, wg_ref[:, n * D + c * w:n * D + (c + 1) * w]))
            term = gate * _dot(ys[n], wb_ref[n, :, c * w:(c + 1) * w])
            merged = term if merged is None else merged + term
        halves.append(merged.astype(BF16))
    o_ref[0] = x + _dot(_lane_cat(halves), wo_ref[...])


def _merge(x, g, ys5, yfox, yconv, ysb, wg, wb, wo):
    B, L, D = x.shape
    tm = min(TM_MERGE, L)
    c2 = lambda b, l: (0, 0)
    yspec = pl.BlockSpec((1, tm, D_BRANCH), lambda b, l: (b, l, 0))
    return pl.pallas_call(
        _merge_kernel,
        grid=(B, L // tm),
        in_specs=[
            pl.BlockSpec((1, tm, D), lambda b, l: (b, l, 0)),
            pl.BlockSpec((1, D), c2),
            yspec, yspec, yspec, yspec,
            pl.BlockSpec((D, 4 * D), c2),
            pl.BlockSpec((4, D_BRANCH, D), lambda b, l: (0, 0, 0)),
            pl.BlockSpec((D, D), c2),
        ],
        out_specs=pl.BlockSpec((1, tm, D), lambda b, l: (b, l, 0)),
        out_shape=jax.ShapeDtypeStruct((B, L, D), F32),
        compiler_params=_cparams(("parallel", "parallel")),
        name="merge",
    )(x, g, ys5, yfox, yconv, ysb, wg, wb, wo)


def _swiglu(h, w1_ref, w3_ref, w2_ref, lead, fc):
    acc = None
    for c in range(w1_ref.shape[-1] // fc):
        cs = slice(c * fc, (c + 1) * fc)
        a = _dot(h, w1_ref[lead + (slice(None), cs)])
        b = _dot(h, w3_ref[lead + (slice(None), cs)])
        part = _dot((a * _sigmoid(a) * b).astype(BF16), w2_ref[lead + (cs, slice(None))])
        acc = part if acc is None else acc + part
    return acc


def _ffn_kernel(x_ref, g_ref, w1_ref, w3_ref, w2_ref, o_ref, *, fc):
    x = x_ref[...]
    h = _rmsnorm(x, g_ref[...]).astype(BF16)
    o_ref[...] = x + _swiglu(h, w1_ref, w3_ref, w2_ref, (), fc)


def _ffn(x2, g, w1, w3, w2):
    T, D = x2.shape
    F = w1.shape[1]
    tm = min(TM_FFN, T)
    c2 = lambda i: (0, 0)
    return pl.pallas_call(
        functools.partial(_ffn_kernel, fc=FC_FFN),
        grid=(T // tm,),
        in_specs=[
            pl.BlockSpec((tm, D), lambda i: (i, 0)),
            pl.BlockSpec((1, D), c2),
            pl.BlockSpec((D, F), c2), pl.BlockSpec((D, F), c2), pl.BlockSpec((F, D), c2),
        ],
        out_specs=pl.BlockSpec((tm, D), lambda i: (i, 0)),
        out_shape=jax.ShapeDtypeStruct((T, D), F32),
        compiler_params=pltpu.CompilerParams(dimension_semantics=("parallel",),
                                             vmem_limit_bytes=VMEM_LIMIT_FFN),
        name="dense_ffn",
    )(x2, g, w1, w3, w2)


def _router_kernel(x_ref, g_ref, wr_ref, br_ref, meta_ref, cnt_ref, carry_ref, *, tm):
    i = pl.program_id(0)

    @pl.when(i == 0)
    def _():
        carry_ref[...] = jnp.zeros_like(carry_ref)

    h = _rmsnorm(x_ref[...], g_ref[...])
    logits = jnp.dot(h, wr_ref[...], preferred_element_type=F32, precision=lax.Precision.HIGHEST) + br_ref[...]
    lane = lax.broadcasted_iota(jnp.int32, logits.shape, 1).astype(F32)
    m1 = jnp.max(logits, axis=1, keepdims=True)
    i1 = jnp.min(jnp.where(logits == m1, lane, float(LANES)), axis=1, keepdims=True)
    rest = jnp.where(lane == i1, NEG_INF, logits)
    m2 = jnp.max(rest, axis=1, keepdims=True)
    i2 = jnp.min(jnp.where(rest == m2, lane, float(LANES)), axis=1, keepdims=True)
    e2 = jnp.exp(m2 - m1)
    p1 = 1.0 / (1.0 + e2)
    p2 = e2 * p1
    onehot = jnp.where(lane == i1, 1.0, 0.0) + jnp.where(lane == i2, 1.0, 0.0)
    rr = lax.broadcasted_iota(jnp.int32, (tm, tm), 0)
    cc = lax.broadcasted_iota(jnp.int32, (tm, tm), 1)
    before = jnp.where(cc < rr, 1.0, 0.0).astype(BF16)
    prefix = _dot(before, onehot.astype(BF16)) + carry_ref[...]
    r1 = jnp.sum(jnp.where(lane == i1, prefix, 0.0), axis=1, keepdims=True)
    r2 = jnp.sum(jnp.where(lane == i2, prefix, 0.0), axis=1, keepdims=True)
    meta = jnp.zeros_like(logits)
    for n, col in enumerate((i1, i2, p1, p2, r1, r2)):
        meta = jnp.where(lane == float(n), col, meta)
    meta_ref[...] = meta
    total = carry_ref[...] + jnp.sum(onehot, axis=0, keepdims=True)
    carry_ref[...] = total
    cnt_ref[...] = total


def _router(x2, g, wr, br):
    T, D = x2.shape
    tm = min(TM_FFN, T)
    c2 = lambda i: (0, 0)
    return pl.pallas_call(
        functools.partial(_router_kernel, tm=tm),
        grid=(T // tm,),
        in_specs=[
            pl.BlockSpec((tm, D), lambda i: (i, 0)),
            pl.BlockSpec((1, D), c2),
            pl.BlockSpec((D, LANES), c2),
            pl.BlockSpec((1, LANES), c2),
        ],
        out_specs=[pl.BlockSpec((tm, LANES), lambda i: (i, 0)), pl.BlockSpec((1, LANES), c2)],
        out_shape=[jax.ShapeDtypeStruct((T, LANES), F32), jax.ShapeDtypeStruct((1, LANES), F32)],
        scratch_shapes=[pltpu.VMEM((1, LANES), F32)],
        compiler_params=_cparams(("arbitrary",)),
        name="router",
    )(x2, g, wr, br)


def _to_slabs(x):
    return x.reshape(x.shape[0], SUBLANES, x.shape[1] // SUBLANES)


def _from_slabs(x):
    return x.reshape(x.shape[0], x.shape[1] * x.shape[2])


def _dispatch_kernel(dst_ref, ztile_ref, x_ref, xs_hbm, xt_ref, zero_ref, sem_ref, zsem_ref, *, tt, tm):
    i = pl.program_id(0)

    @pl.when(i == 0)
    def _():
        zero_ref[...] = jnp.zeros_like(zero_ref)
        for e in range(2 * N_EXPERTS):
            @pl.when(ztile_ref[e] >= 0)
            def _():
                start = pl.multiple_of(ztile_ref[e], tm)
                fill = pltpu.make_async_copy(zero_ref, xs_hbm.at[pl.ds(start, tm)], zsem_ref)
                fill.start()
                fill.wait()

    xt_ref[...] = _to_slabs(x_ref[...])

    def body(r, _):
        for k in range(2):
            pltpu.make_async_copy(xt_ref.at[r], xs_hbm.at[dst_ref[2 * (i * tt + r) + k]],
                                  sem_ref).start(priority=k)
        return 0

    lax.fori_loop(0, tt, body, 0, unroll=4)
    for _ in range(2):
        pltpu.make_async_copy(xt_ref, xs_hbm.at[pl.ds(0, tt)], sem_ref).wait()


def _dispatch(x2, dst, ztile, R):
    T, D = x2.shape
    tt = min(TT_MOE, T)
    slab = (SUBLANES, D // SUBLANES)
    grid_spec = pltpu.PrefetchScalarGridSpec(
        num_scalar_prefetch=2,
        grid=(T // tt,),
        in_specs=[pl.BlockSpec((tt, D), lambda i, d, z: (i, 0))],
        out_specs=pl.BlockSpec(memory_space=pl.ANY),
        scratch_shapes=[pltpu.VMEM((tt,) + slab, F32), pltpu.VMEM((TM_MOE,) + slab, F32),
                        pltpu.SemaphoreType.DMA(()), pltpu.SemaphoreType.DMA(())],
    )
    return pl.pallas_call(
        functools.partial(_dispatch_kernel, tt=tt, tm=TM_MOE),
        grid_spec=grid_spec,
        out_shape=jax.ShapeDtypeStruct((R,) + slab, F32),
        compiler_params=_cparams(("arbitrary",)),
        name="dispatch",
    )(dst, ztile, x2)


def _experts_kernel(texp_ref, nused_ref, xs_ref, g_ref, w1_ref, w3_ref, w2_ref, y_ref, *, fc):
    i = pl.program_id(0)

    @pl.when(i < nused_ref[0])
    def _():
        h = _rmsnorm(_from_slabs(xs_ref[...]), g_ref[...]).astype(BF16)
        y_ref[...] = _to_slabs(_swiglu(h, w1_ref, w3_ref, w2_ref, (0,), fc))

    @pl.when(i >= nused_ref[0])
    def _():
        y_ref[...] = jnp.zeros_like(y_ref)


def _experts(xs, g, tile_expert, n_used, w1, w3, w2):
    R = xs.shape[0]
    slab = xs.shape[1:]
    D, F = w1.shape[1:]
    tm = TM_MOE
    wspec = lambda shape: pl.BlockSpec(shape, lambda i, te, n: (te[i], 0, 0))
    grid_spec = pltpu.PrefetchScalarGridSpec(
        num_scalar_prefetch=2,
        grid=(R // tm,),
        in_specs=[
            pl.BlockSpec((tm,) + slab, lambda i, te, n: (jnp.minimum(i, n[0] - 1), 0, 0)),
            pl.BlockSpec((1, D), lambda i, te, n: (0, 0)),
            wspec((1, D, F)), wspec((1, D, F)), wspec((1, F, D)),
        ],
        out_specs=pl.BlockSpec((tm,) + slab, lambda i, te, n: (i, 0, 0)),
    )
    return pl.pallas_call(
        functools.partial(_experts_kernel, fc=FC_FFN),
        grid_spec=grid_spec,
        out_shape=jax.ShapeDtypeStruct((R,) + slab, F32),
        compiler_params=pltpu.CompilerParams(dimension_semantics=("arbitrary",),
                                             vmem_limit_bytes=VMEM_LIMIT_FFN),
        name="experts",
    )(tile_expert, n_used, xs, g, w1, w3, w2)


def _combine_kernel(dst_ref, x_ref, p_ref, y_hbm, o_ref, buf_ref, sem_ref, *, tc, n_tiles):
    i = pl.program_id(0)
    slot = lax.rem(i, 2)

    def issue(tile, s):
        base = tile * (2 * tc)

        def body(r, _):
            for k in range(2):
                pltpu.make_async_copy(y_hbm.at[dst_ref[base + 2 * r + k]], buf_ref.at[s, k, r],
                                      sem_ref.at[s]).start(priority=k)
            return 0

        lax.fori_loop(0, tc, body, 0, unroll=4)

    @pl.when(i == 0)
    def _():
        issue(0, 0)

    @pl.when(i + 1 < n_tiles)
    def _():
        issue(i + 1, 1 - slot)

    for k in range(2):
        pltpu.make_async_copy(y_hbm.at[pl.ds(0, tc)], buf_ref.at[slot, k], sem_ref.at[slot]).wait()
    o_ref[...] = (x_ref[...] + p_ref[:, 0:1] * _from_slabs(buf_ref[slot, 0])
                  + p_ref[:, 1:2] * _from_slabs(buf_ref[slot, 1]))


def _combine(x2, prob, y, dst):
    T, D = x2.shape
    tc = min(TC_MOE, T)
    n_tiles = T // tc
    grid_spec = pltpu.PrefetchScalarGridSpec(
        num_scalar_prefetch=1,
        grid=(n_tiles,),
        in_specs=[pl.BlockSpec((tc, D), lambda i, d: (i, 0)), pl.BlockSpec((tc, 2), lambda i, d: (i, 0)),
                  pl.BlockSpec(memory_space=pl.ANY)],
        out_specs=pl.BlockSpec((tc, D), lambda i, d: (i, 0)),
        scratch_shapes=[pltpu.VMEM((2, 2, tc) + y.shape[1:], F32), pltpu.SemaphoreType.DMA((2,))],
    )
    return pl.pallas_call(
        functools.partial(_combine_kernel, tc=tc, n_tiles=n_tiles),
        grid_spec=grid_spec,
        out_shape=jax.ShapeDtypeStruct((T, D), F32),
        compiler_params=_cparams(("arbitrary",)),
        name="combine",
    )(dst, x2, prob, y)


def _moe(x2, g, wr, br, w1, w3, w2):
    T, D = x2.shape
    tm = TM_MOE
    meta, cnt = _router(x2, g, wr, br)
    expert = meta[:, 0:2].astype(jnp.int32)
    prob = meta[:, 2:4]
    rank = meta[:, 4:6].astype(jnp.int32)
    counts = cnt[0, :N_EXPERTS].astype(jnp.int32)
    padded = (counts + (tm - 1)) // tm * tm
    ends = jnp.cumsum(padded)
    dst = ((ends - padded)[expert] + rank).reshape(-1)
    R = 2 * T + N_EXPERTS * tm
    tile_start = jnp.arange(R // tm, dtype=jnp.int32) * tm
    tile_expert = jnp.minimum(jnp.searchsorted(ends, tile_start, side="right"), N_EXPERTS - 1).astype(jnp.int32)
    n_used = (ends[-1] // tm).astype(jnp.int32).reshape(1)
    last_tile = jnp.where(padded > 0, ends - tm, -1)
    tail = ends[-1] + jnp.arange(N_EXPERTS, dtype=jnp.int32) * tm
    ztile = jnp.concatenate([last_tile, jnp.where(tail < R, tail, -1)]).astype(jnp.int32)
    xs = _dispatch(x2, dst, ztile, R)
    y = _experts(xs, g, tile_expert, n_used, w1, w3, w2)
    return _combine(x2, prob, y, dst)


def _s5_params(lam_re, lam_im, log_dt, b_re, b_im, c_re, c_im):
    G, P, H = S5_GROUPS, S5_STATE, S5_GROUP
    dt = jnp.exp(log_dt)[:, None]
    mag = jnp.exp(lam_re * dt)
    ab_re = mag * jnp.cos(lam_im * dt)
    ab_im = mag * jnp.sin(lam_im * dt)
    nr, ni = ab_re - 1.0, ab_im
    den = lam_re * lam_re + lam_im * lam_im
    k_re = (nr * lam_re + ni * lam_im) / den
    k_im = (ni * lam_re - nr * lam_im) / den
    bb_re = k_re[..., None] * b_re - k_im[..., None] * b_im
    bb_im = k_re[..., None] * b_im + k_im[..., None] * b_re
    eye = jnp.eye(G, dtype=F32)
    bm_re = jnp.einsum("gph,gk->ghkp", bb_re, eye).reshape(G * H, G * P)
    bm_im = jnp.einsum("gph,gk->ghkp", bb_im, eye).reshape(G * H, G * P)
    bmat = jnp.concatenate([bm_re, bm_im], axis=1).astype(BF16)
    cm_re = jnp.einsum("ghp,gk->kpgh", c_re, eye).reshape(G * P, G * H)
    cm_im = jnp.einsum("ghp,gk->kpgh", c_im, eye).reshape(G * P, G * H)
    cmat = jnp.concatenate([cm_re, -cm_im], axis=0).astype(BF16)
    a = jnp.stack([ab_re.reshape(-1), ab_im.reshape(-1)], axis=0)
    return bmat, cmat, a


def kernel(x, g_mix, w_in, b_forget, fox_q_gain, fox_k_gain, sb_q_gain, sb_k_gain, s5_lam_re, s5_lam_im, s5_log_dt, s5_b_re, s5_b_im, s5_c_re, s5_c_im, s5_d, s5_w_glu, conv_w, conv_b, conv_ln_g, conv_ln_b, w_branch, w_out, g_ffn, ffn_w1, ffn_w3, ffn_w2, router_w, router_b, moe_w1, moe_w3, moe_w2):
    B, L, D = x.shape
    depth = g_mix.shape[0]
    assert B == SUBLANES, "the S5 scan lays the batch along the sublanes"
    n_qkv = 3 * D_BRANCH
    o_f = D_BRANCH + n_qkv
    o_c = o_f + N_HEADS
    o_g = o_c + 2 * D_BRANCH + n_qkv
    scale = 1.0 / math.sqrt(HEAD_DIM)
    sel_np, const_np = _forget_feature_tables()
    sel = jnp.asarray(sel_np, BF16)
    const = jnp.asarray(const_np, F32)

    w_a = w_in[:, :, :o_f].astype(BF16)
    w_b = w_in[:, :, o_c:o_g].astype(BF16)
    w_f = jnp.pad(w_in[:, :, o_f:o_c], ((0, 0), (0, 0), (0, LANES - N_HEADS))).astype(BF16)
    w_g = w_in[:, :, o_g:].astype(BF16)

    for i in range(depth):
        bfg = jnp.pad(b_forget[i], (0, LANES - N_HEADS)).reshape(1, LANES)
        gains = jnp.stack([jnp.tile(fox_q_gain[i] * (scale * LOG2E), N_HEADS), jnp.tile(fox_k_gain[i], N_HEADS),
                           jnp.tile(sb_q_gain[i] * (scale * LOG2E), N_HEADS), jnp.tile(sb_k_gain[i], N_HEADS)],
                          axis=0)
        us5, qf, kf, vf, conv_ab, qs, ks, vs = _inproj(x, g_mix[i].reshape(1, D), w_a[i], w_b[i], w_f[i], bfg,
                                                       gains, sel, const)

        bmat, cmat, a = _s5_params(s5_lam_re[i], s5_lam_im[i], s5_log_dt[i], s5_b_re[i], s5_b_im[i],
                                   s5_c_re[i], s5_c_im[i])
        ys5 = _s5(us5, bmat, cmat, a, s5_d[i].reshape(1, D_BRANCH), s5_w_glu[i].astype(BF16))
        yfox = _fox(qf, kf, vf)
        yconv = _conv(conv_ab, conv_w[i], conv_b[i].reshape(1, -1), conv_ln_g[i].reshape(1, -1),
                      conv_ln_b[i].reshape(1, -1))
        ysb = _sb(qs, ks, vs)

        x = _merge(x, g_mix[i].reshape(1, D), ys5, yfox, yconv, ysb, w_g[i],
                   w_branch[i].astype(BF16), w_out[i].astype(BF16))

        j = i // 2
        gf = g_ffn[i].reshape(1, D)
        if i % 2 == 0:
            x2 = _ffn(x.reshape(B * L, D), gf, ffn_w1[j].astype(BF16), ffn_w3[j].astype(BF16),
                      ffn_w2[j].astype(BF16))
        else:
            wr = jnp.pad(router_w[j], ((0, 0), (0, LANES - N_EXPERTS)))
            br = jnp.pad(router_b[j], (0, LANES - N_EXPERTS), constant_values=NO_EXPERT_LOGIT).reshape(1, LANES)
            x2 = _moe(x.reshape(B * L, D), gf, wr, br, moe_w1[j].astype(BF16), moe_w3[j].astype(BF16),
                      moe_w2[j].astype(BF16))
        x = x2.reshape(B, L, D)
    return x
```
